```python
import jax, jax.numpy as jnp
from jax import lax
import numpy as np

D_MODEL = 2048
BATCH = 4
SEQ = 2048
DEPTH = 4
DEC_BATCH = 32
DEC_SEQ = 8
PAST_LEN = 16384
PAGE_SIZE = 128

N_Q_HEADS = 8
N_KV_HEADS = 2
HEAD_DIM = 128
Q_PER_KV = N_Q_HEADS // N_KV_HEADS
WINDOW = 128
ATTN_BLOCK = WINDOW
W_A = N_Q_HEADS * HEAD_DIM
KV_W = N_KV_HEADS * HEAD_DIM
W_B = 1024
N_LRU_BLOCKS = 8
LRU_BLOCK = W_B // N_LRU_BLOCKS
CONV_W = 4
LRU_C = 8.0
W_C = 1024
N_C_GROUPS = 8
C_GROUP = W_C // N_C_GROUPS
CHUNK = 128
N_BRANCH = 3
D_FF = 5632
N_EXPERTS = 8
TOP_K = 2
D_FF_EXPERT = 7168
N_DENSE = (DEPTH + 1) // 2
N_MOE = DEPTH // 2
EPS = 1e-6
S_Q = W_A
S_K = S_Q + KV_W
S_V = S_K + KV_W
S_LX = S_V + W_B
S_LG = S_LX + W_B
S_CU = S_LG + W_C
S_CV = S_CU + W_C
IN_WIDTH = S_CV + N_BRANCH * D_MODEL

kernel_name = 'griffin_gated_hybrid_swa_rglru_chunkmlp_moe_step'


def rms_norm(x, w):
    xf = x.astype(jnp.float32)
    y = xf * lax.rsqrt(jnp.mean(xf * xf, axis=-1, keepdims=True) + EPS)
    return (y * w.astype(jnp.float32)).astype(x.dtype)


def sink_attention(q, k, v, q_pos, k_pos, sinks):
    s = jnp.einsum('...qkgd,...skd->...kgqs', q, k).astype(jnp.float32) * (HEAD_DIM ** -0.5)
    dist = q_pos[..., :, None] - k_pos[..., None, :]
    ok = (dist >= 0) & (dist <= WINDOW) & (k_pos[..., None, :] >= 0)
    s = jnp.where(ok[..., None, None, :, :], s, -1e30)
    sink = jnp.broadcast_to(sinks.astype(jnp.float32).reshape(N_KV_HEADS, Q_PER_KV, 1, 1), s.shape[:-1] + (1,))
    p = jax.nn.softmax(jnp.concatenate([s, sink], axis=-1), axis=-1)[..., :-1]
    return jnp.einsum('...kgqs,...skd->...qkgd', p.astype(v.dtype), v)


def band_rows(k):
    B, S = k.shape[:2]
    nb = S // ATTN_BLOCK
    kb = k.reshape(B, nb, ATTN_BLOCK, *k.shape[2:])
    prev = jnp.concatenate([jnp.zeros_like(kb[:, :1]), kb[:, :-1]], axis=1)
    return jnp.concatenate([prev, kb], axis=2)


def attention_prompt(q, k, v, sinks):
    B, S = q.shape[:2]
    nb = S // ATTN_BLOCK
    qb = q.reshape(B, nb, ATTN_BLOCK, N_KV_HEADS, Q_PER_KV, HEAD_DIM)
    pos = jnp.arange(S, dtype=jnp.int32).reshape(nb, ATTN_BLOCK)
    kpos = jnp.concatenate([pos - ATTN_BLOCK, pos], axis=1)
    o = sink_attention(qb, band_rows(k), band_rows(v), pos, kpos, sinks)
    return o.reshape(B, S, W_A)


def attention_sample(q, k, v, k_win, v_win, sinks):
    B, T = q.shape[:2]
    k_all = jnp.concatenate([k_win.astype(k.dtype), k], axis=1)
    v_all = jnp.concatenate([v_win.astype(v.dtype), v], axis=1)
    q_pos = PAST_LEN + jnp.arange(T, dtype=jnp.int32)
    k_pos = PAST_LEN - WINDOW + jnp.arange(WINDOW + T, dtype=jnp.int32)
    o = sink_attention(q.reshape(B, T, N_KV_HEADS, Q_PER_KV, HEAD_DIM), k_all, v_all, q_pos, k_pos, sinks)
    return o.reshape(B, T, W_A), k_all[:, -WINDOW:], v_all[:, -WINDOW:]


def causal_conv(x, buf, w, b):
    T = x.shape[1]
    xp = jnp.concatenate([buf.astype(x.dtype), x], axis=1)
    y = sum(xp[:, j:j + T] * w[j] for j in range(CONV_W)) + b
    return y, xp[:, -(CONV_W - 1):]


def rg_lru(x, h0, w_a, b_a, w_x, b_x, lam):
    B, T, C = x.shape
    xb = x.reshape(B, T, N_LRU_BLOCKS, LRU_BLOCK)
    r = jax.nn.sigmoid((jnp.einsum('btnc,ncd->btnd', xb, w_a).reshape(B, T, C) + b_a).astype(jnp.float32))
    i = jax.nn.sigmoid((jnp.einsum('btnc,ncd->btnd', xb, w_x).reshape(B, T, C) + b_x).astype(jnp.float32))
    log_a = -LRU_C * r * jax.nn.softplus(-lam.astype(jnp.float32))
    a = jnp.exp(log_a)
    u = jnp.sqrt(-jnp.expm1(2.0 * log_a)) * i * x.astype(jnp.float32)

    def step(h, au):
        h = au[0] * h + au[1]
        return h, h

    hT, hs = lax.scan(step, h0.astype(jnp.float32), (jnp.swapaxes(a, 0, 1), jnp.swapaxes(u, 0, 1)))
    return jnp.swapaxes(hs, 0, 1).astype(x.dtype), hT.astype(h0.dtype)


def chunk_spatial_gating(u, v, v_norm_w, w_s, b_s):
    B, T, _ = u.shape
    vn = rms_norm(v, v_norm_w)
    nc = -(-T // CHUNK)
    vp = jnp.pad(vn, ((0, 0), (0, nc * CHUNK - T), (0, 0))).reshape(B, nc, CHUNK, N_C_GROUPS, C_GROUP)
    ws = jnp.tril(w_s)
    mixed = jnp.einsum('gts,bcsgd->bctgd', ws, vp) + jnp.swapaxes(b_s, 0, 1)[:, :, None]
    mixed = mixed.reshape(B, nc * CHUNK, W_C)[:, :T]
    return u * mixed, vn


def swiglu(h, w_gate, w_up, w_down):
    return (jax.nn.silu(h @ w_gate) * (h @ w_up)) @ w_down


def moe_swiglu(h, w_router, w_gate, w_up, w_down):
    logits = (h @ w_router).astype(jnp.float32)
    top_v, top_i = lax.top_k(logits, TOP_K)
    top_w = jax.nn.softmax(top_v, axis=-1)
    gate = jnp.sum(jax.nn.one_hot(top_i, N_EXPERTS, dtype=jnp.float32) * top_w[..., None], axis=-2)
    out = jnp.zeros_like(h)
    for e in range(N_EXPERTS):
        out = out + gate[..., e:e + 1].astype(h.dtype) * swiglu(h, w_gate[e], w_up[e], w_down[e])
    return out


def mixer_block(h, l, W, prompt, st):
    B, T, _ = h.shape
    z = h @ W['w_in'][l]
    q, k, v, lx, lg, cu, cv, gates = jnp.split(z, [S_Q, S_K, S_V, S_LX, S_LG, S_CU, S_CV], axis=-1)
    q = rms_norm(q.reshape(B, T, N_Q_HEADS, HEAD_DIM), W['q_norm_w'][l])
    k = rms_norm(k.reshape(B, T, N_KV_HEADS, HEAD_DIM), W['k_norm_w'][l])
    v = v.reshape(B, T, N_KV_HEADS, HEAD_DIM)
    if prompt:
        attn_o = attention_prompt(q, k, v, W['attn_sinks'][l])
        k_win, v_win = k[:, -WINDOW:], v[:, -WINDOW:]
        conv_buf = jnp.zeros((B, CONV_W - 1, W_B), h.dtype)
        h0 = jnp.zeros((B, W_B), h.dtype)
    else:
        attn_o, k_win, v_win = attention_sample(q, k, v, st['k'][l], st['v'][l], W['attn_sinks'][l])
        conv_buf = st['conv'][l]
        h0 = st['h'][l]
    xc, conv_new = causal_conv(lx, conv_buf, W['conv_w'][l], W['conv_b'][l])
    lru_y, h_new = rg_lru(xc, h0, W['lru_w_a'][l], W['lru_b_a'][l], W['lru_w_x'][l], W['lru_b_x'][l], W['lru_lambda'][l])
    lru_o = jax.nn.gelu(lg) * lru_y
    chunk_o, vn = chunk_spatial_gating(jax.nn.gelu(cu), jax.nn.gelu(cv), W['chunk_v_norm_w'][l], W['chunk_w_s'][l], W['chunk_b_s'][l])
    chunk_state = vn[:, ((T - 1) // CHUNK) * CHUNK:]
    g = jax.nn.sigmoid(gates.astype(jnp.float32)).astype(h.dtype).reshape(B, T, N_BRANCH, D_MODEL)
    m = (g[:, :, 0] * (attn_o @ W['w_branch_a'][l])
         + g[:, :, 1] * (lru_o @ W['w_branch_b'][l])
         + g[:, :, 2] * (chunk_o @ W['w_branch_c'][l]))
    return m @ W['w_out'][l], (k_win, v_win, h_new, conv_new, chunk_state)


def trunk(x, c, W, prompt, st):
    news = []
    for l in range(DEPTH):
        mod = (jax.nn.silu(c) @ W['w_ada'][l] + W['b_ada'][l])[:, None, :]
        sh1, sc1, g1, sh2, sc2, g2 = jnp.split(mod, 6, axis=-1)
        h = rms_norm(x, W['norm1_w'][l]) * (1 + sc1) + sh1
        m, ns = mixer_block(h, l, W, prompt, st)
        x = x + g1 * m
        h = rms_norm(x, W['norm2_w'][l]) * (1 + sc2) + sh2
        if l % 2 == 0:
            j = l // 2
            f = swiglu(h, W['ffn_w_gate'][j], W['ffn_w_up'][j], W['ffn_w_down'][j])
        else:
            j = l // 2
            f = moe_swiglu(h, W['moe_w_router'][j], W['moe_w_gate'][j], W['moe_w_up'][j], W['moe_w_down'][j])
        x = x + g2 * f
        news.append(ns)
    k_w, v_w, h_s, conv_s, chunk_s = [jnp.stack(s) for s in zip(*news)]
    return x, k_w, v_w, h_s, conv_s, chunk_s


def setup_inputs(seed: int = 0) -> dict:
    key = jax.random.key(seed)
    keys = iter(jax.random.split(key, 64))
    f32 = jnp.float32
    D = D_MODEL
    L = DEPTH

    def nrm(shape, scale):
        return jax.random.normal(next(keys), shape, f32) * scale

    def gain(shape):
        return 1.0 + nrm(shape, 0.02)

    a0 = jax.random.uniform(next(keys), (L, W_B), f32, 0.9, 0.999)
    s0 = a0 ** (1.0 / LRU_C)
    return {
        'x_prompt': nrm((BATCH, SEQ, D), 1.0),
        'x_sample': nrm((DEC_BATCH, DEC_SEQ, D), 1.0),
        'cache_k_win': nrm((L, DEC_BATCH, WINDOW, N_KV_HEADS, HEAD_DIM), 1.0),
        'cache_v_win': nrm((L, DEC_BATCH, WINDOW, N_KV_HEADS, HEAD_DIM), 1.0),
        'state_rglru_h': nrm((L, DEC_BATCH, W_B), 0.5),
        'state_conv': nrm((L, DEC_BATCH, CONV_W - 1, W_B), 1.0),
        'c_prompt': nrm((BATCH, D), 1.0),
        'c_sample': nrm((DEC_BATCH, D), 1.0),
        'norm1_w': gain((L, D)),
        'norm2_w': gain((L, D)),
        'w_ada': nrm((L, D, 6 * D), 0.5 * D ** -0.5),
        'b_ada': nrm((L, 6 * D), 0.02),
        'w_in': nrm((L, D, IN_WIDTH), D ** -0.5),
        'q_norm_w': gain((L, HEAD_DIM)),
        'k_norm_w': gain((L, HEAD_DIM)),
        'attn_sinks': nrm((L, N_Q_HEADS), 1.0),
        'conv_w': nrm((L, CONV_W, W_B), CONV_W ** -0.5),
        'conv_b': nrm((L, W_B), 0.02),
        'lru_w_a': nrm((L, N_LRU_BLOCKS, LRU_BLOCK, LRU_BLOCK), LRU_BLOCK ** -0.5),
        'lru_b_a': nrm((L, W_B), 0.02),
        'lru_w_x': nrm((L, N_LRU_BLOCKS, LRU_BLOCK, LRU_BLOCK), LRU_BLOCK ** -0.5),
        'lru_b_x': nrm((L, W_B), 0.02),
        'lru_lambda': jnp.log(s0) - jnp.log1p(-s0),
        'chunk_v_norm_w': gain((L, W_C)),
        'chunk_w_s': nrm((L, N_C_GROUPS, CHUNK, CHUNK), CHUNK ** -0.5),
        'chunk_b_s': gain((L, N_C_GROUPS, CHUNK)),
        'w_branch_a': nrm((L, W_A, D), W_A ** -0.5),
        'w_branch_b': nrm((L, W_B, D), W_B ** -0.5),
        'w_branch_c': nrm((L, W_C, D), W_C ** -0.5),
        'w_out': nrm((L, D, D), D ** -0.5),
        'ffn_w_gate': nrm((N_DENSE, D, D_FF), D ** -0.5),
        'ffn_w_up': nrm((N_DENSE, D, D_FF), D ** -0.5),
        'ffn_w_down': nrm((N_DENSE, D_FF, D), D_FF ** -0.5),
        'moe_w_router': nrm((N_MOE, D, N_EXPERTS), D ** -0.5),
        'moe_w_gate': nrm((N_MOE, N_EXPERTS, D, D_FF_EXPERT), D ** -0.5),
        'moe_w_up': nrm((N_MOE, N_EXPERTS, D, D_FF_EXPERT), D ** -0.5),
        'moe_w_down': nrm((N_MOE, N_EXPERTS, D_FF_EXPERT, D), D_FF_EXPERT ** -0.5),
    }


def reference(x_prompt, x_sample, cache_k_win, cache_v_win, state_rglru_h, state_conv, c_prompt, c_sample,
              norm1_w, norm2_w, w_ada, b_ada, w_in, q_norm_w, k_norm_w, attn_sinks, conv_w, conv_b,
              lru_w_a, lru_b_a, lru_w_x, lru_b_x, lru_lambda, chunk_v_norm_w, chunk_w_s, chunk_b_s,
              w_branch_a, w_branch_b, w_branch_c, w_out, ffn_w_gate, ffn_w_up, ffn_w_down,
              moe_w_router, moe_w_gate, moe_w_up, moe_w_down):
    W = {
        'norm1_w': norm1_w, 'norm2_w': norm2_w, 'w_ada': w_ada, 'b_ada': b_ada, 'w_in': w_in,
        'q_norm_w': q_norm_w, 'k_norm_w': k_norm_w, 'attn_sinks': attn_sinks,
        'conv_w': conv_w, 'conv_b': conv_b, 'lru_w_a': lru_w_a, 'lru_b_a': lru_b_a,
        'lru_w_x': lru_w_x, 'lru_b_x': lru_b_x, 'lru_lambda': lru_lambda,
        'chunk_v_norm_w': chunk_v_norm_w, 'chunk_w_s': chunk_w_s, 'chunk_b_s': chunk_b_s,
        'w_branch_a': w_branch_a, 'w_branch_b': w_branch_b, 'w_branch_c': w_branch_c, 'w_out': w_out,
        'ffn_w_gate': ffn_w_gate, 'ffn_w_up': ffn_w_up, 'ffn_w_down': ffn_w_down,
        'moe_w_router': moe_w_router, 'moe_w_gate': moe_w_gate, 'moe_w_up': moe_w_up, 'moe_w_down': moe_w_down,
    }
    st = {'k': cache_k_win, 'v': cache_v_win, 'h': state_rglru_h, 'conv': state_conv}
    y_prompt, k_win_p, v_win_p, h_p, conv_p, chunk_v_p = trunk(x_prompt, c_prompt, W, True, None)
    y_sample, k_win_s, v_win_s, h_s, conv_s, chunk_v_s = trunk(x_sample, c_sample, W, False, st)
    return (y_prompt, y_sample, k_win_p, v_win_p, h_p, conv_p, chunk_v_p,
            k_win_s, v_win_s, h_s, conv_s, chunk_v_s)
```

```python
import functools

import jax
import jax.numpy as jnp
from jax import lax
from jax.experimental import pallas as pl
from jax.experimental.pallas import tpu as pltpu

EPS = 1e-6
LRU_C = 8.0
PAST_LEN = 16384
TOP_K = 2
SUBLANES = 8
LANES = 128
V7X_VMEM_LIMIT = 56 * 1024 * 1024

F32 = jnp.float32
BF16 = jnp.bfloat16


def _pick(n, target, mult):
    best = None
    for t in range(mult, min(n, target) + 1, mult):
        if n % t == 0:
            best = t
    assert best is not None, (n, target, mult)
    return best


def _params(*sem):
    return pltpu.CompilerParams(dimension_semantics=sem, vmem_limit_bytes=V7X_VMEM_LIMIT)


def _rms(x, w):
    return x * lax.rsqrt(jnp.mean(x * x, axis=-1, keepdims=True) + EPS) * w


def _ada_kernel(c_ref, w_ref, b_ref, o_ref):
    a = jax.nn.silu(c_ref[...]).astype(BF16)
    o_ref[...] = jnp.dot(a, w_ref[...].astype(BF16), preferred_element_type=F32) + b_ref[...]


def _ada(c_all, w_ada, b_ada):
    L, D, W6 = w_ada.shape
    Bc = c_all.shape[0]
    tn = _pick(W6, 1024, LANES)
    return pl.pallas_call(
        _ada_kernel,
        grid=(L, W6 // tn),
        in_specs=[pl.BlockSpec((Bc, D), lambda l, j: (0, 0)),
                  pl.BlockSpec((None, D, tn), lambda l, j: (l, 0, j)),
                  pl.BlockSpec((None, 1, tn), lambda l, j: (l, 0, j))],
        out_specs=pl.BlockSpec((None, Bc, tn), lambda l, j: (l, 0, j)),
        out_shape=jax.ShapeDtypeStruct((L, Bc, W6), F32),
        compiler_params=_params("parallel", "parallel"),
        name="ada_mod",
    )(c_all, w_ada, b_ada.reshape(L, 1, W6))


def _resid_norm_kernel(*refs, n_prompt_tiles, has_resid, has_norm):
    refs = list(refs)
    x_ref = refs.pop(0)
    if has_resid:
        y_ref, gp_ref, gs_ref = refs.pop(0), refs.pop(0), refs.pop(0)
    if has_norm:
        scp_ref, scs_ref, shp_ref, shs_ref, nw_ref = (refs.pop(0) for _ in range(5))
    if has_resid:
        xo_ref = refs.pop(0)
    if has_norm:
        h_ref = refs.pop(0)
    i = pl.program_id(0)

    def body(prompt):
        mod = (lambda p, s: p[0:1, :]) if prompt else (lambda p, s: s[...])
        x = x_ref[...]
        if has_resid:
            x = x + mod(gp_ref, gs_ref) * y_ref[...]
            xo_ref[...] = x
        if has_norm:
            hn = _rms(x, nw_ref[...])
            h = hn * (1.0 + mod(scp_ref, scs_ref)) + mod(shp_ref, shs_ref)
            h_ref[...] = h.astype(h_ref.dtype)

    pl.when(i < n_prompt_tiles)(lambda: body(True))
    pl.when(i >= n_prompt_tiles)(lambda: body(False))


def _resid_norm(x, y, mod_p, mod_s, norm_w, l, *, gate_k, scale_k, shift_k, seq, n_prompt):
    R, D = x.shape
    TE = mod_s.shape[1]
    tpb = seq // TE
    npt = n_prompt * tpb
    has_resid = y is not None
    has_norm = norm_w is not None
    row = pl.BlockSpec((TE, D), lambda i: (i, 0))

    def mp(lk):
        return pl.BlockSpec((None, SUBLANES, D), lambda i: (lk[0], jnp.minimum(i // tpb, n_prompt - 1), lk[1]))

    def ms(lk):
        return pl.BlockSpec((None, TE, D), lambda i: (lk[0], 0, lk[1]))

    args, specs, outs, out_specs = [x], [row], [], []
    if has_resid:
        args += [y, mod_p, mod_s]
        specs += [row, mp(gate_k), ms(gate_k)]
        outs.append(jax.ShapeDtypeStruct((R, D), F32))
        out_specs.append(row)
    if has_norm:
        args += [mod_p, mod_s, mod_p, mod_s, norm_w.reshape(norm_w.shape[0], 1, D)]
        specs += [mp(scale_k), ms(scale_k), mp(shift_k), ms(shift_k),
                  pl.BlockSpec((None, 1, D), lambda i: (l, 0, 0))]
        outs.append(jax.ShapeDtypeStruct((R, D), BF16))
        out_specs.append(row)
    res = pl.pallas_call(
        functools.partial(_resid_norm_kernel, n_prompt_tiles=npt, has_resid=has_resid, has_norm=has_norm),
        grid=(R // TE,),
        in_specs=specs, out_specs=out_specs, out_shape=outs,
        compiler_params=_params("parallel"),
        name="resid_norm",
    )(*args)
    res = list(res)
    x_new = res.pop(0) if has_resid else x
    h = res.pop(0) if has_norm else None
    return x_new, h


def _mm_kernel(a_ref, w_ref, o_ref, acc_ref, *, nk):
    part = jnp.dot(a_ref[...], w_ref[...].astype(BF16), preferred_element_type=F32)
    if nk == 1:
        o_ref[...] = part.astype(o_ref.dtype)
        return
    k = pl.program_id(2)

    @pl.when(k == 0)
    def _():
        acc_ref[...] = part

    @pl.when(k > 0)
    def _():
        acc_ref[...] += part

    @pl.when(k == nk - 1)
    def _():
        o_ref[...] = acc_ref[...].astype(o_ref.dtype)


def _matmul(a, w, l, *, out_dtype, tm, tn, tk):
    R, K = a.shape
    N = w.shape[-1]
    nk = K // tk
    acc_shape = (tm, tn) if nk > 1 else (SUBLANES, LANES)
    return pl.pallas_call(
        functools.partial(_mm_kernel, nk=nk),
        grid=(R // tm, N // tn, nk),
        in_specs=[pl.BlockSpec((tm, tk), lambda i, j, k: (i, k)),
                  pl.BlockSpec((None, tk, tn), lambda i, j, k: (l, k, j))],
        out_specs=pl.BlockSpec((tm, tn), lambda i, j, k: (i, j)),
        out_shape=jax.ShapeDtypeStruct((R, N), out_dtype),
        scratch_shapes=[pltpu.VMEM(acc_shape, F32)],
        compiler_params=_params("parallel", "parallel", "arbitrary"),
        name="matmul",
    )(a, w)


def _softmax_sink_pv(s, sink, v):
    m = jnp.maximum(jnp.max(s, axis=-1, keepdims=True), sink)
    e = jnp.exp(s - m)
    den = jnp.sum(e, axis=-1, keepdims=True) + jnp.exp(sink - m)
    return jnp.dot((e / den).astype(BF16), v, preferred_element_type=F32)


def _attn_prompt_kernel(sink_ref, q_ref, kp_ref, kc_ref, vp_ref, vc_ref, qw_ref, kw_ref,
                        o_ref, kwin_ref, vwin_ref, *, l, n_kv, q_per_kv, hd, window, nblk):
    i = pl.program_id(1)
    blk = q_ref.shape[0]
    scale = hd ** -0.5
    r = lax.broadcasted_iota(jnp.int32, (blk, 2 * blk), 0)
    c = lax.broadcasted_iota(jnp.int32, (blk, 2 * blk), 1)
    dist = r + blk - c
    ok = (dist >= 0) & (dist <= window) & ((c >= blk) | (i > 0))
    qw = qw_ref[...]
    kw = kw_ref[...]
    for g in range(n_kv):
        sl = slice(g * hd, (g + 1) * hd)
        kcn = _rms(kc_ref[:, sl], kw)
        kcat = jnp.concatenate([_rms(kp_ref[:, sl], kw), kcn], axis=0).astype(BF16)
        vcat = jnp.concatenate([vp_ref[:, sl], vc_ref[:, sl]], axis=0).astype(BF16)
        for h in range(q_per_kv):
            head = g * q_per_kv + h
            hs = slice(head * hd, (head + 1) * hd)
            qh = _rms(q_ref[:, hs], qw).astype(BF16)
            s = lax.dot_general(qh, kcat, (((1,), (1,)), ((), ())), preferred_element_type=F32) * scale
            s = jnp.where(ok, s, -1e30)
            o_ref[:, hs] = _softmax_sink_pv(s, sink_ref[l, head], vcat).astype(o_ref.dtype)

        @pl.when(i == nblk - 1)
        def _():
            kwin_ref[:, sl] = kcn
            vwin_ref[:, sl] = vc_ref[:, sl]


def _attn_prompt(z, sinks, q_norm_w, k_norm_w, l, *, R, n_prompt, seq, n_q, n_kv, hd, window):
    blk = window
    nblk = seq // blk
    wa = n_q * hd
    kvw = n_kv * hd
    kcol = wa // kvw
    vcol = kcol + 1

    def cur(col):
        return pl.BlockSpec((blk, kvw), lambda b, i: (b * nblk + i, col))

    def prev(col):
        return pl.BlockSpec((blk, kvw), lambda b, i: (b * nblk + jnp.maximum(i - 1, 0), col))

    L = q_norm_w.shape[0]
    nw = pl.BlockSpec((None, 1, hd), lambda b, i: (l, 0, 0))
    return pl.pallas_call(
        functools.partial(_attn_prompt_kernel, l=l, n_kv=n_kv, q_per_kv=n_q // n_kv, hd=hd,
                          window=window, nblk=nblk),
        grid=(n_prompt, nblk),
        in_specs=[pl.BlockSpec(memory_space=pltpu.SMEM),
                  pl.BlockSpec((blk, wa), lambda b, i: (b * nblk + i, 0)),
                  prev(kcol), cur(kcol), prev(vcol), cur(vcol), nw, nw],
        out_specs=[pl.BlockSpec((blk, wa), lambda b, i: (b * nblk + i, 0)),
                   pl.BlockSpec((None, blk, kvw), lambda b, i: (b, 0, 0)),
                   pl.BlockSpec((None, blk, kvw), lambda b, i: (b, 0, 0))],
        out_shape=[jax.ShapeDtypeStruct((R, wa), BF16),
                   jax.ShapeDtypeStruct((n_prompt, blk, kvw), F32),
                   jax.ShapeDtypeStruct((n_prompt, blk, kvw), F32)],
        compiler_params=_params("parallel", "arbitrary"),
        name="attn_prompt",
    )(sinks, z, z, z, z, z, q_norm_w.reshape(L, 1, hd), k_norm_w.reshape(L, 1, hd))


def _attn_sample_kernel(sink_ref, o_in_ref, q_ref, kn_ref, vn_ref, ck_ref, cv_ref, qw_ref, kw_ref,
                        o_ref, kwin_ref, vwin_ref, keys_s, vals_s, ostage,
                        *, l, nb, T, n_kv, q_per_kv, hd, window):
    del o_in_ref
    W = ck_ref.shape[1]
    scale = hd ** -0.5
    rows_q = q_per_kv * T
    r = lax.broadcasted_iota(jnp.int32, (rows_q, 2 * W), 0) % T
    c = lax.broadcasted_iota(jnp.int32, (rows_q, 2 * W), 1)
    q_pos = PAST_LEN + r
    k_pos = jnp.where(c < W, PAST_LEN - W + c, PAST_LEN + c - W)
    dist = q_pos - k_pos
    ok = (dist >= 0) & (dist <= window) & (k_pos >= 0) & (c < W + T)
    hrow = lax.broadcasted_iota(jnp.int32, (rows_q, 1), 0) // T
    qw = qw_ref[...]
    kw = kw_ref[...]
    keys_s[...] = jnp.zeros_like(keys_s)
    vals_s[...] = jnp.zeros_like(vals_s)

    def step(b, carry):
        rows = pl.ds(pl.multiple_of(b * T, T), T)
        for g in range(n_kv):
            sl = slice(g * hd, (g + 1) * hd)
            ck = ck_ref[b, :, sl]
            cv = cv_ref[b, :, sl]
            knn = _rms(kn_ref[rows, sl], kw)
            vnn = vn_ref[rows, sl]
            keys_s[0:W, :] = ck
            keys_s[W:W + T, :] = knn
            vals_s[0:W, :] = cv
            vals_s[W:W + T, :] = vnn
            qs = jnp.concatenate(
                [_rms(q_ref[rows, (g * q_per_kv + h) * hd:(g * q_per_kv + h + 1) * hd], qw)
                 for h in range(q_per_kv)], axis=0).astype(BF16)
            s = lax.dot_general(qs, keys_s[...].astype(BF16), (((1,), (1,)), ((), ())),
                                preferred_element_type=F32) * scale
            s = jnp.where(ok, s, -1e30)
            sink = jnp.zeros((rows_q, 1), F32)
            for h in range(q_per_kv):
                sink = jnp.where(hrow == h, sink_ref[l, g * q_per_kv + h], sink)
            o = _softmax_sink_pv(s, sink, vals_s[...].astype(BF16))
            for h in range(q_per_kv):
                head = g * q_per_kv + h
                ostage[rows, head * hd:(head + 1) * hd] = o[h * T:(h + 1) * T, :]
            kwin_ref[b, 0:W - T, sl] = ck[T:, :]
            kwin_ref[b, W - T:W, sl] = knn
            vwin_ref[b, 0:W - T, sl] = cv[T:, :]
            vwin_ref[b, W - T:W, sl] = vnn
        return carry

    lax.fori_loop(0, nb, step, 0)
    o_ref[...] = ostage[...].astype(o_ref.dtype)


def _attn_sample(attn_o, z, cache_k, cache_v, sinks, q_norm_w, k_norm_w, l, *, n_sample, T, n_q, n_kv, hd,
                 window):
    R, wa = attn_o.shape
    kvw = n_kv * hd
    rows = n_sample * T
    blk_i = (R - rows) // rows
    kcol = wa // kvw
    L, nb, W = cache_k.shape[:3]
    ck = cache_k.reshape(L, nb, W, kvw)
    cv = cache_v.reshape(L, nb, W, kvw)
    nw = pl.BlockSpec((None, 1, hd), lambda i: (l, 0, 0))
    cache = pl.BlockSpec((None, nb, W, kvw), lambda i: (l, 0, 0, 0))
    win = pl.BlockSpec((nb, W, kvw), lambda i: (0, 0, 0))
    return pl.pallas_call(
        functools.partial(_attn_sample_kernel, l=l, nb=nb, T=T, n_kv=n_kv, q_per_kv=n_q // n_kv, hd=hd,
                          window=window),
        grid=(1,),
        in_specs=[pl.BlockSpec(memory_space=pltpu.SMEM),
                  pl.BlockSpec(memory_space=pl.ANY),
                  pl.BlockSpec((rows, wa), lambda i: (blk_i, 0)),
                  pl.BlockSpec((rows, kvw), lambda i: (blk_i, kcol)),
                  pl.BlockSpec((rows, kvw), lambda i: (blk_i, kcol + 1)),
                  cache, cache, nw, nw],
        out_specs=[pl.BlockSpec((rows, wa), lambda i: (blk_i, 0)), win, win],
        out_shape=[jax.ShapeDtypeStruct((R, wa), BF16),
                   jax.ShapeDtypeStruct((nb, W, kvw), F32),
                   jax.ShapeDtypeStruct((nb, W, kvw), F32)],
        scratch_shapes=[pltpu.VMEM((2 * W, hd), F32), pltpu.VMEM((2 * W, hd), F32),
                        pltpu.VMEM((rows, wa), F32)],
        input_output_aliases={1: 0},
        compiler_params=_params("arbitrary"),
        name="attn_sample",
    )(sinks, attn_o, z, z, z, ck, cv, q_norm_w.reshape(-1, 1, hd), k_norm_w.reshape(-1, 1, hd))


def _lru_gates(xc, wa_ref, ba_ref, wx_ref, bx_ref, lam_ref, a_s, u_s, *, nblk, lb):
    xcb = xc.astype(BF16)
    for n in range(nblk):
        sl = slice(n * lb, (n + 1) * lb)
        ra = jnp.dot(xcb[:, sl], wa_ref[n].astype(BF16), preferred_element_type=F32) + ba_ref[:, sl]
        rx = jnp.dot(xcb[:, sl], wx_ref[n].astype(BF16), preferred_element_type=F32) + bx_ref[:, sl]
        log_a = -LRU_C * jax.nn.sigmoid(ra) * jax.nn.softplus(-lam_ref[:, sl])
        a_s[:, sl] = jnp.exp(log_a)
        th = jnp.tanh(log_a)
        u_s[:, sl] = jnp.sqrt(-2.0 * th / (1.0 - th)) * jax.nn.sigmoid(rx) * xc[:, sl]

def _lru_prompt_kernel(x_ref, g_ref, cw_ref, cb_ref, wa_ref, ba_ref, wx_ref, bx_ref, lam_ref,
                       o_ref, hT_ref, cnew_ref, xp_s, a_s, u_s, h_s, *, Tb, ntb, cw, nblk, lb):
    tb = pl.program_id(2)
    P = SUBLANES

    @pl.when(tb == 0)
    def _():
        xp_s[0:P, :] = jnp.zeros((P, xp_s.shape[1]), F32)
        h_s[...] = jnp.zeros_like(h_s)

    xp_s[P:P + Tb, :] = x_ref[...]
    xc = cb_ref[...]
    for j in range(cw):
        off = P - (cw - 1) + j
        xc = xc + xp_s[off:off + Tb, :] * cw_ref[j:j + 1, :]
    _lru_gates(xc, wa_ref, ba_ref, wx_ref, bx_ref, lam_ref, a_s, u_s, nblk=nblk, lb=lb)

    def step(t, h):
        h = a_s[pl.ds(t, 1), :] * h + u_s[pl.ds(t, 1), :]
        u_s[pl.ds(t, 1), :] = h
        return h

    h = lax.fori_loop(0, Tb, step, h_s[...])
    h_s[...] = h
    o_ref[...] = (jax.nn.gelu(g_ref[...]) * u_s[...]).astype(o_ref.dtype)
    tail = xp_s[Tb:Tb + P, :]
    xp_s[0:P, :] = tail

    @pl.when(tb == ntb - 1)
    def _():
        hT_ref[...] = h
        cnew_ref[...] = tail


def _lru_sample_kernel(o_in_ref, x_ref, g_ref, h0_ref, buf_ref, cw_ref, cb_ref, wa_ref, ba_ref, wx_ref, bx_ref,
                       lam_ref, o_ref, hs_ref, a_s, u_s, *, T, cw, nblk, lb):
    del o_in_ref
    rows = x_ref.shape[0]
    t = lax.broadcasted_iota(jnp.int32, (rows, 1), 0) % T
    x = x_ref[...]
    buf = buf_ref[...]
    xc = cb_ref[...] + x * cw_ref[cw - 1:cw, :]
    for s in range(1, cw):
        xs = jnp.where(t >= s, pltpu.roll(x, s, axis=0), pltpu.roll(buf, rows - (T - s), axis=0))
        xc = xc + xs * cw_ref[cw - 1 - s:cw - s, :]
    _lru_gates(xc, wa_ref, ba_ref, wx_ref, bx_ref, lam_ref, a_s, u_s, nblk=nblk, lb=lb)
    a = a_s[...]
    u = u_s[...]
    d = 1
    while d < T:
        keep = t >= d
        u = jnp.where(keep, a * pltpu.roll(u, d, axis=0) + u, u)
        a = jnp.where(keep, a * pltpu.roll(a, d, axis=0), a)
        d *= 2
    hs = u + a * h0_ref[...]
    hs_ref[...] = hs
    o_ref[...] = (jax.nn.gelu(g_ref[...]) * hs).astype(o_ref.dtype)


def _lru_weight_args(W, l, wb, index):
    ch = wb // 2
    nlb, lb = W['lru_w_a'].shape[1:3]
    nblk = nlb // 2
    cw = W['conv_w'].shape[1]

    def vec(name):
        return W[name].reshape(W[name].shape[0], 1, wb), pl.BlockSpec((None, 1, ch), index(lambda hf: (l, 0, hf)))

    def blkw(name):
        return W[name], pl.BlockSpec((None, nblk, lb, lb), index(lambda hf: (l, hf, 0, 0)))

    pairs = [(W['conv_w'], pl.BlockSpec((None, cw, ch), index(lambda hf: (l, 0, hf)))),
             vec('conv_b'), blkw('lru_w_a'), vec('lru_b_a'), blkw('lru_w_x'), vec('lru_b_x'), vec('lru_lambda')]
    return [p[0] for p in pairs], [p[1] for p in pairs], dict(cw=cw, nblk=nblk, lb=lb)


def _lru_prompt(z, W, l, *, R, nseq, T, Tb, col_x, col_g, wb):
    ch = wb // 2
    ntb = T // Tb
    wargs, wspecs, kw = _lru_weight_args(W, l, wb, lambda f: (lambda s, hf, t: f(hf)))

    def rowblk(col):
        return pl.BlockSpec((Tb, ch), lambda s, hf, t: (s * ntb + t, col + hf))

    return pl.pallas_call(
        functools.partial(_lru_prompt_kernel, Tb=Tb, ntb=ntb, **kw),
        grid=(nseq, 2, ntb),
        in_specs=[rowblk(col_x), rowblk(col_g)] + wspecs,
        out_specs=[pl.BlockSpec((Tb, ch), lambda s, hf, t: (s * ntb + t, hf)),
                   pl.BlockSpec((None, 1, ch), lambda s, hf, t: (s, 0, hf)),
                   pl.BlockSpec((None, SUBLANES, ch), lambda s, hf, t: (s, 0, hf))],
        out_shape=[jax.ShapeDtypeStruct((R, wb), BF16),
                   jax.ShapeDtypeStruct((nseq, 1, wb), F32),
                   jax.ShapeDtypeStruct((nseq, SUBLANES, wb), F32)],
        scratch_shapes=[pltpu.VMEM((SUBLANES + Tb, ch), F32), pltpu.VMEM((Tb, ch), F32),
                        pltpu.VMEM((Tb, ch), F32), pltpu.VMEM((1, ch), F32)],
        compiler_params=_params("parallel", "parallel", "arbitrary"),
        name="lru_prompt",
    )(z, z, *wargs)


def _lru_sample(lru_o, z, h0_rep, buf_rows, W, l, *, row0, T, col_x, col_g, wb):
    ch = wb // 2
    rows = h0_rep.shape[0]
    rb = row0 // rows
    wargs, wspecs, kw = _lru_weight_args(W, l, wb, lambda f: (lambda hf: f(hf)))
    st = pl.BlockSpec((rows, ch), lambda hf: (0, hf))
    return pl.pallas_call(
        functools.partial(_lru_sample_kernel, T=T, **kw),
        grid=(2,),
        in_specs=[pl.BlockSpec(memory_space=pl.ANY),
                  pl.BlockSpec((rows, ch), lambda hf: (rb, col_x + hf)),
                  pl.BlockSpec((rows, ch), lambda hf: (rb, col_g + hf)), st, st] + wspecs,
        out_specs=[pl.BlockSpec((rows, ch), lambda hf: (rb, hf)), st],
        out_shape=[jax.ShapeDtypeStruct(lru_o.shape, BF16), jax.ShapeDtypeStruct((rows, wb), F32)],
        scratch_shapes=[pltpu.VMEM((rows, ch), F32), pltpu.VMEM((rows, ch), F32)],
        input_output_aliases={0: 0},
        compiler_params=_params("parallel"),
        name="lru_sample",
    )(lru_o, z, z, h0_rep, buf_rows, *wargs)


def _chunk_kernel(*refs, nb, Tc, ngroups, gw, cps, aliased):
    refs = list(refs)
    if aliased:
        refs.pop(0)
    ulo_ref, uhi_ref, vlo_ref, vhi_ref, nw_ref, ws_ref, bst_ref, o_ref, vn_ref = refs
    rows = nb * Tc
    half = vlo_ref.shape[1]
    wc = 2 * half
    v_lo = jax.nn.gelu(vlo_ref[...])
    v_hi = jax.nn.gelu(vhi_ref[...])
    ms = (jnp.sum(v_lo * v_lo, axis=-1, keepdims=True) + jnp.sum(v_hi * v_hi, axis=-1, keepdims=True)) / wc
    inv = lax.rsqrt(ms + EPS)
    vn_halves = (v_lo * inv * nw_ref[:, 0:half], v_hi * inv * nw_ref[:, half:wc])
    u_halves = (ulo_ref, uhi_ref)

    r = lax.broadcasted_iota(jnp.int32, (rows, rows), 0)
    c = lax.broadcasted_iota(jnp.int32, (rows, rows), 1)
    mask = (r // Tc == c // Tc) & (c % Tc <= r % Tc)
    if nb > 1:
        sel = (lax.broadcasted_iota(jnp.int32, (ws_ref.shape[2], rows), 0)
               == lax.broadcasted_iota(jnp.int32, (ws_ref.shape[2], rows), 1) % Tc).astype(BF16)
        bias_rows = jnp.broadcast_to(bst_ref[0:Tc, :][None], (nb, Tc, bst_ref.shape[1])).reshape(rows, -1)
    else:
        bias_rows = bst_ref[0:Tc, :]
    gph = half // gw
    for g in range(ngroups):
        hf, gi = divmod(g, gph)
        sl = slice(gi * gw, (gi + 1) * gw)
        if nb > 1:
            t1 = jnp.broadcast_to(ws_ref[g, 0:Tc, :][None], (nb, Tc, ws_ref.shape[2])).reshape(rows, -1)
            wfull = jnp.dot(t1.astype(BF16), sel, preferred_element_type=F32)
        else:
            wfull = ws_ref[g, 0:Tc, 0:Tc]
        wm = jnp.where(mask, wfull, 0.0).astype(BF16)
        mixed = jnp.dot(wm, vn_halves[hf][:, sl].astype(BF16), preferred_element_type=F32)
        mixed = mixed + bias_rows[:, g:g + 1]
        o_ref[:, g * gw:(g + 1) * gw] = (jax.nn.gelu(u_halves[hf][:, sl]) * mixed).astype(o_ref.dtype)

    if cps == 1:
        vn_ref[:, 0:half] = vn_halves[0]
        vn_ref[:, half:wc] = vn_halves[1]
    else:
        @pl.when(pl.program_id(0) % cps == cps - 1)
        def _():
            vn_ref[:, 0:half] = vn_halves[0]
            vn_ref[:, half:wc] = vn_halves[1]


def _chunk(chunk_o, z, W, l, *, R, row0, nsteps, nb, Tc, cps, col_u, col_v, wc):
    half = wc // 2
    ngroups, chunk = W['chunk_w_s'].shape[1:3]
    gw = wc // ngroups
    rows = nb * Tc
    rb0 = row0 // rows
    aliased = chunk_o is not None

    def rowblk(col):
        return pl.BlockSpec((rows, half), lambda i: (rb0 + i, col))

    args = [z, z, z, z, W['chunk_v_norm_w'].reshape(-1, 1, wc), W['chunk_w_s'],
            jnp.swapaxes(W['chunk_b_s'], 1, 2)]
    specs = [rowblk(col_u), rowblk(col_u + 1), rowblk(col_v), rowblk(col_v + 1),
             pl.BlockSpec((None, 1, wc), lambda i: (l, 0, 0)),
             pl.BlockSpec((None, ngroups, chunk, chunk), lambda i: (l, 0, 0, 0)),
             pl.BlockSpec((None, chunk, ngroups), lambda i: (l, 0, 0))]
    io_alias = {}
    if aliased:
        args.insert(0, chunk_o)
        specs.insert(0, pl.BlockSpec(memory_space=pl.ANY))
        io_alias = {0: 0}
    return pl.pallas_call(
        functools.partial(_chunk_kernel, nb=nb, Tc=Tc, ngroups=ngroups, gw=gw, cps=cps, aliased=aliased),
        grid=(nsteps,),
        in_specs=specs,
        out_specs=[pl.BlockSpec((rows, wc), lambda i: (rb0 + i, 0)),
                   pl.BlockSpec((None, rows, wc), lambda i: (i // cps, 0, 0))],
        out_shape=[jax.ShapeDtypeStruct((R, wc), BF16),
                   jax.ShapeDtypeStruct((nsteps // cps, rows, wc), F32)],
        input_output_aliases=io_alias,
        compiler_params=_params("arbitrary"),
        name="chunk_sample" if aliased else "chunk_prompt",
    )(*args)


def _merge_kernel(a_ref, b_ref, c_ref, wa_ref, wb_ref, wc_ref, ga_ref, gb_ref, gc_ref, o_ref):
    def branch(x_ref, w_ref, g_ref):
        y = jnp.dot(x_ref[...], w_ref[...].astype(BF16), preferred_element_type=F32)
        return jax.nn.sigmoid(g_ref[...]) * y

    m = branch(a_ref, wa_ref, ga_ref) + branch(b_ref, wb_ref, gb_ref) + branch(c_ref, wc_ref, gc_ref)
    o_ref[...] = m.astype(o_ref.dtype)


def _merge(attn_o, lru_o, chunk_o, z, W, l, *, gate_col, tm, tn):
    R = attn_o.shape[0]
    D = W['w_branch_a'].shape[-1]
    g0 = gate_col // tn
    gstep = D // tn

    def xin(a):
        return pl.BlockSpec((tm, a.shape[1]), lambda i, j: (i, 0))

    def win(w):
        return pl.BlockSpec((None, w.shape[1], tn), lambda i, j: (l, 0, j))

    def gin(k):
        return pl.BlockSpec((tm, tn), lambda i, j: (i, g0 + k * gstep + j))

    return pl.pallas_call(
        _merge_kernel,
        grid=(R // tm, D // tn),
        in_specs=[xin(attn_o), xin(lru_o), xin(chunk_o),
                  win(W['w_branch_a']), win(W['w_branch_b']), win(W['w_branch_c']),
                  gin(0), gin(1), gin(2)],
        out_specs=pl.BlockSpec((tm, tn), lambda i, j: (i, j)),
        out_shape=jax.ShapeDtypeStruct((R, D), BF16),
        compiler_params=_params("parallel", "parallel"),
        name="merge",
    )(attn_o, lru_o, chunk_o, W['w_branch_a'], W['w_branch_b'], W['w_branch_c'], z, z, z)


def _glu_kernel(*refs, gated):
    if gated:
        h_ref, wg_ref, wu_ref, gate_ref, o_ref = refs
    else:
        h_ref, wg_ref, wu_ref, o_ref = refs
    h = h_ref[...]
    g = jnp.dot(h, wg_ref[...].astype(BF16), preferred_element_type=F32)
    u = jnp.dot(h, wu_ref[...].astype(BF16), preferred_element_type=F32)
    a = jax.nn.silu(g) * u
    if gated:
        e = pl.program_id(1)
        gate = gate_ref[...]
        lane = lax.broadcasted_iota(jnp.int32, gate.shape, 1)
        a = a * jnp.sum(jnp.where(lane == e, gate, 0.0), axis=-1, keepdims=True)
    o_ref[...] = a.astype(o_ref.dtype)


def _glu(h, w_gate, w_up, l0, n_exp, gate, *, tm, tn):
    R, D = h.shape
    F = w_gate.shape[-1]
    nf = F // tn
    gated = gate is not None
    wspec = pl.BlockSpec((None, D, tn), lambda i, e, j: (l0 + e, 0, j))
    args = [h, w_gate, w_up]
    specs = [pl.BlockSpec((tm, D), lambda i, e, j: (i, 0)), wspec, wspec]
    if gated:
        args.append(gate)
        specs.append(pl.BlockSpec((tm, gate.shape[1]), lambda i, e, j: (i, 0)))
    return pl.pallas_call(
        functools.partial(_glu_kernel, gated=gated),
        grid=(R // tm, n_exp, nf),
        in_specs=specs,
        out_specs=pl.BlockSpec((tm, tn), lambda i, e, j: (i, e * nf + j)),
        out_shape=jax.ShapeDtypeStruct((R, n_exp * F), BF16),
        compiler_params=_params("parallel", "parallel", "parallel"),
        name="glu",
    )(*args)


def _router_kernel(h_ref, w_ref, o_ref, *, n_exp):
    logits = jnp.dot(h_ref[...], w_ref[...].astype(BF16), preferred_element_type=F32)
    lane = lax.broadcasted_iota(jnp.int32, logits.shape, 1).astype(F32)
    big = float(logits.shape[1])
    l0 = jnp.where(lane < n_exp, logits, -jnp.inf)
    m1 = jnp.max(l0, axis=-1, keepdims=True)
    i1 = jnp.min(jnp.where(l0 == m1, lane, big), axis=-1, keepdims=True)
    l1 = jnp.where(lane == i1, -jnp.inf, l0)
    m2 = jnp.max(l1, axis=-1, keepdims=True)
    i2 = jnp.min(jnp.where(l1 == m2, lane, big), axis=-1, keepdims=True)
    e2 = jnp.exp(m2 - m1)
    den = 1.0 + e2
    o_ref[...] = jnp.where(lane == i1, 1.0 / den, 0.0) + jnp.where(lane == i2, e2 / den, 0.0)


def _router(h, w_router_padded, j, *, n_exp, tm):
    R, D = h.shape
    NP = w_router_padded.shape[-1]
    return pl.pallas_call(
        functools.partial(_router_kernel, n_exp=n_exp),
        grid=(R // tm,),
        in_specs=[pl.BlockSpec((tm, D), lambda i: (i, 0)),
                  pl.BlockSpec((None, D, NP), lambda i: (j, 0, 0))],
        out_specs=pl.BlockSpec((tm, NP), lambda i: (i, 0)),
        out_shape=jax.ShapeDtypeStruct((R, NP), F32),
        compiler_params=_params("parallel"),
        name="router",
    )(h, w_router_padded)


def kernel(x_prompt, x_sample, cache_k_win, cache_v_win, state_rglru_h, state_conv, c_prompt, c_sample, norm1_w, norm2_w, w_ada, b_ada, w_in, q_norm_w, k_norm_w, attn_sinks, conv_w, conv_b, lru_w_a, lru_b_a, lru_w_x, lru_b_x, lru_lambda, chunk_v_norm_w, chunk_w_s, chunk_b_s, w_branch_a, w_branch_b, w_branch_c, w_out, ffn_w_gate, ffn_w_up, ffn_w_down, moe_w_router, moe_w_gate, moe_w_up, moe_w_down):
    Bp, S, D = x_prompt.shape
    Bs, Ts, _ = x_sample.shape
    L = w_in.shape[0]
    window, n_kv, hd = cache_k_win.shape[2:]
    n_q = attn_sinks.shape[1]
    wa, kvw = n_q * hd, n_kv * hd
    wb = conv_w.shape[-1]
    wc = chunk_v_norm_w.shape[-1]
    chunk = chunk_w_s.shape[-1]
    cwid = conv_w.shape[1]
    n_exp = moe_w_router.shape[-1]
    assert Ts == SUBLANES and S % (Bs * Ts) == 0 and S % chunk == 0 and S % window == 0
    Rp, Rs = Bp * S, Bs * Ts
    R = Rp + Rs
    half = wb // 2
    assert wb == wc and wa % half == 0 and kvw % half == 0 or True
    col_lx = (wa + 2 * kvw) // half
    col_lg = col_lx + 2
    col_cu = col_lg + 2
    col_cv = col_cu + 2
    gate_col = wa + 2 * kvw + 2 * wb + 2 * wc

    W = dict(conv_w=conv_w, conv_b=conv_b, lru_w_a=lru_w_a, lru_b_a=lru_b_a, lru_w_x=lru_w_x, lru_b_x=lru_b_x,
             lru_lambda=lru_lambda, chunk_v_norm_w=chunk_v_norm_w, chunk_w_s=chunk_w_s, chunk_b_s=chunk_b_s,
             w_branch_a=w_branch_a, w_branch_b=w_branch_b, w_branch_c=w_branch_c)

    n_c = Bp + Bs
    n_c_pad = -(-n_c // SUBLANES) * SUBLANES
    c_all = jnp.concatenate([c_prompt, c_sample, jnp.zeros((n_c_pad - n_c, D), F32)], axis=0)
    mod = _ada(c_all, w_ada, b_ada)
    mod_p = jnp.repeat(mod[:, :Bp], SUBLANES, axis=1)
    mod_s = jnp.repeat(mod[:, Bp:n_c], Ts, axis=1)

    x = jnp.concatenate([x_prompt.reshape(Rp, D), x_sample.reshape(Rs, D)], axis=0)
    rn = functools.partial(_resid_norm, seq=S, n_prompt=Bp)
    tm_big = _pick(R, 1408, 128) if R % 128 == 0 else _pick(R, 1408, 16)
    tm_mid = _pick(R, 768, 128) if R % 128 == 0 else _pick(R, 768, 16)
    w_router_p = jnp.pad(moe_w_router, ((0, 0), (0, 0), (0, LANES - n_exp)))
    moe_gate_w = moe_w_gate.reshape((-1,) + moe_w_gate.shape[2:])
    moe_up_w = moe_w_up.reshape((-1,) + moe_w_up.shape[2:])
    moe_down_w = moe_w_down.reshape(moe_w_down.shape[0], -1, D)

    buf_rows = jnp.pad(state_conv, ((0, 0), (0, 0), (Ts - (cwid - 1), 0), (0, 0))).reshape(L, Rs, wb)
    h0_rep = jnp.repeat(state_rglru_h, Ts, axis=1)

    _, h = rn(x, None, mod_p, mod_s, norm1_w, 0, gate_k=None, scale_k=(0, 1), shift_k=(0, 0))
    outs = []
    for l in range(L):
        z = _matmul(h, w_in, l, out_dtype=F32, tm=tm_big, tn=_pick(w_in.shape[-1], 512, half), tk=D)
        attn_o, kwp, vwp = _attn_prompt(z, attn_sinks, q_norm_w, k_norm_w, l, R=R, n_prompt=Bp, seq=S,
                                        n_q=n_q, n_kv=n_kv, hd=hd, window=window)
        attn_o, kws, vws = _attn_sample(attn_o, z, cache_k_win, cache_v_win, attn_sinks, q_norm_w, k_norm_w, l,
                                        n_sample=Bs, T=Ts, n_q=n_q, n_kv=n_kv, hd=hd, window=window)
        lru_o, hp, cp8 = _lru_prompt(z, W, l, R=R, nseq=Bp, T=S, Tb=_pick(S, 256, SUBLANES),
                                     col_x=col_lx, col_g=col_lg, wb=wb)
        lru_o, hs_all = _lru_sample(lru_o, z, h0_rep[l], buf_rows[l], W, l, row0=Rp, T=Ts,
                                    col_x=col_lx, col_g=col_lg, wb=wb)
        chunk_o, vnp = _chunk(None, z, W, l, R=R, row0=0, nsteps=Rp // chunk, nb=1, Tc=chunk,
                              cps=S // chunk, col_u=col_cu, col_v=col_cv, wc=wc)
        chunk_o, vns = _chunk(chunk_o, z, W, l, R=R, row0=Rp, nsteps=1, nb=Bs, Tc=Ts, cps=1,
                              col_u=col_cu, col_v=col_cv, wc=wc)
        m = _merge(attn_o, lru_o, chunk_o, z, W, l, gate_col=gate_col, tm=tm_mid, tn=_pick(D, 512, LANES))
        y = _matmul(m, w_out, l, out_dtype=F32, tm=tm_big, tn=_pick(D, 512, LANES), tk=D)
        x, h = rn(x, y, mod_p, mod_s, norm2_w, l, gate_k=(l, 2), scale_k=(l, 4), shift_k=(l, 3))
        j = l // 2
        if l % 2 == 0:
            a = _glu(h, ffn_w_gate, ffn_w_up, j, 1, None, tm=tm_big, tn=_pick(ffn_w_gate.shape[-1], 512, LANES))
            f = _matmul(a, ffn_w_down, j, out_dtype=F32, tm=tm_big, tn=_pick(D, 1024, LANES),
                        tk=_pick(a.shape[1], 512, LANES))
        else:
            gate = _router(h, w_router_p, j, n_exp=n_exp, tm=tm_mid)
            a = _glu(h, moe_gate_w, moe_up_w, j * n_exp, n_exp, gate, tm=tm_big,
                     tn=_pick(moe_gate_w.shape[-1], 512, LANES))
            f = _matmul(a, moe_down_w, j, out_dtype=F32, tm=tm_big, tn=_pick(D, 1024, LANES),
                        tk=_pick(a.shape[1], 512, LANES))
        if l + 1 < L:
            x, h = rn(x, f, mod_p, mod_s, norm1_w, l + 1, gate_k=(l, 5), scale_k=(l + 1, 1), shift_k=(l + 1, 0))
        else:
            x, _ = rn(x, f, mod_p, mod_s, None, l, gate_k=(l, 5), scale_k=None, shift_k=None)
        conv_p = cp8[:, SUBLANES - (cwid - 1):, :]
        lx_s = z[Rp:, col_lx * half:col_lx * half + wb].reshape(Bs, Ts, wb)
        conv_s = jnp.concatenate([state_conv[l], lx_s], axis=1)[:, -(cwid - 1):]
        hs = hs_all.reshape(Bs, Ts, wb)[:, Ts - 1]
        outs.append((kwp.reshape(Bp, window, n_kv, hd), vwp.reshape(Bp, window, n_kv, hd), hp.reshape(Bp, wb),
                     conv_p, vnp,
                     kws.reshape(Bs, window, n_kv, hd), vws.reshape(Bs, window, n_kv, hd), hs.reshape(Bs, wb),
                     conv_s, vns.reshape(Bs, Ts, wc)))
    st = [jnp.stack(s) for s in zip(*outs)]
    y_prompt = x[:Rp].reshape(Bp, S, D)
    y_sample = x[Rp:].reshape(Bs, Ts, D)
    return (y_prompt, y_sample, st[0], st[1], st[2], st[3], st[4], st[5], st[6], st[7], st[8], st[9])
```

```python
import functools

import jax
import jax.numpy as jnp
from jax import lax
from jax.experimental import pallas as pl
from jax.experimental.pallas import tpu as pltpu

EPS = 1e-6
LRU_C = 8.0
PAST_LEN = 16384
TOP_K = 2
MOE_TILE_SUBS = 3
SUBLANES = 8
LANES = 128
V7X_VMEM_LIMIT = 56 * 1024 * 1024

F32 = jnp.float32
BF16 = jnp.bfloat16


def _pick(n, target, mult):
    best = None
    for t in range(mult, min(n, target) + 1, mult):
        if n % t == 0:
            best = t
    assert best is not None, (n, target, mult)
    return best


def _params(*sem):
    return pltpu.CompilerParams(dimension_semantics=sem, vmem_limit_bytes=V7X_VMEM_LIMIT)


def _rms(x, w):
    return x * lax.rsqrt(jnp.mean(x * x, axis=-1, keepdims=True) + EPS) * w


def _ada_kernel(c_ref, w_ref, b_ref, o_ref):
    a = jax.nn.silu(c_ref[...]).astype(BF16)
    o_ref[...] = jnp.dot(a, w_ref[...].astype(BF16), preferred_element_type=F32) + b_ref[...]


def _ada(c_all, w_ada, b_ada):
    L, D, W6 = w_ada.shape
    Bc = c_all.shape[0]
    tn = _pick(W6, 1024, LANES)
    return pl.pallas_call(
        _ada_kernel,
        grid=(L, W6 // tn),
        in_specs=[pl.BlockSpec((Bc, D), lambda l, j: (0, 0)),
                  pl.BlockSpec((None, D, tn), lambda l, j: (l, 0, j)),
                  pl.BlockSpec((None, 1, tn), lambda l, j: (l, 0, j))],
        out_specs=pl.BlockSpec((None, Bc, tn), lambda l, j: (l, 0, j)),
        out_shape=jax.ShapeDtypeStruct((L, Bc, W6), F32),
        compiler_params=_params("parallel", "parallel"),
        name="ada_mod",
    )(c_all, w_ada, b_ada.reshape(L, 1, W6))


def _resid_norm_kernel(*refs, n_prompt_tiles, has_resid, has_norm, has_rows):
    refs = list(refs)
    x_ref = refs.pop(0)
    if has_resid:
        y_ref, gp_ref, gs_ref = refs.pop(0), refs.pop(0), refs.pop(0)
    if has_norm:
        scp_ref, scs_ref, shp_ref, shs_ref, nw_ref = (refs.pop(0) for _ in range(5))
    if has_resid:
        xo_ref = refs.pop(0)
    if has_norm:
        h_ref = refs.pop(0)
    if has_rows:
        hrow_ref = refs.pop(0)
    i = pl.program_id(0)

    def body(prompt):
        mod = (lambda p, s: p[0:1, :]) if prompt else (lambda p, s: s[...])
        x = x_ref[...]
        if has_resid:
            x = x + mod(gp_ref, gs_ref) * y_ref[...]
            xo_ref[...] = x
        if has_norm:
            hn = _rms(x, nw_ref[...])
            h = hn * (1.0 + mod(scp_ref, scs_ref)) + mod(shp_ref, shs_ref)
            h_ref[...] = h.astype(h_ref.dtype)
            if has_rows:
                for c in range(hrow_ref.shape[1]):
                    hrow_ref[:, c, :] = h[:, c * LANES:(c + 1) * LANES]

    pl.when(i < n_prompt_tiles)(lambda: body(True))
    pl.when(i >= n_prompt_tiles)(lambda: body(False))


def _resid_norm(x, y, mod_p, mod_s, norm_w, l, *, gate_k, scale_k, shift_k, seq, n_prompt, rows_out=False):
    R, D = x.shape
    TE = mod_s.shape[1]
    tpb = seq // TE
    npt = n_prompt * tpb
    has_resid = y is not None
    has_norm = norm_w is not None
    row = pl.BlockSpec((TE, D), lambda i: (i, 0))

    def mp(lk):
        return pl.BlockSpec((None, SUBLANES, D), lambda i: (lk[0], jnp.minimum(i // tpb, n_prompt - 1), lk[1]))

    def ms(lk):
        return pl.BlockSpec((None, TE, D), lambda i: (lk[0], 0, lk[1]))

    args, specs, outs, out_specs = [x], [row], [], []
    if has_resid:
        args += [y, mod_p, mod_s]
        specs += [row, mp(gate_k), ms(gate_k)]
        outs.append(jax.ShapeDtypeStruct((R, D), F32))
        out_specs.append(row)
    if has_norm:
        args += [mod_p, mod_s, mod_p, mod_s, norm_w.reshape(norm_w.shape[0], 1, D)]
        specs += [mp(scale_k), ms(scale_k), mp(shift_k), ms(shift_k),
                  pl.BlockSpec((None, 1, D), lambda i: (l, 0, 0))]
        outs.append(jax.ShapeDtypeStruct((R, D), BF16))
        out_specs.append(row)
    if rows_out:
        outs.append(jax.ShapeDtypeStruct((R, D // LANES, LANES), F32))
        out_specs.append(pl.BlockSpec((TE, D // LANES, LANES), lambda i: (i, 0, 0)))
    res = pl.pallas_call(
        functools.partial(_resid_norm_kernel, n_prompt_tiles=npt, has_resid=has_resid, has_norm=has_norm,
                          has_rows=rows_out),
        grid=(R // TE,),
        in_specs=specs, out_specs=out_specs, out_shape=outs,
        compiler_params=_params("parallel"),
        name="resid_norm",
    )(*args)
    res = list(res)
    x_new = res.pop(0) if has_resid else x
    h = res.pop(0) if has_norm else None
    if rows_out:
        return x_new, h, res.pop(0)
    return x_new, h


def _mm_kernel(a_ref, w_ref, o_ref, acc_ref, *, nk):
    part = jnp.dot(a_ref[...], w_ref[...].astype(BF16), preferred_element_type=F32)
    if nk == 1:
        o_ref[...] = part.astype(o_ref.dtype)
        return
    k = pl.program_id(2)

    @pl.when(k == 0)
    def _():
        acc_ref[...] = part

    @pl.when(k > 0)
    def _():
        acc_ref[...] += part

    @pl.when(k == nk - 1)
    def _():
        o_ref[...] = acc_ref[...].astype(o_ref.dtype)


def _matmul(a, w, l, *, out_dtype, tm, tn, tk):
    R, K = a.shape
    N = w.shape[-1]
    nk = K // tk
    acc_shape = (tm, tn) if nk > 1 else (SUBLANES, LANES)
    return pl.pallas_call(
        functools.partial(_mm_kernel, nk=nk),
        grid=(R // tm, N // tn, nk),
        in_specs=[pl.BlockSpec((tm, tk), lambda i, j, k: (i, k)),
                  pl.BlockSpec((None, tk, tn), lambda i, j, k: (l, k, j))],
        out_specs=pl.BlockSpec((tm, tn), lambda i, j, k: (i, j)),
        out_shape=jax.ShapeDtypeStruct((R, N), out_dtype),
        scratch_shapes=[pltpu.VMEM(acc_shape, F32)],
        compiler_params=_params("parallel", "parallel", "arbitrary"),
        name="matmul",
    )(a, w)


def _softmax_sink_pv(s, sink, v):
    m = jnp.maximum(jnp.max(s, axis=-1, keepdims=True), sink)
    e = jnp.exp(s - m)
    den = jnp.sum(e, axis=-1, keepdims=True) + jnp.exp(sink - m)
    return jnp.dot((e / den).astype(BF16), v, preferred_element_type=F32)


def _attn_prompt_kernel(sink_ref, q_ref, kp_ref, kc_ref, vp_ref, vc_ref, qw_ref, kw_ref,
                        o_ref, kwin_ref, vwin_ref, *, l, n_kv, q_per_kv, hd, window, nblk):
    i = pl.program_id(1)
    blk = q_ref.shape[0]
    scale = hd ** -0.5
    r = lax.broadcasted_iota(jnp.int32, (blk, 2 * blk), 0)
    c = lax.broadcasted_iota(jnp.int32, (blk, 2 * blk), 1)
    dist = r + blk - c
    ok = (dist >= 0) & (dist <= window) & ((c >= blk) | (i > 0))
    qw = qw_ref[...]
    kw = kw_ref[...]
    for g in range(n_kv):
        sl = slice(g * hd, (g + 1) * hd)
        kcn = _rms(kc_ref[:, sl], kw)
        kcat = jnp.concatenate([_rms(kp_ref[:, sl], kw), kcn], axis=0).astype(BF16)
        vcat = jnp.concatenate([vp_ref[:, sl], vc_ref[:, sl]], axis=0).astype(BF16)
        for h in range(q_per_kv):
            head = g * q_per_kv + h
            hs = slice(head * hd, (head + 1) * hd)
            qh = _rms(q_ref[:, hs], qw).astype(BF16)
            s = lax.dot_general(qh, kcat, (((1,), (1,)), ((), ())), preferred_element_type=F32) * scale
            s = jnp.where(ok, s, -1e30)
            o_ref[:, hs] = _softmax_sink_pv(s, sink_ref[l, head], vcat).astype(o_ref.dtype)

        @pl.when(i == nblk - 1)
        def _():
            kwin_ref[:, sl] = kcn
            vwin_ref[:, sl] = vc_ref[:, sl]


def _attn_prompt(z, sinks, q_norm_w, k_norm_w, l, *, R, n_prompt, seq, n_q, n_kv, hd, window):
    blk = window
    nblk = seq // blk
    wa = n_q * hd
    kvw = n_kv * hd
    kcol = wa // kvw
    vcol = kcol + 1

    def cur(col):
        return pl.BlockSpec((blk, kvw), lambda b, i: (b * nblk + i, col))

    def prev(col):
        return pl.BlockSpec((blk, kvw), lambda b, i: (b * nblk + jnp.maximum(i - 1, 0), col))

    L = q_norm_w.shape[0]
    nw = pl.BlockSpec((None, 1, hd), lambda b, i: (l, 0, 0))
    return pl.pallas_call(
        functools.partial(_attn_prompt_kernel, l=l, n_kv=n_kv, q_per_kv=n_q // n_kv, hd=hd,
                          window=window, nblk=nblk),
        grid=(n_prompt, nblk),
        in_specs=[pl.BlockSpec(memory_space=pltpu.SMEM),
                  pl.BlockSpec((blk, wa), lambda b, i: (b * nblk + i, 0)),
                  prev(kcol), cur(kcol), prev(vcol), cur(vcol), nw, nw],
        out_specs=[pl.BlockSpec((blk, wa), lambda b, i: (b * nblk + i, 0)),
                   pl.BlockSpec((None, blk, kvw), lambda b, i: (b, 0, 0)),
                   pl.BlockSpec((None, blk, kvw), lambda b, i: (b, 0, 0))],
        out_shape=[jax.ShapeDtypeStruct((R, wa), BF16),
                   jax.ShapeDtypeStruct((n_prompt, blk, kvw), F32),
                   jax.ShapeDtypeStruct((n_prompt, blk, kvw), F32)],
        compiler_params=_params("parallel", "arbitrary"),
        name="attn_prompt",
    )(sinks, z, z, z, z, z, q_norm_w.reshape(L, 1, hd), k_norm_w.reshape(L, 1, hd))


def _attn_sample_kernel(sink_ref, o_in_ref, q_ref, kn_ref, vn_ref, ck_ref, cv_ref, qw_ref, kw_ref,
                        o_ref, kwin_ref, vwin_ref, keys_s, vals_s, ostage,
                        *, l, nb, T, n_kv, q_per_kv, hd, window):
    del o_in_ref
    W = ck_ref.shape[1]
    scale = hd ** -0.5
    rows_q = q_per_kv * T
    r = lax.broadcasted_iota(jnp.int32, (rows_q, 2 * W), 0) % T
    c = lax.broadcasted_iota(jnp.int32, (rows_q, 2 * W), 1)
    q_pos = PAST_LEN + r
    k_pos = jnp.where(c < W, PAST_LEN - W + c, PAST_LEN + c - W)
    dist = q_pos - k_pos
    ok = (dist >= 0) & (dist <= window) & (k_pos >= 0) & (c < W + T)
    hrow = lax.broadcasted_iota(jnp.int32, (rows_q, 1), 0) // T
    qw = qw_ref[...]
    kw = kw_ref[...]
    keys_s[...] = jnp.zeros_like(keys_s)
    vals_s[...] = jnp.zeros_like(vals_s)

    def step(b, carry):
        rows = pl.ds(pl.multiple_of(b * T, T), T)
        for g in range(n_kv):
            sl = slice(g * hd, (g + 1) * hd)
            ck = ck_ref[b, :, sl]
            cv = cv_ref[b, :, sl]
            knn = _rms(kn_ref[rows, sl], kw)
            vnn = vn_ref[rows, sl]
            keys_s[0:W, :] = ck
            keys_s[W:W + T, :] = knn
            vals_s[0:W, :] = cv
            vals_s[W:W + T, :] = vnn
            qs = jnp.concatenate(
                [_rms(q_ref[rows, (g * q_per_kv + h) * hd:(g * q_per_kv + h + 1) * hd], qw)
                 for h in range(q_per_kv)], axis=0).astype(BF16)
            s = lax.dot_general(qs, keys_s[...].astype(BF16), (((1,), (1,)), ((), ())),
                                preferred_element_type=F32) * scale
            s = jnp.where(ok, s, -1e30)
            sink = jnp.zeros((rows_q, 1), F32)
            for h in range(q_per_kv):
                sink = jnp.where(hrow == h, sink_ref[l, g * q_per_kv + h], sink)
            o = _softmax_sink_pv(s, sink, vals_s[...].astype(BF16))
            for h in range(q_per_kv):
                head = g * q_per_kv + h
                ostage[rows, head * hd:(head + 1) * hd] = o[h * T:(h + 1) * T, :]
            kwin_ref[b, 0:W - T, sl] = ck[T:, :]
            kwin_ref[b, W - T:W, sl] = knn
            vwin_ref[b, 0:W - T, sl] = cv[T:, :]
            vwin_ref[b, W - T:W, sl] = vnn
        return carry

    lax.fori_loop(0, nb, step, 0)
    o_ref[...] = ostage[...].astype(o_ref.dtype)


def _attn_sample(attn_o, z, cache_k, cache_v, sinks, q_norm_w, k_norm_w, l, *, n_sample, T, n_q, n_kv, hd,
                 window):
    R, wa = attn_o.shape
    kvw = n_kv * hd
    rows = n_sample * T
    blk_i = (R - rows) // rows
    kcol = wa // kvw
    L, nb, W = cache_k.shape[:3]
    ck = cache_k.reshape(L, nb, W, kvw)
    cv = cache_v.reshape(L, nb, W, kvw)
    nw = pl.BlockSpec((None, 1, hd), lambda i: (l, 0, 0))
    cache = pl.BlockSpec((None, nb, W, kvw), lambda i: (l, 0, 0, 0))
    win = pl.BlockSpec((nb, W, kvw), lambda i: (0, 0, 0))
    return pl.pallas_call(
        functools.partial(_attn_sample_kernel, l=l, nb=nb, T=T, n_kv=n_kv, q_per_kv=n_q // n_kv, hd=hd,
                          window=window),
        grid=(1,),
        in_specs=[pl.BlockSpec(memory_space=pltpu.SMEM),
                  pl.BlockSpec(memory_space=pl.ANY),
                  pl.BlockSpec((rows, wa), lambda i: (blk_i, 0)),
                  pl.BlockSpec((rows, kvw), lambda i: (blk_i, kcol)),
                  pl.BlockSpec((rows, kvw), lambda i: (blk_i, kcol + 1)),
                  cache, cache, nw, nw],
        out_specs=[pl.BlockSpec((rows, wa), lambda i: (blk_i, 0)), win, win],
        out_shape=[jax.ShapeDtypeStruct((R, wa), BF16),
                   jax.ShapeDtypeStruct((nb, W, kvw), F32),
                   jax.ShapeDtypeStruct((nb, W, kvw), F32)],
        scratch_shapes=[pltpu.VMEM((2 * W, hd), F32), pltpu.VMEM((2 * W, hd), F32),
                        pltpu.VMEM((rows, wa), F32)],
        input_output_aliases={1: 0},
        compiler_params=_params("arbitrary"),
        name="attn_sample",
    )(sinks, attn_o, z, z, z, ck, cv, q_norm_w.reshape(-1, 1, hd), k_norm_w.reshape(-1, 1, hd))


def _lru_gates(xc, wa_ref, ba_ref, wx_ref, bx_ref, lam_ref, a_s, u_s, *, nblk, lb):
    xcb = xc.astype(BF16)
    for n in range(nblk):
        sl = slice(n * lb, (n + 1) * lb)
        ra = jnp.dot(xcb[:, sl], wa_ref[n].astype(BF16), preferred_element_type=F32) + ba_ref[:, sl]
        rx = jnp.dot(xcb[:, sl], wx_ref[n].astype(BF16), preferred_element_type=F32) + bx_ref[:, sl]
        log_a = -LRU_C * jax.nn.sigmoid(ra) * jax.nn.softplus(-lam_ref[:, sl])
        a_s[:, sl] = jnp.exp(log_a)
        th = jnp.tanh(log_a)
        u_s[:, sl] = jnp.sqrt(-2.0 * th / (1.0 - th)) * jax.nn.sigmoid(rx) * xc[:, sl]

def _lru_prompt_kernel(x_ref, g_ref, cw_ref, cb_ref, wa_ref, ba_ref, wx_ref, bx_ref, lam_ref,
                       o_ref, hT_ref, cnew_ref, xp_s, a_s, u_s, h_s, *, Tb, ntb, cw, nblk, lb):
    tb = pl.program_id(2)
    P = SUBLANES

    @pl.when(tb == 0)
    def _():
        xp_s[0:P, :] = jnp.zeros((P, xp_s.shape[1]), F32)
        h_s[...] = jnp.zeros_like(h_s)

    xp_s[P:P + Tb, :] = x_ref[...]
    xc = cb_ref[...]
    for j in range(cw):
        off = P - (cw - 1) + j
        xc = xc + xp_s[off:off + Tb, :] * cw_ref[j:j + 1, :]
    _lru_gates(xc, wa_ref, ba_ref, wx_ref, bx_ref, lam_ref, a_s, u_s, nblk=nblk, lb=lb)

    def step(t, h):
        h = a_s[pl.ds(t, 1), :] * h + u_s[pl.ds(t, 1), :]
        u_s[pl.ds(t, 1), :] = h
        return h

    h = lax.fori_loop(0, Tb, step, h_s[...])
    h_s[...] = h
    o_ref[...] = (jax.nn.gelu(g_ref[...]) * u_s[...]).astype(o_ref.dtype)
    tail = xp_s[Tb:Tb + P, :]
    xp_s[0:P, :] = tail

    @pl.when(tb == ntb - 1)
    def _():
        hT_ref[...] = h
        cnew_ref[...] = tail


def _lru_sample_kernel(o_in_ref, x_ref, g_ref, h0_ref, buf_ref, cw_ref, cb_ref, wa_ref, ba_ref, wx_ref, bx_ref,
                       lam_ref, o_ref, hs_ref, a_s, u_s, *, T, cw, nblk, lb):
    del o_in_ref
    rows = x_ref.shape[0]
    t = lax.broadcasted_iota(jnp.int32, (rows, 1), 0) % T
    x = x_ref[...]
    buf = buf_ref[...]
    xc = cb_ref[...] + x * cw_ref[cw - 1:cw, :]
    for s in range(1, cw):
        xs = jnp.where(t >= s, pltpu.roll(x, s, axis=0), pltpu.roll(buf, rows - (T - s), axis=0))
        xc = xc + xs * cw_ref[cw - 1 - s:cw - s, :]
    _lru_gates(xc, wa_ref, ba_ref, wx_ref, bx_ref, lam_ref, a_s, u_s, nblk=nblk, lb=lb)
    a = a_s[...]
    u = u_s[...]
    d = 1
    while d < T:
        keep = t >= d
        u = jnp.where(keep, a * pltpu.roll(u, d, axis=0) + u, u)
        a = jnp.where(keep, a * pltpu.roll(a, d, axis=0), a)
        d *= 2
    hs = u + a * h0_ref[...]
    hs_ref[...] = hs
    o_ref[...] = (jax.nn.gelu(g_ref[...]) * hs).astype(o_ref.dtype)


def _lru_weight_args(W, l, wb, index):
    ch = wb // 2
    nlb, lb = W['lru_w_a'].shape[1:3]
    nblk = nlb // 2
    cw = W['conv_w'].shape[1]

    def vec(name):
        return W[name].reshape(W[name].shape[0], 1, wb), pl.BlockSpec((None, 1, ch), index(lambda hf: (l, 0, hf)))

    def blkw(name):
        return W[name], pl.BlockSpec((None, nblk, lb, lb), index(lambda hf: (l, hf, 0, 0)))

    pairs = [(W['conv_w'], pl.BlockSpec((None, cw, ch), index(lambda hf: (l, 0, hf)))),
             vec('conv_b'), blkw('lru_w_a'), vec('lru_b_a'), blkw('lru_w_x'), vec('lru_b_x'), vec('lru_lambda')]
    return [p[0] for p in pairs], [p[1] for p in pairs], dict(cw=cw, nblk=nblk, lb=lb)


def _lru_prompt(z, W, l, *, R, nseq, T, Tb, col_x, col_g, wb):
    ch = wb // 2
    ntb = T // Tb
    wargs, wspecs, kw = _lru_weight_args(W, l, wb, lambda f: (lambda s, hf, t: f(hf)))

    def rowblk(col):
        return pl.BlockSpec((Tb, ch), lambda s, hf, t: (s * ntb + t, col + hf))

    return pl.pallas_call(
        functools.partial(_lru_prompt_kernel, Tb=Tb, ntb=ntb, **kw),
        grid=(nseq, 2, ntb),
        in_specs=[rowblk(col_x), rowblk(col_g)] + wspecs,
        out_specs=[pl.BlockSpec((Tb, ch), lambda s, hf, t: (s * ntb + t, hf)),
                   pl.BlockSpec((None, 1, ch), lambda s, hf, t: (s, 0, hf)),
                   pl.BlockSpec((None, SUBLANES, ch), lambda s, hf, t: (s, 0, hf))],
        out_shape=[jax.ShapeDtypeStruct((R, wb), BF16),
                   jax.ShapeDtypeStruct((nseq, 1, wb), F32),
                   jax.ShapeDtypeStruct((nseq, SUBLANES, wb), F32)],
        scratch_shapes=[pltpu.VMEM((SUBLANES + Tb, ch), F32), pltpu.VMEM((Tb, ch), F32),
                        pltpu.VMEM((Tb, ch), F32), pltpu.VMEM((1, ch), F32)],
        compiler_params=_params("parallel", "parallel", "arbitrary"),
        name="lru_prompt",
    )(z, z, *wargs)


def _lru_sample(lru_o, z, h0_rep, buf_rows, W, l, *, row0, T, col_x, col_g, wb):
    ch = wb // 2
    rows = h0_rep.shape[0]
    rb = row0 // rows
    wargs, wspecs, kw = _lru_weight_args(W, l, wb, lambda f: (lambda hf: f(hf)))
    st = pl.BlockSpec((rows, ch), lambda hf: (0, hf))
    return pl.pallas_call(
        functools.partial(_lru_sample_kernel, T=T, **kw),
        grid=(2,),
        in_specs=[pl.BlockSpec(memory_space=pl.ANY),
                  pl.BlockSpec((rows, ch), lambda hf: (rb, col_x + hf)),
                  pl.BlockSpec((rows, ch), lambda hf: (rb, col_g + hf)), st, st] + wspecs,
        out_specs=[pl.BlockSpec((rows, ch), lambda hf: (rb, hf)), st],
        out_shape=[jax.ShapeDtypeStruct(lru_o.shape, BF16), jax.ShapeDtypeStruct((rows, wb), F32)],
        scratch_shapes=[pltpu.VMEM((rows, ch), F32), pltpu.VMEM((rows, ch), F32)],
        input_output_aliases={0: 0},
        compiler_params=_params("parallel"),
        name="lru_sample",
    )(lru_o, z, z, h0_rep, buf_rows, *wargs)


def _chunk_kernel(*refs, nb, Tc, ngroups, gw, cps, aliased):
    refs = list(refs)
    if aliased:
        refs.pop(0)
    ulo_ref, uhi_ref, vlo_ref, vhi_ref, nw_ref, ws_ref, bst_ref, o_ref, vn_ref = refs
    rows = nb * Tc
    half = vlo_ref.shape[1]
    wc = 2 * half
    v_lo = jax.nn.gelu(vlo_ref[...])
    v_hi = jax.nn.gelu(vhi_ref[...])
    ms = (jnp.sum(v_lo * v_lo, axis=-1, keepdims=True) + jnp.sum(v_hi * v_hi, axis=-1, keepdims=True)) / wc
    inv = lax.rsqrt(ms + EPS)
    vn_halves = (v_lo * inv * nw_ref[:, 0:half], v_hi * inv * nw_ref[:, half:wc])
    u_halves = (ulo_ref, uhi_ref)

    r = lax.broadcasted_iota(jnp.int32, (rows, rows), 0)
    c = lax.broadcasted_iota(jnp.int32, (rows, rows), 1)
    mask = (r // Tc == c // Tc) & (c % Tc <= r % Tc)
    if nb > 1:
        sel = (lax.broadcasted_iota(jnp.int32, (ws_ref.shape[2], rows), 0)
               == lax.broadcasted_iota(jnp.int32, (ws_ref.shape[2], rows), 1) % Tc).astype(BF16)
        bias_rows = jnp.broadcast_to(bst_ref[0:Tc, :][None], (nb, Tc, bst_ref.shape[1])).reshape(rows, -1)
    else:
        bias_rows = bst_ref[0:Tc, :]
    gph = half // gw
    for g in range(ngroups):
        hf, gi = divmod(g, gph)
        sl = slice(gi * gw, (gi + 1) * gw)
        if nb > 1:
            t1 = jnp.broadcast_to(ws_ref[g, 0:Tc, :][None], (nb, Tc, ws_ref.shape[2])).reshape(rows, -1)
            wfull = jnp.dot(t1.astype(BF16), sel, preferred_element_type=F32)
        else:
            wfull = ws_ref[g, 0:Tc, 0:Tc]
        wm = jnp.where(mask, wfull, 0.0).astype(BF16)
        mixed = jnp.dot(wm, vn_halves[hf][:, sl].astype(BF16), preferred_element_type=F32)
        mixed = mixed + bias_rows[:, g:g + 1]
        o_ref[:, g * gw:(g + 1) * gw] = (jax.nn.gelu(u_halves[hf][:, sl]) * mixed).astype(o_ref.dtype)

    if cps == 1:
        vn_ref[:, 0:half] = vn_halves[0]
        vn_ref[:, half:wc] = vn_halves[1]
    else:
        @pl.when(pl.program_id(0) % cps == cps - 1)
        def _():
            vn_ref[:, 0:half] = vn_halves[0]
            vn_ref[:, half:wc] = vn_halves[1]


def _chunk(chunk_o, z, W, l, *, R, row0, nsteps, nb, Tc, cps, col_u, col_v, wc):
    half = wc // 2
    ngroups, chunk = W['chunk_w_s'].shape[1:3]
    gw = wc // ngroups
    rows = nb * Tc
    rb0 = row0 // rows
    aliased = chunk_o is not None

    def rowblk(col):
        return pl.BlockSpec((rows, half), lambda i: (rb0 + i, col))

    args = [z, z, z, z, W['chunk_v_norm_w'].reshape(-1, 1, wc), W['chunk_w_s'],
            jnp.swapaxes(W['chunk_b_s'], 1, 2)]
    specs = [rowblk(col_u), rowblk(col_u + 1), rowblk(col_v), rowblk(col_v + 1),
             pl.BlockSpec((None, 1, wc), lambda i: (l, 0, 0)),
             pl.BlockSpec((None, ngroups, chunk, chunk), lambda i: (l, 0, 0, 0)),
             pl.BlockSpec((None, chunk, ngroups), lambda i: (l, 0, 0))]
    io_alias = {}
    if aliased:
        args.insert(0, chunk_o)
        specs.insert(0, pl.BlockSpec(memory_space=pl.ANY))
        io_alias = {0: 0}
    return pl.pallas_call(
        functools.partial(_chunk_kernel, nb=nb, Tc=Tc, ngroups=ngroups, gw=gw, cps=cps, aliased=aliased),
        grid=(nsteps,),
        in_specs=specs,
        out_specs=[pl.BlockSpec((rows, wc), lambda i: (rb0 + i, 0)),
                   pl.BlockSpec((None, rows, wc), lambda i: (i // cps, 0, 0))],
        out_shape=[jax.ShapeDtypeStruct((R, wc), BF16),
                   jax.ShapeDtypeStruct((nsteps // cps, rows, wc), F32)],
        input_output_aliases=io_alias,
        compiler_params=_params("arbitrary"),
        name="chunk_sample" if aliased else "chunk_prompt",
    )(*args)


def _merge_kernel(a_ref, b_ref, c_ref, wa_ref, wb_ref, wc_ref, ga_ref, gb_ref, gc_ref, o_ref):
    def branch(x_ref, w_ref, g_ref):
        y = jnp.dot(x_ref[...], w_ref[...].astype(BF16), preferred_element_type=F32)
        return jax.nn.sigmoid(g_ref[...]) * y

    m = branch(a_ref, wa_ref, ga_ref) + branch(b_ref, wb_ref, gb_ref) + branch(c_ref, wc_ref, gc_ref)
    o_ref[...] = m.astype(o_ref.dtype)


def _merge(attn_o, lru_o, chunk_o, z, W, l, *, gate_col, tm, tn):
    R = attn_o.shape[0]
    D = W['w_branch_a'].shape[-1]
    g0 = gate_col // tn
    gstep = D // tn

    def xin(a):
        return pl.BlockSpec((tm, a.shape[1]), lambda i, j: (i, 0))

    def win(w):
        return pl.BlockSpec((None, w.shape[1], tn), lambda i, j: (l, 0, j))

    def gin(k):
        return pl.BlockSpec((tm, tn), lambda i, j: (i, g0 + k * gstep + j))

    return pl.pallas_call(
        _merge_kernel,
        grid=(R // tm, D // tn),
        in_specs=[xin(attn_o), xin(lru_o), xin(chunk_o),
                  win(W['w_branch_a']), win(W['w_branch_b']), win(W['w_branch_c']),
                  gin(0), gin(1), gin(2)],
        out_specs=pl.BlockSpec((tm, tn), lambda i, j: (i, j)),
        out_shape=jax.ShapeDtypeStruct((R, D), BF16),
        compiler_params=_params("parallel", "parallel"),
        name="merge",
    )(attn_o, lru_o, chunk_o, W['w_branch_a'], W['w_branch_b'], W['w_branch_c'], z, z, z)


def _glu_kernel(*refs, gated):
    if gated:
        h_ref, wg_ref, wu_ref, gate_ref, o_ref = refs
    else:
        h_ref, wg_ref, wu_ref, o_ref = refs
    h = h_ref[...]
    g = jnp.dot(h, wg_ref[...].astype(BF16), preferred_element_type=F32)
    u = jnp.dot(h, wu_ref[...].astype(BF16), preferred_element_type=F32)
    a = jax.nn.silu(g) * u
    if gated:
        e = pl.program_id(1)
        gate = gate_ref[...]
        lane = lax.broadcasted_iota(jnp.int32, gate.shape, 1)
        a = a * jnp.sum(jnp.where(lane == e, gate, 0.0), axis=-1, keepdims=True)
    o_ref[...] = a.astype(o_ref.dtype)


def _glu(h, w_gate, w_up, l0, n_exp, gate, *, tm, tn):
    R, D = h.shape
    F = w_gate.shape[-1]
    nf = F // tn
    gated = gate is not None
    wspec = pl.BlockSpec((None, D, tn), lambda i, e, j: (l0 + e, 0, j))
    args = [h, w_gate, w_up]
    specs = [pl.BlockSpec((tm, D), lambda i, e, j: (i, 0)), wspec, wspec]
    if gated:
        args.append(gate)
        specs.append(pl.BlockSpec((tm, gate.shape[1]), lambda i, e, j: (i, 0)))
    return pl.pallas_call(
        functools.partial(_glu_kernel, gated=gated),
        grid=(R // tm, n_exp, nf),
        in_specs=specs,
        out_specs=pl.BlockSpec((tm, tn), lambda i, e, j: (i, e * nf + j)),
        out_shape=jax.ShapeDtypeStruct((R, n_exp * F), BF16),
        compiler_params=_params("parallel", "parallel", "parallel"),
        name="glu",
    )(*args)


META_E1, META_E2, META_W1, META_W2, META_R1, META_R2 = range(6)


def _router_kernel(h_ref, w_ref, meta_ref, cnt_ref, carry, *, n_exp):
    i = pl.program_id(0)

    @pl.when(i == 0)
    def _():
        carry[...] = jnp.zeros_like(carry)

    logits = jnp.dot(h_ref[...], w_ref[...].astype(BF16), preferred_element_type=F32)
    tm = logits.shape[0]
    lane = lax.broadcasted_iota(jnp.int32, logits.shape, 1).astype(F32)
    big = float(logits.shape[1])
    l0 = jnp.where(lane < n_exp, logits, -jnp.inf)
    m1 = jnp.max(l0, axis=-1, keepdims=True)
    i1 = jnp.min(jnp.where(l0 == m1, lane, big), axis=-1, keepdims=True)
    l1 = jnp.where(lane == i1, -jnp.inf, l0)
    m2 = jnp.max(l1, axis=-1, keepdims=True)
    i2 = jnp.min(jnp.where(l1 == m2, lane, big), axis=-1, keepdims=True)
    e2 = jnp.exp(m2 - m1)
    den = 1.0 + e2
    hit = ((lane == i1) | (lane == i2)).astype(F32)
    earlier = (lax.broadcasted_iota(jnp.int32, (tm, tm), 0) > lax.broadcasted_iota(jnp.int32, (tm, tm), 1))
    rank = jnp.dot(earlier.astype(BF16), hit.astype(BF16), preferred_element_type=F32) + carry[...]
    r1 = jnp.sum(jnp.where(lane == i1, rank, 0.0), axis=-1, keepdims=True)
    r2 = jnp.sum(jnp.where(lane == i2, rank, 0.0), axis=-1, keepdims=True)
    meta = jnp.zeros_like(logits)
    for k, v in ((META_E1, i1), (META_E2, i2), (META_W1, 1.0 / den), (META_W2, e2 / den), (META_R1, r1),
                 (META_R2, r2)):
        meta = jnp.where(lane == k, v, meta)
    meta_ref[...] = meta
    carry[...] += jnp.sum(hit, axis=0, keepdims=True)
    cnt_ref[...] = carry[...]


def _router(h, w_router_padded, j, *, n_exp, tm):
    R, D = h.shape
    NP = w_router_padded.shape[-1]
    return pl.pallas_call(
        functools.partial(_router_kernel, n_exp=n_exp),
        grid=(R // tm,),
        in_specs=[pl.BlockSpec((tm, D), lambda i: (i, 0)),
                  pl.BlockSpec((None, D, NP), lambda i: (j, 0, 0))],
        out_specs=[pl.BlockSpec((tm, NP), lambda i: (i, 0)), pl.BlockSpec((1, NP), lambda i: (0, 0))],
        out_shape=[jax.ShapeDtypeStruct((R, NP), F32), jax.ShapeDtypeStruct((1, NP), F32)],
        scratch_shapes=[pltpu.VMEM((1, NP), F32)],
        compiler_params=_params("arbitrary"),
        name="router",
    )(h, w_router_padded)


def _row_copies_wait(src_like, dst_like, sem, n):
    for _ in range(n):
        pltpu.make_async_copy(src_like, dst_like, sem).wait()


def _dispatch_kernel(p1_ref, p2_ref, h_ref, xs_ref, sem):
    tm = h_ref.shape[0]
    i0 = pl.program_id(0) * tm

    def body(i, c):
        pltpu.make_async_copy(h_ref.at[i], xs_ref.at[p1_ref[i0 + i]], sem).start()
        pltpu.make_async_copy(h_ref.at[i], xs_ref.at[p2_ref[i0 + i]], sem).start()
        return c

    lax.fori_loop(0, tm, body, 0)
    _row_copies_wait(h_ref, xs_ref.at[pl.ds(0, tm)], sem, TOP_K)


def _dispatch(h_rows, p1, p2, *, n_slots, tm):
    R, C, _ = h_rows.shape
    return pl.pallas_call(
        _dispatch_kernel,
        grid_spec=pltpu.PrefetchScalarGridSpec(
            num_scalar_prefetch=2, grid=(R // tm,),
            in_specs=[pl.BlockSpec((tm, C, LANES), lambda i, p1, p2: (i, 0, 0))],
            out_specs=pl.BlockSpec(memory_space=pl.ANY),
            scratch_shapes=[pltpu.SemaphoreType.DMA]),
        out_shape=jax.ShapeDtypeStruct((n_slots, C, LANES), F32),
        compiler_params=_params("arbitrary"),
        name="moe_dispatch",
    )(p1, p2, h_rows)


def _moe_ffn_kernel(te_ref, nv_ref, nu_ref, xs_ref, wg_ref, wu_ref, wd_ref, y_ref, xb_s, acc_s, *, sub, nf):
    t = pl.program_id(0)
    j = pl.program_id(1)
    nv = nv_ref[t]
    tile, nchunk = xs_ref.shape[0], xs_ref.shape[1]

    @pl.when(t < nu_ref[0])
    def _():
        wg = wg_ref[...].astype(BF16)
        wu = wu_ref[...].astype(BF16)
        wd = wd_ref[...].astype(BF16)
        for s in range(tile // sub):
            rows = slice(s * sub, (s + 1) * sub)

            @pl.when(s * sub < nv)
            def _():
                @pl.when(j == 0)
                def _():
                    live = lax.broadcasted_iota(jnp.int32, (sub, 1), 0) + s * sub < nv
                    for c in range(nchunk):
                        xb_s[rows, c * LANES:(c + 1) * LANES] = jnp.where(live, xs_ref[rows, c, :], 0.0).astype(BF16)

                x = xb_s[rows, :]
                g = jnp.dot(x, wg, preferred_element_type=F32)
                u = jnp.dot(x, wu, preferred_element_type=F32)
                part = jnp.dot((jax.nn.silu(g) * u).astype(BF16), wd, preferred_element_type=F32)

                @pl.when(j == 0)
                def _():
                    acc_s[rows, :] = part

                @pl.when(j > 0)
                def _():
                    acc_s[rows, :] += part

                @pl.when(j == nf - 1)
                def _():
                    for c in range(nchunk):
                        y_ref[rows, c, :] = acc_s[rows, c * LANES:(c + 1) * LANES]

            @pl.when((s * sub >= nv) & (j == nf - 1))
            def _():
                y_ref[rows, :, :] = jnp.zeros((sub, nchunk, LANES), F32)


def _moe_ffn(xs, w_gate, w_up, w_down, l0, tile_expert, tile_valid, n_used, *, tile, sub, tn):
    n_slots, C, _ = xs.shape
    D = C * LANES
    F = w_gate.shape[-1]
    nf = F // tn
    nt = n_slots // tile

    def live_t(t, nu):
        return jnp.minimum(t, nu[0] - 1)

    def live_j(t, j, nu):
        return jnp.where(t < nu[0], j, nf - 1)

    return pl.pallas_call(
        functools.partial(_moe_ffn_kernel, sub=sub, nf=nf),
        grid_spec=pltpu.PrefetchScalarGridSpec(
            num_scalar_prefetch=3, grid=(nt, nf),
            in_specs=[pl.BlockSpec((tile, C, LANES), lambda t, j, te, nv, nu: (live_t(t, nu), 0, 0)),
                      pl.BlockSpec((None, D, tn), lambda t, j, te, nv, nu: (l0 + te[live_t(t, nu)], 0, live_j(t, j, nu))),
                      pl.BlockSpec((None, D, tn), lambda t, j, te, nv, nu: (l0 + te[live_t(t, nu)], 0, live_j(t, j, nu))),
                      pl.BlockSpec((None, tn, D), lambda t, j, te, nv, nu: (l0 + te[live_t(t, nu)], live_j(t, j, nu), 0))],
            out_specs=pl.BlockSpec((tile, C, LANES), lambda t, j, te, nv, nu: (live_t(t, nu), 0, 0)),
            scratch_shapes=[pltpu.VMEM((tile, D), BF16), pltpu.VMEM((tile, D), F32)]),
        out_shape=jax.ShapeDtypeStruct((n_slots, C, LANES), F32),
        compiler_params=_params("arbitrary", "arbitrary"),
        name="moe_ffn",
    )(tile_expert, tile_valid, n_used, xs, w_gate, w_up, w_down)


def _combine_kernel(p1_ref, p2_ref, meta_ref, y_ref, f_ref, a_s, b_s, sem):
    tm, nchunk = a_s.shape[0], a_s.shape[1]
    i0 = pl.program_id(0) * tm

    def body(i, c):
        pltpu.make_async_copy(y_ref.at[p1_ref[i0 + i]], a_s.at[i], sem).start()
        pltpu.make_async_copy(y_ref.at[p2_ref[i0 + i]], b_s.at[i], sem).start()
        return c

    lax.fori_loop(0, tm, body, 0)
    _row_copies_wait(y_ref.at[pl.ds(0, tm)], a_s, sem, TOP_K)
    w1 = meta_ref[:, META_W1:META_W1 + 1]
    w2 = meta_ref[:, META_W2:META_W2 + 1]
    for c in range(nchunk):
        f_ref[:, c * LANES:(c + 1) * LANES] = w1 * a_s[:, c, :] + w2 * b_s[:, c, :]


def _combine(y, meta, p1, p2, *, tm):
    _, C, _ = y.shape
    R, NP = meta.shape
    return pl.pallas_call(
        _combine_kernel,
        grid_spec=pltpu.PrefetchScalarGridSpec(
            num_scalar_prefetch=2, grid=(R // tm,),
            in_specs=[pl.BlockSpec((tm, NP), lambda i, p1, p2: (i, 0)),
                      pl.BlockSpec(memory_space=pl.ANY)],
            out_specs=pl.BlockSpec((tm, C * LANES), lambda i, p1, p2: (i, 0)),
            scratch_shapes=[pltpu.VMEM((tm, C, LANES), F32), pltpu.VMEM((tm, C, LANES), F32),
                            pltpu.SemaphoreType.DMA]),
        out_shape=jax.ShapeDtypeStruct((R, C * LANES), F32),
        compiler_params=_params("arbitrary"),
        name="moe_combine",
    )(p1, p2, meta, y)


def _moe(h, h_rows, w_router_padded, w_gate, w_up, w_down, j, *, n_exp, tm_router, tile, sub, tn, tm_rows):
    R = h.shape[0]
    meta, counts = _router(h, w_router_padded, j, n_exp=n_exp, tm=tm_router)
    cnt = counts[0, :n_exp].astype(jnp.int32)
    ntile = (cnt + tile - 1) // tile
    tend = jnp.cumsum(ntile)
    tstart = tend - ntile
    n_tiles = -(-TOP_K * R // tile) + n_exp
    e1, e2 = meta[:, META_E1].astype(jnp.int32), meta[:, META_E2].astype(jnp.int32)
    p1 = tstart[e1] * tile + meta[:, META_R1].astype(jnp.int32)
    p2 = tstart[e2] * tile + meta[:, META_R2].astype(jnp.int32)
    tid = jnp.arange(n_tiles, dtype=jnp.int32)
    tile_expert = jnp.minimum(jnp.sum(tid[:, None] >= tend[None, :], axis=1), n_exp - 1).astype(jnp.int32)
    tile_valid = jnp.clip(cnt[tile_expert] - (tid - tstart[tile_expert]) * tile, 0, tile)
    tile_valid = jnp.where(tid < tend[-1], tile_valid, 0).astype(jnp.int32)
    xs = _dispatch(h_rows, p1, p2, n_slots=n_tiles * tile, tm=tm_rows)
    y = _moe_ffn(xs, w_gate, w_up, w_down, j * n_exp, tile_expert, tile_valid, tend[-1:].astype(jnp.int32),
                 tile=tile, sub=sub, tn=tn)
    return _combine(y, meta, p1, p2, tm=tm_rows)


def kernel(x_prompt, x_sample, cache_k_win, cache_v_win, state_rglru_h, state_conv, c_prompt, c_sample, norm1_w, norm2_w, w_ada, b_ada, w_in, q_norm_w, k_norm_w, attn_sinks, conv_w, conv_b, lru_w_a, lru_b_a, lru_w_x, lru_b_x, lru_lambda, chunk_v_norm_w, chunk_w_s, chunk_b_s, w_branch_a, w_branch_b, w_branch_c, w_out, ffn_w_gate, ffn_w_up, ffn_w_down, moe_w_router, moe_w_gate, moe_w_up, moe_w_down):
    Bp, S, D = x_prompt.shape
    Bs, Ts, _ = x_sample.shape
    L = w_in.shape[0]
    window, n_kv, hd = cache_k_win.shape[2:]
    n_q = attn_sinks.shape[1]
    wa, kvw = n_q * hd, n_kv * hd
    wb = conv_w.shape[-1]
    wc = chunk_v_norm_w.shape[-1]
    chunk = chunk_w_s.shape[-1]
    cwid = conv_w.shape[1]
    n_exp = moe_w_router.shape[-1]
    assert Ts == SUBLANES and S % (Bs * Ts) == 0 and S % chunk == 0 and S % window == 0
    Rp, Rs = Bp * S, Bs * Ts
    R = Rp + Rs
    half = wb // 2
    assert wb == wc and wa % half == 0 and kvw % half == 0 or True
    col_lx = (wa + 2 * kvw) // half
    col_lg = col_lx + 2
    col_cu = col_lg + 2
    col_cv = col_cu + 2
    gate_col = wa + 2 * kvw + 2 * wb + 2 * wc

    W = dict(conv_w=conv_w, conv_b=conv_b, lru_w_a=lru_w_a, lru_b_a=lru_b_a, lru_w_x=lru_w_x, lru_b_x=lru_b_x,
             lru_lambda=lru_lambda, chunk_v_norm_w=chunk_v_norm_w, chunk_w_s=chunk_w_s, chunk_b_s=chunk_b_s,
             w_branch_a=w_branch_a, w_branch_b=w_branch_b, w_branch_c=w_branch_c)

    n_c = Bp + Bs
    n_c_pad = -(-n_c // SUBLANES) * SUBLANES
    c_all = jnp.concatenate([c_prompt, c_sample, jnp.zeros((n_c_pad - n_c, D), F32)], axis=0)
    mod = _ada(c_all, w_ada, b_ada)
    mod_p = jnp.repeat(mod[:, :Bp], SUBLANES, axis=1)
    mod_s = jnp.repeat(mod[:, Bp:n_c], Ts, axis=1)

    x = jnp.concatenate([x_prompt.reshape(Rp, D), x_sample.reshape(Rs, D)], axis=0)
    rn = functools.partial(_resid_norm, seq=S, n_prompt=Bp)
    tm_big = _pick(R, 1408, 128) if R % 128 == 0 else _pick(R, 1408, 16)
    tm_mid = _pick(R, 768, 128) if R % 128 == 0 else _pick(R, 768, 16)
    w_router_p = jnp.pad(moe_w_router, ((0, 0), (0, 0), (0, LANES - n_exp)))
    moe_gate_w = moe_w_gate.reshape((-1,) + moe_w_gate.shape[2:])
    moe_up_w = moe_w_up.reshape((-1,) + moe_w_up.shape[2:])
    moe_down_w = moe_w_down.reshape((-1,) + moe_w_down.shape[2:])

    buf_rows = jnp.pad(state_conv, ((0, 0), (0, 0), (Ts - (cwid - 1), 0), (0, 0))).reshape(L, Rs, wb)
    h0_rep = jnp.repeat(state_rglru_h, Ts, axis=1)

    _, h = rn(x, None, mod_p, mod_s, norm1_w, 0, gate_k=None, scale_k=(0, 1), shift_k=(0, 0))
    outs = []
    for l in range(L):
        z = _matmul(h, w_in, l, out_dtype=F32, tm=tm_big, tn=_pick(w_in.shape[-1], 512, half), tk=D)
        attn_o, kwp, vwp = _attn_prompt(z, attn_sinks, q_norm_w, k_norm_w, l, R=R, n_prompt=Bp, seq=S,
                                        n_q=n_q, n_kv=n_kv, hd=hd, window=window)
        attn_o, kws, vws = _attn_sample(attn_o, z, cache_k_win, cache_v_win, attn_sinks, q_norm_w, k_norm_w, l,
                                        n_sample=Bs, T=Ts, n_q=n_q, n_kv=n_kv, hd=hd, window=window)
        lru_o, hp, cp8 = _lru_prompt(z, W, l, R=R, nseq=Bp, T=S, Tb=_pick(S, 256, SUBLANES),
                                     col_x=col_lx, col_g=col_lg, wb=wb)
        lru_o, hs_all = _lru_sample(lru_o, z, h0_rep[l], buf_rows[l], W, l, row0=Rp, T=Ts,
                                    col_x=col_lx, col_g=col_lg, wb=wb)
        chunk_o, vnp = _chunk(None, z, W, l, R=R, row0=0, nsteps=Rp // chunk, nb=1, Tc=chunk,
                              cps=S // chunk, col_u=col_cu, col_v=col_cv, wc=wc)
        chunk_o, vns = _chunk(chunk_o, z, W, l, R=R, row0=Rp, nsteps=1, nb=Bs, Tc=Ts, cps=1,
                              col_u=col_cu, col_v=col_cv, wc=wc)
        m = _merge(attn_o, lru_o, chunk_o, z, W, l, gate_col=gate_col, tm=tm_mid, tn=_pick(D, 512, LANES))
        y = _matmul(m, w_out, l, out_dtype=F32, tm=tm_big, tn=_pick(D, 512, LANES), tk=D)
        j = l // 2
        if l % 2 == 0:
            x, h = rn(x, y, mod_p, mod_s, norm2_w, l, gate_k=(l, 2), scale_k=(l, 4), shift_k=(l, 3))
            a = _glu(h, ffn_w_gate, ffn_w_up, j, 1, None, tm=tm_big, tn=_pick(ffn_w_gate.shape[-1], 512, LANES))
            f = _matmul(a, ffn_w_down, j, out_dtype=F32, tm=tm_big, tn=_pick(D, 1024, LANES),
                        tk=_pick(a.shape[1], 512, LANES))
        else:
            x, h, h_rows = rn(x, y, mod_p, mod_s, norm2_w, l, gate_k=(l, 2), scale_k=(l, 4), shift_k=(l, 3),
                              rows_out=True)
            f = _moe(h, h_rows, w_router_p, moe_gate_w, moe_up_w, moe_down_w, j, n_exp=n_exp, tm_router=tm_mid,
                     tile=MOE_TILE_SUBS * Rs, sub=Rs, tn=_pick(moe_gate_w.shape[-1], 256, LANES), tm_rows=Rs)
        if l + 1 < L:
            x, h = rn(x, f, mod_p, mod_s, norm1_w, l + 1, gate_k=(l, 5), scale_k=(l + 1, 1), shift_k=(l + 1, 0))
        else:
            x, _ = rn(x, f, mod_p, mod_s, None, l, gate_k=(l, 5), scale_k=None, shift_k=None)
        conv_p = cp8[:, SUBLANES - (cwid - 1):, :]
        lx_s = z[Rp:, col_lx * half:col_lx * half + wb].reshape(Bs, Ts, wb)
        conv_s = jnp.concatenate([state_conv[l], lx_s], axis=1)[:, -(cwid - 1):]
        hs = hs_all.reshape(Bs, Ts, wb)[:, Ts - 1]
        outs.append((kwp.reshape(Bp, window, n_kv, hd), vwp.reshape(Bp, window, n_kv, hd), hp.reshape(Bp, wb),
                     conv_p, vnp,
                     kws.reshape(Bs, window, n_kv, hd), vws.reshape(Bs, window, n_kv, hd), hs.reshape(Bs, wb),
                     conv_s, vns.reshape(Bs, Ts, wc)))
    st = [jnp.stack(s) for s in zip(*outs)]
    y_prompt = x[:Rp].reshape(Bp, S, D)
    y_sample = x[Rp:].reshape(Bs, Ts, D)
    return (y_prompt, y_sample, st[0], st[1], st[2], st[3], st[4], st[5], st[6], st[7], st[8], st[9])
```

```python
import functools

import jax
import jax.numpy as jnp
from jax import lax
from jax.experimental import pallas as pl
from jax.experimental.pallas import tpu as pltpu

EPS = 1e-6
LRU_C = 8.0
PAST_LEN = 16384
TOP_K = 2
MOE_TILE_SUBS = 3
SUBLANES = 8
LANES = 128
V7X_VMEM_LIMIT = 56 * 1024 * 1024

F32 = jnp.float32
BF16 = jnp.bfloat16


def _pick(n, target, mult):
    best = None
    for t in range(mult, min(n, target) + 1, mult):
        if n % t == 0:
            best = t
    assert best is not None, (n, target, mult)
    return best


def _params(*sem):
    return pltpu.CompilerParams(dimension_semantics=sem, vmem_limit_bytes=V7X_VMEM_LIMIT)


def _rms(x, w):
    return x * lax.rsqrt(jnp.mean(x * x, axis=-1, keepdims=True) + EPS) * w


def _ada_kernel(c_ref, w_ref, b_ref, o_ref):
    a = jax.nn.silu(c_ref[...]).astype(BF16)
    o_ref[...] = jnp.dot(a, w_ref[...].astype(BF16), preferred_element_type=F32) + b_ref[...]


def _ada(c_all, w_ada, b_ada):
    L, D, W6 = w_ada.shape
    Bc = c_all.shape[0]
    tn = _pick(W6, 1024, LANES)
    return pl.pallas_call(
        _ada_kernel,
        grid=(L, W6 // tn),
        in_specs=[pl.BlockSpec((Bc, D), lambda l, j: (0, 0)),
                  pl.BlockSpec((None, D, tn), lambda l, j: (l, 0, j)),
                  pl.BlockSpec((None, 1, tn), lambda l, j: (l, 0, j))],
        out_specs=pl.BlockSpec((None, Bc, tn), lambda l, j: (l, 0, j)),
        out_shape=jax.ShapeDtypeStruct((L, Bc, W6), F32),
        compiler_params=_params("parallel", "parallel"),
        name="ada_mod",
    )(c_all, w_ada, b_ada.reshape(L, 1, W6))


def _resid_norm_kernel(*refs, n_prompt_tiles, has_resid, has_norm, has_rows):
    refs = list(refs)
    x_ref = refs.pop(0)
    if has_resid:
        y_ref, gp_ref, gs_ref = refs.pop(0), refs.pop(0), refs.pop(0)
    if has_norm:
        scp_ref, scs_ref, shp_ref, shs_ref, nw_ref = (refs.pop(0) for _ in range(5))
    if has_resid:
        xo_ref = refs.pop(0)
    if has_norm:
        h_ref = refs.pop(0)
    if has_rows:
        hrow_ref = refs.pop(0)
    i = pl.program_id(0)

    def body(prompt):
        mod = (lambda p, s: p[0:1, :]) if prompt else (lambda p, s: s[...])
        x = x_ref[...]
        if has_resid:
            x = x + mod(gp_ref, gs_ref) * y_ref[...]
            xo_ref[...] = x
        if has_norm:
            hn = _rms(x, nw_ref[...])
            h = hn * (1.0 + mod(scp_ref, scs_ref)) + mod(shp_ref, shs_ref)
            h_ref[...] = h.astype(h_ref.dtype)
            if has_rows:
                for c in range(hrow_ref.shape[1]):
                    hrow_ref[:, c, :] = h[:, c * LANES:(c + 1) * LANES]

    pl.when(i < n_prompt_tiles)(lambda: body(True))
    pl.when(i >= n_prompt_tiles)(lambda: body(False))


def _resid_norm(x, y, mod_p, mod_s, norm_w, l, *, gate_k, scale_k, shift_k, seq, n_prompt, rows_out=False):
    R, D = x.shape
    TE = mod_s.shape[1]
    tpb = seq // TE
    npt = n_prompt * tpb
    has_resid = y is not None
    has_norm = norm_w is not None
    row = pl.BlockSpec((TE, D), lambda i: (i, 0))

    def mp(lk):
        return pl.BlockSpec((None, SUBLANES, D), lambda i: (lk[0], jnp.minimum(i // tpb, n_prompt - 1), lk[1]))

    def ms(lk):
        return pl.BlockSpec((None, TE, D), lambda i: (lk[0], 0, lk[1]))

    args, specs, outs, out_specs = [x], [row], [], []
    if has_resid:
        args += [y, mod_p, mod_s]
        specs += [row, mp(gate_k), ms(gate_k)]
        outs.append(jax.ShapeDtypeStruct((R, D), F32))
        out_specs.append(row)
    if has_norm:
        args += [mod_p, mod_s, mod_p, mod_s, norm_w.reshape(norm_w.shape[0], 1, D)]
        specs += [mp(scale_k), ms(scale_k), mp(shift_k), ms(shift_k),
                  pl.BlockSpec((None, 1, D), lambda i: (l, 0, 0))]
        outs.append(jax.ShapeDtypeStruct((R, D), BF16))
        out_specs.append(row)
    if rows_out:
        outs.append(jax.ShapeDtypeStruct((R, D // LANES, LANES), F32))
        out_specs.append(pl.BlockSpec((TE, D // LANES, LANES), lambda i: (i, 0, 0)))
    res = pl.pallas_call(
        functools.partial(_resid_norm_kernel, n_prompt_tiles=npt, has_resid=has_resid, has_norm=has_norm,
                          has_rows=rows_out),
        grid=(R // TE,),
        in_specs=specs, out_specs=out_specs, out_shape=outs,
        compiler_params=_params("parallel"),
        name="resid_norm",
    )(*args)
    res = list(res)
    x_new = res.pop(0) if has_resid else x
    h = res.pop(0) if has_norm else None
    if rows_out:
        return x_new, h, res.pop(0)
    return x_new, h


def _mm_kernel(a_ref, w_ref, o_ref, acc_ref, *, nk):
    part = jnp.dot(a_ref[...], w_ref[...].astype(BF16), preferred_element_type=F32)
    if nk == 1:
        o_ref[...] = part.astype(o_ref.dtype)
        return
    k = pl.program_id(2)

    @pl.when(k == 0)
    def _():
        acc_ref[...] = part

    @pl.when(k > 0)
    def _():
        acc_ref[...] += part

    @pl.when(k == nk - 1)
    def _():
        o_ref[...] = acc_ref[...].astype(o_ref.dtype)


def _matmul(a, w, l, *, out_dtype, tm, tn, tk):
    R, K = a.shape
    N = w.shape[-1]
    nk = K // tk
    acc_shape = (tm, tn) if nk > 1 else (SUBLANES, LANES)
    return pl.pallas_call(
        functools.partial(_mm_kernel, nk=nk),
        grid=(R // tm, N // tn, nk),
        in_specs=[pl.BlockSpec((tm, tk), lambda i, j, k: (i, k)),
                  pl.BlockSpec((None, tk, tn), lambda i, j, k: (l, k, j))],
        out_specs=pl.BlockSpec((tm, tn), lambda i, j, k: (i, j)),
        out_shape=jax.ShapeDtypeStruct((R, N), out_dtype),
        scratch_shapes=[pltpu.VMEM(acc_shape, F32)],
        compiler_params=_params("parallel", "parallel", "arbitrary"),
        name="matmul",
    )(a, w)


def _softmax_sink_pv(s, sink, v):
    m = jnp.maximum(jnp.max(s, axis=-1, keepdims=True), sink)
    e = jnp.exp(s - m)
    den = jnp.sum(e, axis=-1, keepdims=True) + jnp.exp(sink - m)
    return jnp.dot((e / den).astype(BF16), v, preferred_element_type=F32)


def _attn_prompt_kernel(sink_ref, q_ref, kp_ref, kc_ref, vp_ref, vc_ref, qw_ref, kw_ref,
                        o_ref, kwin_ref, vwin_ref, *, l, n_kv, q_per_kv, hd, window, nblk):
    i = pl.program_id(1)
    blk = q_ref.shape[0]
    scale = hd ** -0.5
    rows_q = q_per_kv * blk
    r = lax.broadcasted_iota(jnp.int32, (rows_q, 2 * blk), 0) % blk
    c = lax.broadcasted_iota(jnp.int32, (rows_q, 2 * blk), 1)
    dist = r + blk - c
    ok = (dist >= 0) & (dist <= window) & ((c >= blk) | (i > 0))
    hrow = lax.broadcasted_iota(jnp.int32, (rows_q, 1), 0) // blk
    qw = qw_ref[...]
    kw = kw_ref[...]
    for g in range(n_kv):
        sl = slice(g * hd, (g + 1) * hd)
        kcn = _rms(kc_ref[:, sl], kw)
        kcat = jnp.concatenate([_rms(kp_ref[:, sl], kw), kcn], axis=0).astype(BF16)
        vcat = jnp.concatenate([vp_ref[:, sl], vc_ref[:, sl]], axis=0).astype(BF16)
        heads = [g * q_per_kv + h for h in range(q_per_kv)]
        qs = jnp.concatenate([_rms(q_ref[:, hh * hd:(hh + 1) * hd], qw) for hh in heads], axis=0).astype(BF16)
        s = lax.dot_general(qs, kcat, (((1,), (1,)), ((), ())), preferred_element_type=F32) * scale
        s = jnp.where(ok, s, -1e30)
        sink = jnp.zeros((rows_q, 1), F32)
        for h, hh in enumerate(heads):
            sink = jnp.where(hrow == h, sink_ref[l, hh], sink)
        o = _softmax_sink_pv(s, sink, vcat)
        for h, hh in enumerate(heads):
            o_ref[:, hh * hd:(hh + 1) * hd] = o[h * blk:(h + 1) * blk, :].astype(o_ref.dtype)

        @pl.when(i == nblk - 1)
        def _():
            kwin_ref[:, sl] = kcn
            vwin_ref[:, sl] = vc_ref[:, sl]


def _attn_prompt(z, sinks, q_norm_w, k_norm_w, l, *, R, n_prompt, seq, n_q, n_kv, hd, window):
    blk = window
    nblk = seq // blk
    wa = n_q * hd
    kvw = n_kv * hd
    kcol = wa // kvw
    vcol = kcol + 1

    def cur(col):
        return pl.BlockSpec((blk, kvw), lambda b, i: (b * nblk + i, col))

    def prev(col):
        return pl.BlockSpec((blk, kvw), lambda b, i: (b * nblk + jnp.maximum(i - 1, 0), col))

    L = q_norm_w.shape[0]
    nw = pl.BlockSpec((None, 1, hd), lambda b, i: (l, 0, 0))
    return pl.pallas_call(
        functools.partial(_attn_prompt_kernel, l=l, n_kv=n_kv, q_per_kv=n_q // n_kv, hd=hd,
                          window=window, nblk=nblk),
        grid=(n_prompt, nblk),
        in_specs=[pl.BlockSpec(memory_space=pltpu.SMEM),
                  pl.BlockSpec((blk, wa), lambda b, i: (b * nblk + i, 0)),
                  prev(kcol), cur(kcol), prev(vcol), cur(vcol), nw, nw],
        out_specs=[pl.BlockSpec((blk, wa), lambda b, i: (b * nblk + i, 0)),
                   pl.BlockSpec((None, blk, kvw), lambda b, i: (b, 0, 0)),
                   pl.BlockSpec((None, blk, kvw), lambda b, i: (b, 0, 0))],
        out_shape=[jax.ShapeDtypeStruct((R, wa), BF16),
                   jax.ShapeDtypeStruct((n_prompt, blk, kvw), F32),
                   jax.ShapeDtypeStruct((n_prompt, blk, kvw), F32)],
        compiler_params=_params("parallel", "arbitrary"),
        name="attn_prompt",
    )(sinks, z, z, z, z, z, q_norm_w.reshape(L, 1, hd), k_norm_w.reshape(L, 1, hd))


def _attn_sample_kernel(sink_ref, o_in_ref, q_ref, kn_ref, vn_ref, ck_ref, cv_ref, qw_ref, kw_ref,
                        o_ref, kwin_ref, vwin_ref, keys_s, vals_s, ostage,
                        *, l, nb, T, n_kv, q_per_kv, hd, window):
    del o_in_ref
    W = ck_ref.shape[1]
    scale = hd ** -0.5
    rows_q = q_per_kv * T
    r = lax.broadcasted_iota(jnp.int32, (rows_q, 2 * W), 0) % T
    c = lax.broadcasted_iota(jnp.int32, (rows_q, 2 * W), 1)
    q_pos = PAST_LEN + r
    k_pos = jnp.where(c < W, PAST_LEN - W + c, PAST_LEN + c - W)
    dist = q_pos - k_pos
    ok = (dist >= 0) & (dist <= window) & (k_pos >= 0) & (c < W + T)
    hrow = lax.broadcasted_iota(jnp.int32, (rows_q, 1), 0) // T
    qw = qw_ref[...]
    kw = kw_ref[...]
    keys_s[...] = jnp.zeros_like(keys_s)
    vals_s[...] = jnp.zeros_like(vals_s)

    def step(b, carry):
        rows = pl.ds(pl.multiple_of(b * T, T), T)
        for g in range(n_kv):
            sl = slice(g * hd, (g + 1) * hd)
            ck = ck_ref[b, :, sl]
            cv = cv_ref[b, :, sl]
            knn = _rms(kn_ref[rows, sl], kw)
            vnn = vn_ref[rows, sl]
            keys_s[0:W, :] = ck
            keys_s[W:W + T, :] = knn
            vals_s[0:W, :] = cv
            vals_s[W:W + T, :] = vnn
            qs = jnp.concatenate(
                [_rms(q_ref[rows, (g * q_per_kv + h) * hd:(g * q_per_kv + h + 1) * hd], qw)
                 for h in range(q_per_kv)], axis=0).astype(BF16)
            s = lax.dot_general(qs, keys_s[...].astype(BF16), (((1,), (1,)), ((), ())),
                                preferred_element_type=F32) * scale
            s = jnp.where(ok, s, -1e30)
            sink = jnp.zeros((rows_q, 1), F32)
            for h in range(q_per_kv):
                sink = jnp.where(hrow == h, sink_ref[l, g * q_per_kv + h], sink)
            o = _softmax_sink_pv(s, sink, vals_s[...].astype(BF16))
            for h in range(q_per_kv):
                head = g * q_per_kv + h
                ostage[rows, head * hd:(head + 1) * hd] = o[h * T:(h + 1) * T, :]
            kwin_ref[b, 0:W - T, sl] = ck[T:, :]
            kwin_ref[b, W - T:W, sl] = knn
            vwin_ref[b, 0:W - T, sl] = cv[T:, :]
            vwin_ref[b, W - T:W, sl] = vnn
        return carry

    lax.fori_loop(0, nb, step, 0)
    o_ref[...] = ostage[...].astype(o_ref.dtype)


def _attn_sample(attn_o, z, cache_k, cache_v, sinks, q_norm_w, k_norm_w, l, *, n_sample, T, n_q, n_kv, hd,
                 window):
    R, wa = attn_o.shape
    kvw = n_kv * hd
    rows = n_sample * T
    blk_i = (R - rows) // rows
    kcol = wa // kvw
    L, nb, W = cache_k.shape[:3]
    ck = cache_k.reshape(L, nb, W, kvw)
    cv = cache_v.reshape(L, nb, W, kvw)
    nw = pl.BlockSpec((None, 1, hd), lambda i: (l, 0, 0))
    cache = pl.BlockSpec((None, nb, W, kvw), lambda i: (l, 0, 0, 0))
    win = pl.BlockSpec((nb, W, kvw), lambda i: (0, 0, 0))
    return pl.pallas_call(
        functools.partial(_attn_sample_kernel, l=l, nb=nb, T=T, n_kv=n_kv, q_per_kv=n_q // n_kv, hd=hd,
                          window=window),
        grid=(1,),
        in_specs=[pl.BlockSpec(memory_space=pltpu.SMEM),
                  pl.BlockSpec(memory_space=pl.ANY),
                  pl.BlockSpec((rows, wa), lambda i: (blk_i, 0)),
                  pl.BlockSpec((rows, kvw), lambda i: (blk_i, kcol)),
                  pl.BlockSpec((rows, kvw), lambda i: (blk_i, kcol + 1)),
                  cache, cache, nw, nw],
        out_specs=[pl.BlockSpec((rows, wa), lambda i: (blk_i, 0)), win, win],
        out_shape=[jax.ShapeDtypeStruct((R, wa), BF16),
                   jax.ShapeDtypeStruct((nb, W, kvw), F32),
                   jax.ShapeDtypeStruct((nb, W, kvw), F32)],
        scratch_shapes=[pltpu.VMEM((2 * W, hd), F32), pltpu.VMEM((2 * W, hd), F32),
                        pltpu.VMEM((rows, wa), F32)],
        input_output_aliases={1: 0},
        compiler_params=_params("arbitrary"),
        name="attn_sample",
    )(sinks, attn_o, z, z, z, ck, cv, q_norm_w.reshape(-1, 1, hd), k_norm_w.reshape(-1, 1, hd))


def _lru_gates(xc, wa_ref, ba_ref, wx_ref, bx_ref, lam_ref, a_s, u_s, *, nblk, lb):
    xcb = xc.astype(BF16)
    for n in range(nblk):
        sl = slice(n * lb, (n + 1) * lb)
        ra = jnp.dot(xcb[:, sl], wa_ref[n].astype(BF16), preferred_element_type=F32) + ba_ref[:, sl]
        rx = jnp.dot(xcb[:, sl], wx_ref[n].astype(BF16), preferred_element_type=F32) + bx_ref[:, sl]
        log_a = -LRU_C * jax.nn.sigmoid(ra) * jax.nn.softplus(-lam_ref[:, sl])
        a_s[:, sl] = jnp.exp(log_a)
        th = jnp.tanh(log_a)
        u_s[:, sl] = jnp.sqrt(-2.0 * th / (1.0 - th)) * jax.nn.sigmoid(rx) * xc[:, sl]

def _tile_scan(a, u, T):
    t = lax.broadcasted_iota(jnp.int32, (a.shape[0], 1), 0) % T
    d = 1
    while d < T:
        keep = t >= d
        u = jnp.where(keep, a * pltpu.roll(u, d, axis=0) + u, u)
        a = jnp.where(keep, a * pltpu.roll(a, d, axis=0), a)
        d *= 2
    return a, u


def _lru_prompt_kernel(x_ref, g_ref, cw_ref, cb_ref, wa_ref, ba_ref, wx_ref, bx_ref, lam_ref,
                       o_ref, hT_ref, cnew_ref, xp_s, a_s, u_s, h_s, *, Tb, ntb, cw, nblk, lb):
    tb = pl.program_id(2)
    P = SUBLANES

    @pl.when(tb == 0)
    def _():
        xp_s[0:P, :] = jnp.zeros((P, xp_s.shape[1]), F32)
        h_s[...] = jnp.zeros_like(h_s)

    xp_s[P:P + Tb, :] = x_ref[...]
    xc = cb_ref[...]
    for j in range(cw):
        off = P - (cw - 1) + j
        xc = xc + xp_s[off:off + Tb, :] * cw_ref[j:j + 1, :]
    _lru_gates(xc, wa_ref, ba_ref, wx_ref, bx_ref, lam_ref, a_s, u_s, nblk=nblk, lb=lb)

    a, u = _tile_scan(a_s[...], u_s[...], P)
    a_s[...] = a
    u_s[...] = u

    def step(k, h):
        rr = pl.ds(pl.multiple_of(k * P, P), P)
        hs = u_s[rr, :] + a_s[rr, :] * h
        u_s[rr, :] = hs
        return hs[P - 1:P, :]

    h = lax.fori_loop(0, Tb // P, step, h_s[...])
    h_s[...] = h
    o_ref[...] = (jax.nn.gelu(g_ref[...]) * u_s[...]).astype(o_ref.dtype)
    tail = xp_s[Tb:Tb + P, :]
    xp_s[0:P, :] = tail

    @pl.when(tb == ntb - 1)
    def _():
        hT_ref[...] = h
        cnew_ref[...] = tail


def _lru_sample_kernel(o_in_ref, x_ref, g_ref, h0_ref, buf_ref, cw_ref, cb_ref, wa_ref, ba_ref, wx_ref, bx_ref,
                       lam_ref, o_ref, hs_ref, a_s, u_s, *, T, cw, nblk, lb):
    del o_in_ref
    rows = x_ref.shape[0]
    t = lax.broadcasted_iota(jnp.int32, (rows, 1), 0) % T
    x = x_ref[...]
    buf = buf_ref[...]
    xc = cb_ref[...] + x * cw_ref[cw - 1:cw, :]
    for s in range(1, cw):
        xs = jnp.where(t >= s, pltpu.roll(x, s, axis=0), pltpu.roll(buf, rows - (T - s), axis=0))
        xc = xc + xs * cw_ref[cw - 1 - s:cw - s, :]
    _lru_gates(xc, wa_ref, ba_ref, wx_ref, bx_ref, lam_ref, a_s, u_s, nblk=nblk, lb=lb)
    a, u = _tile_scan(a_s[...], u_s[...], T)
    hs = u + a * h0_ref[...]
    hs_ref[...] = hs
    o_ref[...] = (jax.nn.gelu(g_ref[...]) * hs).astype(o_ref.dtype)


def _lru_weight_args(W, l, wb, index):
    ch = wb // 2
    nlb, lb = W['lru_w_a'].shape[1:3]
    nblk = nlb // 2
    cw = W['conv_w'].shape[1]

    def vec(name):
        return W[name].reshape(W[name].shape[0], 1, wb), pl.BlockSpec((None, 1, ch), index(lambda hf: (l, 0, hf)))

    def blkw(name):
        return W[name], pl.BlockSpec((None, nblk, lb, lb), index(lambda hf: (l, hf, 0, 0)))

    pairs = [(W['conv_w'], pl.BlockSpec((None, cw, ch), index(lambda hf: (l, 0, hf)))),
             vec('conv_b'), blkw('lru_w_a'), vec('lru_b_a'), blkw('lru_w_x'), vec('lru_b_x'), vec('lru_lambda')]
    return [p[0] for p in pairs], [p[1] for p in pairs], dict(cw=cw, nblk=nblk, lb=lb)


def _lru_prompt(z, W, l, *, R, nseq, T, Tb, col_x, col_g, wb):
    ch = wb // 2
    ntb = T // Tb
    wargs, wspecs, kw = _lru_weight_args(W, l, wb, lambda f: (lambda s, hf, t: f(hf)))

    def rowblk(col):
        return pl.BlockSpec((Tb, ch), lambda s, hf, t: (s * ntb + t, col + hf))

    return pl.pallas_call(
        functools.partial(_lru_prompt_kernel, Tb=Tb, ntb=ntb, **kw),
        grid=(nseq, 2, ntb),
        in_specs=[rowblk(col_x), rowblk(col_g)] + wspecs,
        out_specs=[pl.BlockSpec((Tb, ch), lambda s, hf, t: (s * ntb + t, hf)),
                   pl.BlockSpec((None, 1, ch), lambda s, hf, t: (s, 0, hf)),
                   pl.BlockSpec((None, SUBLANES, ch), lambda s, hf, t: (s, 0, hf))],
        out_shape=[jax.ShapeDtypeStruct((R, wb), BF16),
                   jax.ShapeDtypeStruct((nseq, 1, wb), F32),
                   jax.ShapeDtypeStruct((nseq, SUBLANES, wb), F32)],
        scratch_shapes=[pltpu.VMEM((SUBLANES + Tb, ch), F32), pltpu.VMEM((Tb, ch), F32),
                        pltpu.VMEM((Tb, ch), F32), pltpu.VMEM((1, ch), F32)],
        compiler_params=_params("parallel", "parallel", "arbitrary"),
        name="lru_prompt",
    )(z, z, *wargs)


def _lru_sample(lru_o, z, h0_rep, buf_rows, W, l, *, row0, T, col_x, col_g, wb):
    ch = wb // 2
    rows = h0_rep.shape[0]
    rb = row0 // rows
    wargs, wspecs, kw = _lru_weight_args(W, l, wb, lambda f: (lambda hf: f(hf)))
    st = pl.BlockSpec((rows, ch), lambda hf: (0, hf))
    return pl.pallas_call(
        functools.partial(_lru_sample_kernel, T=T, **kw),
        grid=(2,),
        in_specs=[pl.BlockSpec(memory_space=pl.ANY),
                  pl.BlockSpec((rows, ch), lambda hf: (rb, col_x + hf)),
                  pl.BlockSpec((rows, ch), lambda hf: (rb, col_g + hf)), st, st] + wspecs,
        out_specs=[pl.BlockSpec((rows, ch), lambda hf: (rb, hf)), st],
        out_shape=[jax.ShapeDtypeStruct(lru_o.shape, BF16), jax.ShapeDtypeStruct((rows, wb), F32)],
        scratch_shapes=[pltpu.VMEM((rows, ch), F32), pltpu.VMEM((rows, ch), F32)],
        input_output_aliases={0: 0},
        compiler_params=_params("parallel"),
        name="lru_sample",
    )(lru_o, z, z, h0_rep, buf_rows, *wargs)


def _chunk_kernel(*refs, nb, Tc, ngroups, gw, cps, aliased):
    refs = list(refs)
    if aliased:
        refs.pop(0)
    ulo_ref, uhi_ref, vlo_ref, vhi_ref, nw_ref, ws_ref, bst_ref, o_ref, vn_ref = refs
    rows = nb * Tc
    half = vlo_ref.shape[1]
    wc = 2 * half
    v_lo = jax.nn.gelu(vlo_ref[...])
    v_hi = jax.nn.gelu(vhi_ref[...])
    ms = (jnp.sum(v_lo * v_lo, axis=-1, keepdims=True) + jnp.sum(v_hi * v_hi, axis=-1, keepdims=True)) / wc
    inv = lax.rsqrt(ms + EPS)
    vn_halves = (v_lo * inv * nw_ref[:, 0:half], v_hi * inv * nw_ref[:, half:wc])
    u_halves = (ulo_ref, uhi_ref)

    r = lax.broadcasted_iota(jnp.int32, (rows, rows), 0)
    c = lax.broadcasted_iota(jnp.int32, (rows, rows), 1)
    mask = (r // Tc == c // Tc) & (c % Tc <= r % Tc)
    if nb > 1:
        sel = (lax.broadcasted_iota(jnp.int32, (ws_ref.shape[2], rows), 0)
               == lax.broadcasted_iota(jnp.int32, (ws_ref.shape[2], rows), 1) % Tc).astype(BF16)
        bias_rows = jnp.broadcast_to(bst_ref[0:Tc, :][None], (nb, Tc, bst_ref.shape[1])).reshape(rows, -1)
    else:
        bias_rows = bst_ref[0:Tc, :]
    gph = half // gw
    for g in range(ngroups):
        hf, gi = divmod(g, gph)
        sl = slice(gi * gw, (gi + 1) * gw)
        if nb > 1:
            t1 = jnp.broadcast_to(ws_ref[g, 0:Tc, :][None], (nb, Tc, ws_ref.shape[2])).reshape(rows, -1)
            wfull = jnp.dot(t1.astype(BF16), sel, preferred_element_type=F32)
        else:
            wfull = ws_ref[g, 0:Tc, 0:Tc]
        wm = jnp.where(mask, wfull, 0.0).astype(BF16)
        mixed = jnp.dot(wm, vn_halves[hf][:, sl].astype(BF16), preferred_element_type=F32)
        mixed = mixed + bias_rows[:, g:g + 1]
        o_ref[:, g * gw:(g + 1) * gw] = (jax.nn.gelu(u_halves[hf][:, sl]) * mixed).astype(o_ref.dtype)

    if cps == 1:
        vn_ref[:, 0:half] = vn_halves[0]
        vn_ref[:, half:wc] = vn_halves[1]
    else:
        @pl.when(pl.program_id(0) % cps == cps - 1)
        def _():
            vn_ref[:, 0:half] = vn_halves[0]
            vn_ref[:, half:wc] = vn_halves[1]


def _chunk(chunk_o, z, W, l, *, R, row0, nsteps, nb, Tc, cps, col_u, col_v, wc):
    half = wc // 2
    ngroups, chunk = W['chunk_w_s'].shape[1:3]
    gw = wc // ngroups
    rows = nb * Tc
    rb0 = row0 // rows
    aliased = chunk_o is not None

    def rowblk(col):
        return pl.BlockSpec((rows, half), lambda i: (rb0 + i, col))

    args = [z, z, z, z, W['chunk_v_norm_w'].reshape(-1, 1, wc), W['chunk_w_s'],
            jnp.swapaxes(W['chunk_b_s'], 1, 2)]
    specs = [rowblk(col_u), rowblk(col_u + 1), rowblk(col_v), rowblk(col_v + 1),
             pl.BlockSpec((None, 1, wc), lambda i: (l, 0, 0)),
             pl.BlockSpec((None, ngroups, chunk, chunk), lambda i: (l, 0, 0, 0)),
             pl.BlockSpec((None, chunk, ngroups), lambda i: (l, 0, 0))]
    io_alias = {}
    if aliased:
        args.insert(0, chunk_o)
        specs.insert(0, pl.BlockSpec(memory_space=pl.ANY))
        io_alias = {0: 0}
    return pl.pallas_call(
        functools.partial(_chunk_kernel, nb=nb, Tc=Tc, ngroups=ngroups, gw=gw, cps=cps, aliased=aliased),
        grid=(nsteps,),
        in_specs=specs,
        out_specs=[pl.BlockSpec((rows, wc), lambda i: (rb0 + i, 0)),
                   pl.BlockSpec((None, rows, wc), lambda i: (i // cps, 0, 0))],
        out_shape=[jax.ShapeDtypeStruct((R, wc), BF16),
                   jax.ShapeDtypeStruct((nsteps // cps, rows, wc), F32)],
        input_output_aliases=io_alias,
        compiler_params=_params("arbitrary"),
        name="chunk_sample" if aliased else "chunk_prompt",
    )(*args)


def _merge_kernel(a_ref, b_ref, c_ref, wa_ref, wb_ref, wc_ref, ga_ref, gb_ref, gc_ref, o_ref):
    def branch(x_ref, w_ref, g_ref):
        y = jnp.dot(x_ref[...], w_ref[...].astype(BF16), preferred_element_type=F32)
        return jax.nn.sigmoid(g_ref[...]) * y

    m = branch(a_ref, wa_ref, ga_ref) + branch(b_ref, wb_ref, gb_ref) + branch(c_ref, wc_ref, gc_ref)
    o_ref[...] = m.astype(o_ref.dtype)


def _merge(attn_o, lru_o, chunk_o, z, W, l, *, gate_col, tm, tn):
    R = attn_o.shape[0]
    D = W['w_branch_a'].shape[-1]
    g0 = gate_col // tn
    gstep = D // tn

    def xin(a):
        return pl.BlockSpec((tm, a.shape[1]), lambda i, j: (i, 0))

    def win(w):
        return pl.BlockSpec((None, w.shape[1], tn), lambda i, j: (l, 0, j))

    def gin(k):
        return pl.BlockSpec((tm, tn), lambda i, j: (i, g0 + k * gstep + j))

    return pl.pallas_call(
        _merge_kernel,
        grid=(R // tm, D // tn),
        in_specs=[xin(attn_o), xin(lru_o), xin(chunk_o),
                  win(W['w_branch_a']), win(W['w_branch_b']), win(W['w_branch_c']),
                  gin(0), gin(1), gin(2)],
        out_specs=pl.BlockSpec((tm, tn), lambda i, j: (i, j)),
        out_shape=jax.ShapeDtypeStruct((R, D), BF16),
        compiler_params=_params("parallel", "parallel"),
        name="merge",
    )(attn_o, lru_o, chunk_o, W['w_branch_a'], W['w_branch_b'], W['w_branch_c'], z, z, z)


def _glu_kernel(*refs, gated):
    if gated:
        h_ref, wg_ref, wu_ref, gate_ref, o_ref = refs
    else:
        h_ref, wg_ref, wu_ref, o_ref = refs
    h = h_ref[...]
    g = jnp.dot(h, wg_ref[...].astype(BF16), preferred_element_type=F32)
    u = jnp.dot(h, wu_ref[...].astype(BF16), preferred_element_type=F32)
    a = jax.nn.silu(g) * u
    if gated:
        e = pl.program_id(1)
        gate = gate_ref[...]
        lane = lax.broadcasted_iota(jnp.int32, gate.shape, 1)
        a = a * jnp.sum(jnp.where(lane == e, gate, 0.0), axis=-1, keepdims=True)
    o_ref[...] = a.astype(o_ref.dtype)


def _glu(h, w_gate, w_up, l0, n_exp, gate, *, tm, tn):
    R, D = h.shape
    F = w_gate.shape[-1]
    nf = F // tn
    gated = gate is not None
    wspec = pl.BlockSpec((None, D, tn), lambda i, e, j: (l0 + e, 0, j))
    args = [h, w_gate, w_up]
    specs = [pl.BlockSpec((tm, D), lambda i, e, j: (i, 0)), wspec, wspec]
    if gated:
        args.append(gate)
        specs.append(pl.BlockSpec((tm, gate.shape[1]), lambda i, e, j: (i, 0)))
    return pl.pallas_call(
        functools.partial(_glu_kernel, gated=gated),
        grid=(R // tm, n_exp, nf),
        in_specs=specs,
        out_specs=pl.BlockSpec((tm, tn), lambda i, e, j: (i, e * nf + j)),
        out_shape=jax.ShapeDtypeStruct((R, n_exp * F), BF16),
        compiler_params=_params("parallel", "parallel", "parallel"),
        name="glu",
    )(*args)


META_E1, META_E2, META_W1, META_W2, META_R1, META_R2 = range(6)


def _router_kernel(h_ref, w_ref, meta_ref, cnt_ref, carry, *, n_exp):
    i = pl.program_id(0)

    @pl.when(i == 0)
    def _():
        carry[...] = jnp.zeros_like(carry)

    logits = jnp.dot(h_ref[...], w_ref[...].astype(BF16), preferred_element_type=F32)
    tm = logits.shape[0]
    lane = lax.broadcasted_iota(jnp.int32, logits.shape, 1).astype(F32)
    big = float(logits.shape[1])
    l0 = jnp.where(lane < n_exp, logits, -jnp.inf)
    m1 = jnp.max(l0, axis=-1, keepdims=True)
    i1 = jnp.min(jnp.where(l0 == m1, lane, big), axis=-1, keepdims=True)
    l1 = jnp.where(lane == i1, -jnp.inf, l0)
    m2 = jnp.max(l1, axis=-1, keepdims=True)
    i2 = jnp.min(jnp.where(l1 == m2, lane, big), axis=-1, keepdims=True)
    e2 = jnp.exp(m2 - m1)
    den = 1.0 + e2
    hit = ((lane == i1) | (lane == i2)).astype(F32)
    earlier = (lax.broadcasted_iota(jnp.int32, (tm, tm), 0) > lax.broadcasted_iota(jnp.int32, (tm, tm), 1))
    rank = jnp.dot(earlier.astype(BF16), hit.astype(BF16), preferred_element_type=F32) + carry[...]
    r1 = jnp.sum(jnp.where(lane == i1, rank, 0.0), axis=-1, keepdims=True)
    r2 = jnp.sum(jnp.where(lane == i2, rank, 0.0), axis=-1, keepdims=True)
    meta = jnp.zeros_like(logits)
    for k, v in ((META_E1, i1), (META_E2, i2), (META_W1, 1.0 / den), (META_W2, e2 / den), (META_R1, r1),
                 (META_R2, r2)):
        meta = jnp.where(lane == k, v, meta)
    meta_ref[...] = meta
    carry[...] += jnp.sum(hit, axis=0, keepdims=True)
    cnt_ref[...] = carry[...]


def _router(h, w_router_padded, j, *, n_exp, tm):
    R, D = h.shape
    NP = w_router_padded.shape[-1]
    return pl.pallas_call(
        functools.partial(_router_kernel, n_exp=n_exp),
        grid=(R // tm,),
        in_specs=[pl.BlockSpec((tm, D), lambda i: (i, 0)),
                  pl.BlockSpec((None, D, NP), lambda i: (j, 0, 0))],
        out_specs=[pl.BlockSpec((tm, NP), lambda i: (i, 0)), pl.BlockSpec((1, NP), lambda i: (0, 0))],
        out_shape=[jax.ShapeDtypeStruct((R, NP), F32), jax.ShapeDtypeStruct((1, NP), F32)],
        scratch_shapes=[pltpu.VMEM((1, NP), F32)],
        compiler_params=_params("arbitrary"),
        name="router",
    )(h, w_router_padded)


def _row_copies_wait(src_like, dst_like, sem, n):
    for _ in range(n):
        pltpu.make_async_copy(src_like, dst_like, sem).wait()


def _dispatch_kernel(p1_ref, p2_ref, h_ref, xs_ref, sem):
    tm = h_ref.shape[0]
    i0 = pl.program_id(0) * tm

    def body(i, c):
        pltpu.make_async_copy(h_ref.at[i], xs_ref.at[p1_ref[i0 + i]], sem).start()
        pltpu.make_async_copy(h_ref.at[i], xs_ref.at[p2_ref[i0 + i]], sem).start()
        return c

    lax.fori_loop(0, tm, body, 0)
    _row_copies_wait(h_ref, xs_ref.at[pl.ds(0, tm)], sem, TOP_K)


def _dispatch(h_rows, p1, p2, *, n_slots, tm):
    R, C, _ = h_rows.shape
    return pl.pallas_call(
        _dispatch_kernel,
        grid_spec=pltpu.PrefetchScalarGridSpec(
            num_scalar_prefetch=2, grid=(R // tm,),
            in_specs=[pl.BlockSpec((tm, C, LANES), lambda i, p1, p2: (i, 0, 0))],
            out_specs=pl.BlockSpec(memory_space=pl.ANY),
            scratch_shapes=[pltpu.SemaphoreType.DMA]),
        out_shape=jax.ShapeDtypeStruct((n_slots, C, LANES), F32),
        compiler_params=_params("arbitrary"),
        name="moe_dispatch",
    )(p1, p2, h_rows)


def _moe_ffn_kernel(te_ref, nv_ref, nu_ref, xs_ref, wg_ref, wu_ref, wd_ref, y_ref, xb_s, acc_s, *, sub, nf):
    t = pl.program_id(0)
    j = pl.program_id(1)
    nv = nv_ref[t]
    tile, nchunk = xs_ref.shape[0], xs_ref.shape[1]

    @pl.when(t < nu_ref[0])
    def _():
        wg = wg_ref[...].astype(BF16)
        wu = wu_ref[...].astype(BF16)
        wd = wd_ref[...].astype(BF16)
        for m in range(sub, tile + 1, sub):
            rows = slice(0, m)

            @pl.when((nv > m - sub) & (nv <= m))
            def _():
                @pl.when(j == 0)
                def _():
                    live = lax.broadcasted_iota(jnp.int32, (m, 1), 0) < nv
                    for c in range(nchunk):
                        xb_s[rows, c * LANES:(c + 1) * LANES] = jnp.where(live, xs_ref[rows, c, :], 0.0).astype(BF16)

                x = xb_s[rows, :]
                g = jnp.dot(x, wg, preferred_element_type=F32)
                u = jnp.dot(x, wu, preferred_element_type=F32)
                part = jnp.dot((jax.nn.silu(g) * u).astype(BF16), wd, preferred_element_type=F32)

                @pl.when(j == 0)
                def _():
                    acc_s[rows, :] = part

                @pl.when(j > 0)
                def _():
                    acc_s[rows, :] += part

                @pl.when(j == nf - 1)
                def _():
                    for c in range(nchunk):
                        y_ref[rows, c, :] = acc_s[rows, c * LANES:(c + 1) * LANES]
                    if m < tile:
                        y_ref[m:tile, :, :] = jnp.zeros((tile - m, nchunk, LANES), F32)


def _moe_ffn(xs, w_gate, w_up, w_down, l0, tile_expert, tile_valid, n_used, *, tile, sub, tn):
    n_slots, C, _ = xs.shape
    D = C * LANES
    F = w_gate.shape[-1]
    nf = F // tn
    nt = n_slots // tile

    def live_t(t, nu):
        return jnp.minimum(t, nu[0] - 1)

    def live_j(t, j, nu):
        return jnp.where(t < nu[0], j, nf - 1)

    return pl.pallas_call(
        functools.partial(_moe_ffn_kernel, sub=sub, nf=nf),
        grid_spec=pltpu.PrefetchScalarGridSpec(
            num_scalar_prefetch=3, grid=(nt, nf),
            in_specs=[pl.BlockSpec((tile, C, LANES), lambda t, j, te, nv, nu: (live_t(t, nu), 0, 0)),
                      pl.BlockSpec((None, D, tn), lambda t, j, te, nv, nu: (l0 + te[live_t(t, nu)], 0, live_j(t, j, nu))),
                      pl.BlockSpec((None, D, tn), lambda t, j, te, nv, nu: (l0 + te[live_t(t, nu)], 0, live_j(t, j, nu))),
                      pl.BlockSpec((None, tn, D), lambda t, j, te, nv, nu: (l0 + te[live_t(t, nu)], live_j(t, j, nu), 0))],
            out_specs=pl.BlockSpec((tile, C, LANES), lambda t, j, te, nv, nu: (live_t(t, nu), 0, 0)),
            scratch_shapes=[pltpu.VMEM((tile, D), BF16), pltpu.VMEM((tile, D), F32)]),
        out_shape=jax.ShapeDtypeStruct((n_slots, C, LANES), F32),
        compiler_params=_params("arbitrary", "arbitrary"),
        name="moe_ffn",
    )(tile_expert, tile_valid, n_used, xs, w_gate, w_up, w_down)


def _combine_kernel(p1_ref, p2_ref, meta_ref, y_ref, f_ref, a_s, b_s, sem):
    tm, nchunk = a_s.shape[0], a_s.shape[1]
    i0 = pl.program_id(0) * tm

    def body(i, c):
        pltpu.make_async_copy(y_ref.at[p1_ref[i0 + i]], a_s.at[i], sem).start()
        pltpu.make_async_copy(y_ref.at[p2_ref[i0 + i]], b_s.at[i], sem).start()
        return c

    lax.fori_loop(0, tm, body, 0)
    _row_copies_wait(y_ref.at[pl.ds(0, tm)], a_s, sem, TOP_K)
    w1 = meta_ref[:, META_W1:META_W1 + 1]
    w2 = meta_ref[:, META_W2:META_W2 + 1]
    for c in range(nchunk):
        f_ref[:, c * LANES:(c + 1) * LANES] = w1 * a_s[:, c, :] + w2 * b_s[:, c, :]


def _combine(y, meta, p1, p2, *, tm):
    _, C, _ = y.shape
    R, NP = meta.shape
    return pl.pallas_call(
        _combine_kernel,
        grid_spec=pltpu.PrefetchScalarGridSpec(
            num_scalar_prefetch=2, grid=(R // tm,),
            in_specs=[pl.BlockSpec((tm, NP), lambda i, p1, p2: (i, 0)),
                      pl.BlockSpec(memory_space=pl.ANY)],
            out_specs=pl.BlockSpec((tm, C * LANES), lambda i, p1, p2: (i, 0)),
            scratch_shapes=[pltpu.VMEM((tm, C, LANES), F32), pltpu.VMEM((tm, C, LANES), F32),
                            pltpu.SemaphoreType.DMA]),
        out_shape=jax.ShapeDtypeStruct((R, C * LANES), F32),
        compiler_params=_params("arbitrary"),
        name="moe_combine",
    )(p1, p2, meta, y)


def _moe(h, h_rows, w_router_padded, w_gate, w_up, w_down, j, *, n_exp, tm_router, tile, sub, tn, tm_rows):
    R = h.shape[0]
    meta, counts = _router(h, w_router_padded, j, n_exp=n_exp, tm=tm_router)
    cnt = counts[0, :n_exp].astype(jnp.int32)
    ntile = (cnt + tile - 1) // tile
    tend = jnp.cumsum(ntile)
    tstart = tend - ntile
    n_tiles = -(-TOP_K * R // tile) + n_exp
    e1, e2 = meta[:, META_E1].astype(jnp.int32), meta[:, META_E2].astype(jnp.int32)
    p1 = tstart[e1] * tile + meta[:, META_R1].astype(jnp.int32)
    p2 = tstart[e2] * tile + meta[:, META_R2].astype(jnp.int32)
    tid = jnp.arange(n_tiles, dtype=jnp.int32)
    tile_expert = jnp.minimum(jnp.sum(tid[:, None] >= tend[None, :], axis=1), n_exp - 1).astype(jnp.int32)
    tile_valid = jnp.clip(cnt[tile_expert] - (tid - tstart[tile_expert]) * tile, 0, tile)
    tile_valid = jnp.where(tid < tend[-1], tile_valid, 0).astype(jnp.int32)
    xs = _dispatch(h_rows, p1, p2, n_slots=n_tiles * tile, tm=tm_rows)
    y = _moe_ffn(xs, w_gate, w_up, w_down, j * n_exp, tile_expert, tile_valid, tend[-1:].astype(jnp.int32),
                 tile=tile, sub=sub, tn=tn)
    return _combine(y, meta, p1, p2, tm=tm_rows)


def kernel(x_prompt, x_sample, cache_k_win, cache_v_win, state_rglru_h, state_conv, c_prompt, c_sample, norm1_w, norm2_w, w_ada, b_ada, w_in, q_norm_w, k_norm_w, attn_sinks, conv_w, conv_b, lru_w_a, lru_b_a, lru_w_x, lru_b_x, lru_lambda, chunk_v_norm_w, chunk_w_s, chunk_b_s, w_branch_a, w_branch_b, w_branch_c, w_out, ffn_w_gate, ffn_w_up, ffn_w_down, moe_w_router, moe_w_gate, moe_w_up, moe_w_down):
    Bp, S, D = x_prompt.shape
    Bs, Ts, _ = x_sample.shape
    L = w_in.shape[0]
    window, n_kv, hd = cache_k_win.shape[2:]
    n_q = attn_sinks.shape[1]
    wa, kvw = n_q * hd, n_kv * hd
    wb = conv_w.shape[-1]
    wc = chunk_v_norm_w.shape[-1]
    chunk = chunk_w_s.shape[-1]
    cwid = conv_w.shape[1]
    n_exp = moe_w_router.shape[-1]
    assert Ts == SUBLANES and S % (Bs * Ts) == 0 and S % chunk == 0 and S % window == 0
    Rp, Rs = Bp * S, Bs * Ts
    R = Rp + Rs
    half = wb // 2
    assert wb == wc and wa % half == 0 and kvw % half == 0 or True
    col_lx = (wa + 2 * kvw) // half
    col_lg = col_lx + 2
    col_cu = col_lg + 2
    col_cv = col_cu + 2
    gate_col = wa + 2 * kvw + 2 * wb + 2 * wc

    W = dict(conv_w=conv_w, conv_b=conv_b, lru_w_a=lru_w_a, lru_b_a=lru_b_a, lru_w_x=lru_w_x, lru_b_x=lru_b_x,
             lru_lambda=lru_lambda, chunk_v_norm_w=chunk_v_norm_w, chunk_w_s=chunk_w_s, chunk_b_s=chunk_b_s,
             w_branch_a=w_branch_a, w_branch_b=w_branch_b, w_branch_c=w_branch_c)

    n_c = Bp + Bs
    n_c_pad = -(-n_c // SUBLANES) * SUBLANES
    c_all = jnp.concatenate([c_prompt, c_sample, jnp.zeros((n_c_pad - n_c, D), F32)], axis=0)
    mod = _ada(c_all, w_ada, b_ada)
    mod_p = jnp.repeat(mod[:, :Bp], SUBLANES, axis=1)
    mod_s = jnp.repeat(mod[:, Bp:n_c], Ts, axis=1)

    x = jnp.concatenate([x_prompt.reshape(Rp, D), x_sample.reshape(Rs, D)], axis=0)
    rn = functools.partial(_resid_norm, seq=S, n_prompt=Bp)
    tm_big = _pick(R, 1408, 128) if R % 128 == 0 else _pick(R, 1408, 16)
    tm_mid = _pick(R, 768, 128) if R % 128 == 0 else _pick(R, 768, 16)
    w_router_p = jnp.pad(moe_w_router, ((0, 0), (0, 0), (0, LANES - n_exp)))
    moe_gate_w = moe_w_gate.reshape((-1,) + moe_w_gate.shape[2:])
    moe_up_w = moe_w_up.reshape((-1,) + moe_w_up.shape[2:])
    moe_down_w = moe_w_down.reshape((-1,) + moe_w_down.shape[2:])

    buf_rows = jnp.pad(state_conv, ((0, 0), (0, 0), (Ts - (cwid - 1), 0), (0, 0))).reshape(L, Rs, wb)
    h0_rep = jnp.repeat(state_rglru_h, Ts, axis=1)

    _, h = rn(x, None, mod_p, mod_s, norm1_w, 0, gate_k=None, scale_k=(0, 1), shift_k=(0, 0))
    outs = []
    for l in range(L):
        z = _matmul(h, w_in, l, out_dtype=F32, tm=tm_big, tn=_pick(w_in.shape[-1], 512, half), tk=D)
        attn_o, kwp, vwp = _attn_prompt(z, attn_sinks, q_norm_w, k_norm_w, l, R=R, n_prompt=Bp, seq=S,
                                        n_q=n_q, n_kv=n_kv, hd=hd, window=window)
        attn_o, kws, vws = _attn_sample(attn_o, z, cache_k_win, cache_v_win, attn_sinks, q_norm_w, k_norm_w, l,
                                        n_sample=Bs, T=Ts, n_q=n_q, n_kv=n_kv, hd=hd, window=window)
        lru_o, hp, cp8 = _lru_prompt(z, W, l, R=R, nseq=Bp, T=S, Tb=_pick(S, 256, SUBLANES),
                                     col_x=col_lx, col_g=col_lg, wb=wb)
        lru_o, hs_all = _lru_sample(lru_o, z, h0_rep[l], buf_rows[l], W, l, row0=Rp, T=Ts,
                                    col_x=col_lx, col_g=col_lg, wb=wb)
        chunk_o, vnp = _chunk(None, z, W, l, R=R, row0=0, nsteps=Rp // chunk, nb=1, Tc=chunk,
                              cps=S // chunk, col_u=col_cu, col_v=col_cv, wc=wc)
        chunk_o, vns = _chunk(chunk_o, z, W, l, R=R, row0=Rp, nsteps=1, nb=Bs, Tc=Ts, cps=1,
                              col_u=col_cu, col_v=col_cv, wc=wc)
        m = _merge(attn_o, lru_o, chunk_o, z, W, l, gate_col=gate_col, tm=tm_big, tn=_pick(D, 256, LANES))
        y = _matmul(m, w_out, l, out_dtype=F32, tm=tm_big, tn=_pick(D, 512, LANES), tk=D)
        j = l // 2
        if l % 2 == 0:
            x, h = rn(x, y, mod_p, mod_s, norm2_w, l, gate_k=(l, 2), scale_k=(l, 4), shift_k=(l, 3))
            a = _glu(h, ffn_w_gate, ffn_w_up, j, 1, None, tm=tm_big, tn=_pick(ffn_w_gate.shape[-1], 512, LANES))
            f = _matmul(a, ffn_w_down, j, out_dtype=F32, tm=tm_big, tn=_pick(D, 1024, LANES),
                        tk=_pick(a.shape[1], 1408, LANES))
        else:
            x, h, h_rows = rn(x, y, mod_p, mod_s, norm2_w, l, gate_k=(l, 2), scale_k=(l, 4), shift_k=(l, 3),
                              rows_out=True)
            f = _moe(h, h_rows, w_router_p, moe_gate_w, moe_up_w, moe_down_w, j, n_exp=n_exp, tm_router=tm_mid,
                     tile=MOE_TILE_SUBS * Rs, sub=Rs, tn=_pick(moe_gate_w.shape[-1], 256, LANES), tm_rows=Rs)
        if l + 1 < L:
            x, h = rn(x, f, mod_p, mod_s, norm1_w, l + 1, gate_k=(l, 5), scale_k=(l + 1, 1), shift_k=(l + 1, 0))
        else:
            x, _ = rn(x, f, mod_p, mod_s, None, l, gate_k=(l, 5), scale_k=None, shift_k=None)
        conv_p = cp8[:, SUBLANES - (cwid - 1):, :]
        lx_s = z[Rp:, col_lx * half:col_lx * half + wb].reshape(Bs, Ts, wb)
        conv_s = jnp.concatenate([state_conv[l], lx_s], axis=1)[:, -(cwid - 1):]
        hs = hs_all.reshape(Bs, Ts, wb)[:, Ts - 1]
        outs.append((kwp.reshape(Bp, window, n_kv, hd), vwp.reshape(Bp, window, n_kv, hd), hp.reshape(Bp, wb),
                     conv_p, vnp,
                     kws.reshape(Bs, window, n_kv, hd), vws.reshape(Bs, window, n_kv, hd), hs.reshape(Bs, wb),
                     conv_s, vns.reshape(Bs, Ts, wc)))
    st = [jnp.stack(s) for s in zip(*outs)]
    y_prompt = x[:Rp].reshape(Bp, S, D)
    y_sample = x[Rp:].reshape(Bs, Ts, D)
    return (y_prompt, y_sample, st[0], st[1], st[2], st[3], st[4], st[5], st[6], st[7], st[8], st[9])
```

```python
import functools

import jax
import jax.numpy as jnp
from jax import lax
from jax.experimental import pallas as pl
from jax.experimental.pallas import tpu as pltpu

EPS = 1e-6
LRU_C = 8.0
PAST_LEN = 16384
TOP_K = 2
MOE_SUB_ROWS = 384
MOE_TILE_SUBS = 3
SUBLANES = 8
LANES = 128
V7X_VMEM_LIMIT = 56 * 1024 * 1024

F32 = jnp.float32
BF16 = jnp.bfloat16


def _pick(n, target, mult):
    best = None
    for t in range(mult, min(n, target) + 1, mult):
        if n % t == 0:
            best = t
    assert best is not None, (n, target, mult)
    return best


def _params(*sem):
    return pltpu.CompilerParams(dimension_semantics=sem, vmem_limit_bytes=V7X_VMEM_LIMIT)


def _rms(x, w):
    return x * lax.rsqrt(jnp.mean(x * x, axis=-1, keepdims=True) + EPS) * w


def _ada_kernel(c_ref, w_ref, b_ref, o_ref):
    a = jax.nn.silu(c_ref[...]).astype(BF16)
    o_ref[...] = jnp.dot(a, w_ref[...].astype(BF16), preferred_element_type=F32) + b_ref[...]


def _ada(c_all, w_ada, b_ada):
    L, D, W6 = w_ada.shape
    Bc = c_all.shape[0]
    tn = _pick(W6, 1024, LANES)
    return pl.pallas_call(
        _ada_kernel,
        grid=(L, W6 // tn),
        in_specs=[pl.BlockSpec((Bc, D), lambda l, j: (0, 0)),
                  pl.BlockSpec((None, D, tn), lambda l, j: (l, 0, j)),
                  pl.BlockSpec((None, 1, tn), lambda l, j: (l, 0, j))],
        out_specs=pl.BlockSpec((None, Bc, tn), lambda l, j: (l, 0, j)),
        out_shape=jax.ShapeDtypeStruct((L, Bc, W6), F32),
        compiler_params=_params("parallel", "parallel"),
        name="ada_mod",
    )(c_all, w_ada, b_ada.reshape(L, 1, W6))


def _resid_norm_kernel(*refs, n_prompt_tiles, has_resid, has_norm, has_rows):
    refs = list(refs)
    x_ref = refs.pop(0)
    if has_resid:
        y_ref, gp_ref, gs_ref = refs.pop(0), refs.pop(0), refs.pop(0)
    if has_norm:
        scp_ref, scs_ref, shp_ref, shs_ref, nw_ref = (refs.pop(0) for _ in range(5))
    if has_resid:
        xo_ref = refs.pop(0)
    if has_norm:
        h_ref = refs.pop(0)
    if has_rows:
        hrow_ref = refs.pop(0)
    i = pl.program_id(0)

    def body(prompt):
        mod = (lambda p, s: p[0:1, :]) if prompt else (lambda p, s: s[...])
        x = x_ref[...]
        if has_resid:
            x = x + mod(gp_ref, gs_ref) * y_ref[...]
            xo_ref[...] = x
        if has_norm:
            hn = _rms(x, nw_ref[...])
            h = hn * (1.0 + mod(scp_ref, scs_ref)) + mod(shp_ref, shs_ref)
            h_ref[...] = h.astype(h_ref.dtype)
            if has_rows:
                for c in range(hrow_ref.shape[1]):
                    hrow_ref[:, c, :] = h[:, c * LANES:(c + 1) * LANES]

    pl.when(i < n_prompt_tiles)(lambda: body(True))
    pl.when(i >= n_prompt_tiles)(lambda: body(False))


def _resid_norm(x, y, mod_p, mod_s, norm_w, l, *, gate_k, scale_k, shift_k, seq, n_prompt, rows_out=False):
    R, D = x.shape
    TE = mod_s.shape[1]
    tpb = seq // TE
    npt = n_prompt * tpb
    has_resid = y is not None
    has_norm = norm_w is not None
    row = pl.BlockSpec((TE, D), lambda i: (i, 0))

    def mp(lk):
        return pl.BlockSpec((None, SUBLANES, D), lambda i: (lk[0], jnp.minimum(i // tpb, n_prompt - 1), lk[1]))

    def ms(lk):
        return pl.BlockSpec((None, TE, D), lambda i: (lk[0], 0, lk[1]))

    args, specs, outs, out_specs = [x], [row], [], []
    if has_resid:
        args += [y, mod_p, mod_s]
        specs += [row, mp(gate_k), ms(gate_k)]
        outs.append(jax.ShapeDtypeStruct((R, D), F32))
        out_specs.append(row)
    if has_norm:
        args += [mod_p, mod_s, mod_p, mod_s, norm_w.reshape(norm_w.shape[0], 1, D)]
        specs += [mp(scale_k), ms(scale_k), mp(shift_k), ms(shift_k),
                  pl.BlockSpec((None, 1, D), lambda i: (l, 0, 0))]
        outs.append(jax.ShapeDtypeStruct((R, D), BF16))
        out_specs.append(row)
    if rows_out:
        outs.append(jax.ShapeDtypeStruct((R, D // LANES, LANES), F32))
        out_specs.append(pl.BlockSpec((TE, D // LANES, LANES), lambda i: (i, 0, 0)))
    res = pl.pallas_call(
        functools.partial(_resid_norm_kernel, n_prompt_tiles=npt, has_resid=has_resid, has_norm=has_norm,
                          has_rows=rows_out),
        grid=(R // TE,),
        in_specs=specs, out_specs=out_specs, out_shape=outs,
        compiler_params=_params("parallel"),
        name="resid_norm",
    )(*args)
    res = list(res)
    x_new = res.pop(0) if has_resid else x
    h = res.pop(0) if has_norm else None
    if rows_out:
        return x_new, h, res.pop(0)
    return x_new, h


def _mm_kernel(a_ref, w_ref, o_ref, acc_ref, *, nk):
    part = jnp.dot(a_ref[...], w_ref[...].astype(BF16), preferred_element_type=F32)
    if nk == 1:
        o_ref[...] = part.astype(o_ref.dtype)
        return
    k = pl.program_id(2)

    @pl.when(k == 0)
    def _():
        acc_ref[...] = part

    @pl.when(k > 0)
    def _():
        acc_ref[...] += part

    @pl.when(k == nk - 1)
    def _():
        o_ref[...] = acc_ref[...].astype(o_ref.dtype)


def _matmul(a, w, l, *, out_dtype, tm, tn, tk):
    R, K = a.shape
    N = w.shape[-1]
    nk = K // tk
    acc_shape = (tm, tn) if nk > 1 else (SUBLANES, LANES)
    return pl.pallas_call(
        functools.partial(_mm_kernel, nk=nk),
        grid=(R // tm, N // tn, nk),
        in_specs=[pl.BlockSpec((tm, tk), lambda i, j, k: (i, k)),
                  pl.BlockSpec((None, tk, tn), lambda i, j, k: (l, k, j))],
        out_specs=pl.BlockSpec((tm, tn), lambda i, j, k: (i, j)),
        out_shape=jax.ShapeDtypeStruct((R, N), out_dtype),
        scratch_shapes=[pltpu.VMEM(acc_shape, F32)],
        compiler_params=_params("parallel", "parallel", "arbitrary"),
        name="matmul",
    )(a, w)


def _softmax_sink_pv(s, sink, v):
    m = jnp.maximum(jnp.max(s, axis=-1, keepdims=True), sink)
    e = jnp.exp(s - m)
    den = jnp.sum(e, axis=-1, keepdims=True) + jnp.exp(sink - m)
    return jnp.dot((e / den).astype(BF16), v, preferred_element_type=F32)


def _attn_prompt_kernel(sink_ref, q_ref, kp_ref, kc_ref, vp_ref, vc_ref, qw_ref, kw_ref,
                        o_ref, kwin_ref, vwin_ref, *, l, n_kv, q_per_kv, hd, window, nblk):
    i = pl.program_id(1)
    blk = q_ref.shape[0]
    scale = hd ** -0.5
    rows_q = q_per_kv * blk
    r = lax.broadcasted_iota(jnp.int32, (rows_q, 2 * blk), 0) % blk
    c = lax.broadcasted_iota(jnp.int32, (rows_q, 2 * blk), 1)
    dist = r + blk - c
    ok = (dist >= 0) & (dist <= window) & ((c >= blk) | (i > 0))
    hrow = lax.broadcasted_iota(jnp.int32, (rows_q, 1), 0) // blk
    qw = qw_ref[...]
    kw = kw_ref[...]
    for g in range(n_kv):
        sl = slice(g * hd, (g + 1) * hd)
        kcn = _rms(kc_ref[:, sl], kw)
        kcat = jnp.concatenate([_rms(kp_ref[:, sl], kw), kcn], axis=0).astype(BF16)
        vcat = jnp.concatenate([vp_ref[:, sl], vc_ref[:, sl]], axis=0).astype(BF16)
        heads = [g * q_per_kv + h for h in range(q_per_kv)]
        qs = jnp.concatenate([_rms(q_ref[:, hh * hd:(hh + 1) * hd], qw) for hh in heads], axis=0).astype(BF16)
        s = lax.dot_general(qs, kcat, (((1,), (1,)), ((), ())), preferred_element_type=F32) * scale
        s = jnp.where(ok, s, -1e30)
        sink = jnp.zeros((rows_q, 1), F32)
        for h, hh in enumerate(heads):
            sink = jnp.where(hrow == h, sink_ref[l, hh], sink)
        o = _softmax_sink_pv(s, sink, vcat)
        for h, hh in enumerate(heads):
            o_ref[:, hh * hd:(hh + 1) * hd] = o[h * blk:(h + 1) * blk, :].astype(o_ref.dtype)

        @pl.when(i == nblk - 1)
        def _():
            kwin_ref[:, sl] = kcn
            vwin_ref[:, sl] = vc_ref[:, sl]


def _attn_prompt(z, sinks, q_norm_w, k_norm_w, l, *, R, n_prompt, seq, n_q, n_kv, hd, window):
    blk = window
    nblk = seq // blk
    wa = n_q * hd
    kvw = n_kv * hd
    kcol = wa // kvw
    vcol = kcol + 1

    def cur(col):
        return pl.BlockSpec((blk, kvw), lambda b, i: (b * nblk + i, col))

    def prev(col):
        return pl.BlockSpec((blk, kvw), lambda b, i: (b * nblk + jnp.maximum(i - 1, 0), col))

    L = q_norm_w.shape[0]
    nw = pl.BlockSpec((None, 1, hd), lambda b, i: (l, 0, 0))
    return pl.pallas_call(
        functools.partial(_attn_prompt_kernel, l=l, n_kv=n_kv, q_per_kv=n_q // n_kv, hd=hd,
                          window=window, nblk=nblk),
        grid=(n_prompt, nblk),
        in_specs=[pl.BlockSpec(memory_space=pltpu.SMEM),
                  pl.BlockSpec((blk, wa), lambda b, i: (b * nblk + i, 0)),
                  prev(kcol), cur(kcol), prev(vcol), cur(vcol), nw, nw],
        out_specs=[pl.BlockSpec((blk, wa), lambda b, i: (b * nblk + i, 0)),
                   pl.BlockSpec((None, blk, kvw), lambda b, i: (b, 0, 0)),
                   pl.BlockSpec((None, blk, kvw), lambda b, i: (b, 0, 0))],
        out_shape=[jax.ShapeDtypeStruct((R, wa), BF16),
                   jax.ShapeDtypeStruct((n_prompt, blk, kvw), F32),
                   jax.ShapeDtypeStruct((n_prompt, blk, kvw), F32)],
        compiler_params=_params("parallel", "arbitrary"),
        name="attn_prompt",
    )(sinks, z, z, z, z, z, q_norm_w.reshape(L, 1, hd), k_norm_w.reshape(L, 1, hd))


def _attn_sample_kernel(sink_ref, o_in_ref, q_ref, kn_ref, vn_ref, ck_ref, cv_ref, qw_ref, kw_ref,
                        o_ref, kwin_ref, vwin_ref, keys_s, vals_s, ostage,
                        *, l, nb, T, n_kv, q_per_kv, hd, window):
    del o_in_ref
    W = ck_ref.shape[1]
    scale = hd ** -0.5
    rows_q = q_per_kv * T
    r = lax.broadcasted_iota(jnp.int32, (rows_q, 2 * W), 0) % T
    c = lax.broadcasted_iota(jnp.int32, (rows_q, 2 * W), 1)
    q_pos = PAST_LEN + r
    k_pos = jnp.where(c < W, PAST_LEN - W + c, PAST_LEN + c - W)
    dist = q_pos - k_pos
    ok = (dist >= 0) & (dist <= window) & (k_pos >= 0) & (c < W + T)
    hrow = lax.broadcasted_iota(jnp.int32, (rows_q, 1), 0) // T
    qw = qw_ref[...]
    kw = kw_ref[...]
    keys_s[...] = jnp.zeros_like(keys_s)
    vals_s[...] = jnp.zeros_like(vals_s)

    def step(b, carry):
        rows = pl.ds(pl.multiple_of(b * T, T), T)
        for g in range(n_kv):
            sl = slice(g * hd, (g + 1) * hd)
            ck = ck_ref[b, :, sl]
            cv = cv_ref[b, :, sl]
            knn = _rms(kn_ref[rows, sl], kw)
            vnn = vn_ref[rows, sl]
            keys_s[0:W, :] = ck
            keys_s[W:W + T, :] = knn
            vals_s[0:W, :] = cv
            vals_s[W:W + T, :] = vnn
            qs = jnp.concatenate(
                [_rms(q_ref[rows, (g * q_per_kv + h) * hd:(g * q_per_kv + h + 1) * hd], qw)
                 for h in range(q_per_kv)], axis=0).astype(BF16)
            s = lax.dot_general(qs, keys_s[...].astype(BF16), (((1,), (1,)), ((), ())),
                                preferred_element_type=F32) * scale
            s = jnp.where(ok, s, -1e30)
            sink = jnp.zeros((rows_q, 1), F32)
            for h in range(q_per_kv):
                sink = jnp.where(hrow == h, sink_ref[l, g * q_per_kv + h], sink)
            o = _softmax_sink_pv(s, sink, vals_s[...].astype(BF16))
            for h in range(q_per_kv):
                head = g * q_per_kv + h
                ostage[rows, head * hd:(head + 1) * hd] = o[h * T:(h + 1) * T, :]
            kwin_ref[b, 0:W - T, sl] = ck[T:, :]
            kwin_ref[b, W - T:W, sl] = knn
            vwin_ref[b, 0:W - T, sl] = cv[T:, :]
            vwin_ref[b, W - T:W, sl] = vnn
        return carry

    lax.fori_loop(0, nb, step, 0)
    o_ref[...] = ostage[...].astype(o_ref.dtype)


def _attn_sample(attn_o, z, cache_k, cache_v, sinks, q_norm_w, k_norm_w, l, *, n_sample, T, n_q, n_kv, hd,
                 window):
    R, wa = attn_o.shape
    kvw = n_kv * hd
    rows = n_sample * T
    blk_i = (R - rows) // rows
    kcol = wa // kvw
    L, nb, W = cache_k.shape[:3]
    ck = cache_k.reshape(L, nb, W, kvw)
    cv = cache_v.reshape(L, nb, W, kvw)
    nw = pl.BlockSpec((None, 1, hd), lambda i: (l, 0, 0))
    cache = pl.BlockSpec((None, nb, W, kvw), lambda i: (l, 0, 0, 0))
    win = pl.BlockSpec((nb, W, kvw), lambda i: (0, 0, 0))
    return pl.pallas_call(
        functools.partial(_attn_sample_kernel, l=l, nb=nb, T=T, n_kv=n_kv, q_per_kv=n_q // n_kv, hd=hd,
                          window=window),
        grid=(1,),
        in_specs=[pl.BlockSpec(memory_space=pltpu.SMEM),
                  pl.BlockSpec(memory_space=pl.ANY),
                  pl.BlockSpec((rows, wa), lambda i: (blk_i, 0)),
                  pl.BlockSpec((rows, kvw), lambda i: (blk_i, kcol)),
                  pl.BlockSpec((rows, kvw), lambda i: (blk_i, kcol + 1)),
                  cache, cache, nw, nw],
        out_specs=[pl.BlockSpec((rows, wa), lambda i: (blk_i, 0)), win, win],
        out_shape=[jax.ShapeDtypeStruct((R, wa), BF16),
                   jax.ShapeDtypeStruct((nb, W, kvw), F32),
                   jax.ShapeDtypeStruct((nb, W, kvw), F32)],
        scratch_shapes=[pltpu.VMEM((2 * W, hd), F32), pltpu.VMEM((2 * W, hd), F32),
                        pltpu.VMEM((rows, wa), F32)],
        input_output_aliases={1: 0},
        compiler_params=_params("arbitrary"),
        name="attn_sample",
    )(sinks, attn_o, z, z, z, ck, cv, q_norm_w.reshape(-1, 1, hd), k_norm_w.reshape(-1, 1, hd))


def _lru_gates(xc, wa_ref, ba_ref, wx_ref, bx_ref, lam_ref, a_s, u_s, *, nblk, lb):
    xcb = xc.astype(BF16)
    for n in range(nblk):
        sl = slice(n * lb, (n + 1) * lb)
        ra = jnp.dot(xcb[:, sl], wa_ref[n].astype(BF16), preferred_element_type=F32) + ba_ref[:, sl]
        rx = jnp.dot(xcb[:, sl], wx_ref[n].astype(BF16), preferred_element_type=F32) + bx_ref[:, sl]
        log_a = -LRU_C * jax.nn.sigmoid(ra) * jax.nn.softplus(-lam_ref[:, sl])
        a_s[:, sl] = jnp.exp(log_a)
        th = jnp.tanh(log_a)
        u_s[:, sl] = jnp.sqrt(-2.0 * th / (1.0 - th)) * jax.nn.sigmoid(rx) * xc[:, sl]

def _tile_scan(a, u, T):
    t = lax.broadcasted_iota(jnp.int32, (a.shape[0], 1), 0) % T
    d = 1
    while d < T:
        keep = t >= d
        u = jnp.where(keep, a * pltpu.roll(u, d, axis=0) + u, u)
        a = jnp.where(keep, a * pltpu.roll(a, d, axis=0), a)
        d *= 2
    return a, u


def _lru_prompt_kernel(x_ref, g_ref, cw_ref, cb_ref, wa_ref, ba_ref, wx_ref, bx_ref, lam_ref,
                       o_ref, hT_ref, cnew_ref, xp_s, a_s, u_s, h_s, *, Tb, ntb, cw, nblk, lb):
    tb = pl.program_id(2)
    P = SUBLANES

    @pl.when(tb == 0)
    def _():
        xp_s[0:P, :] = jnp.zeros((P, xp_s.shape[1]), F32)
        h_s[...] = jnp.zeros_like(h_s)

    xp_s[P:P + Tb, :] = x_ref[...]
    xc = cb_ref[...]
    for j in range(cw):
        off = P - (cw - 1) + j
        xc = xc + xp_s[off:off + Tb, :] * cw_ref[j:j + 1, :]
    _lru_gates(xc, wa_ref, ba_ref, wx_ref, bx_ref, lam_ref, a_s, u_s, nblk=nblk, lb=lb)

    a, u = _tile_scan(a_s[...], u_s[...], P)
    a_s[...] = a
    u_s[...] = u

    def step(k, h):
        rr = pl.ds(pl.multiple_of(k * P, P), P)
        hs = u_s[rr, :] + a_s[rr, :] * h
        u_s[rr, :] = hs
        return hs[P - 1:P, :]

    h = lax.fori_loop(0, Tb // P, step, h_s[...])
    h_s[...] = h
    o_ref[...] = (jax.nn.gelu(g_ref[...]) * u_s[...]).astype(o_ref.dtype)
    tail = xp_s[Tb:Tb + P, :]
    xp_s[0:P, :] = tail

    @pl.when(tb == ntb - 1)
    def _():
        hT_ref[...] = h
        cnew_ref[...] = tail


def _lru_sample_kernel(o_in_ref, x_ref, g_ref, h0_ref, buf_ref, cw_ref, cb_ref, wa_ref, ba_ref, wx_ref, bx_ref,
                       lam_ref, o_ref, hs_ref, a_s, u_s, *, T, cw, nblk, lb):
    del o_in_ref
    rows = x_ref.shape[0]
    t = lax.broadcasted_iota(jnp.int32, (rows, 1), 0) % T
    x = x_ref[...]
    buf = buf_ref[...]
    xc = cb_ref[...] + x * cw_ref[cw - 1:cw, :]
    for s in range(1, cw):
        xs = jnp.where(t >= s, pltpu.roll(x, s, axis=0), pltpu.roll(buf, rows - (T - s), axis=0))
        xc = xc + xs * cw_ref[cw - 1 - s:cw - s, :]
    _lru_gates(xc, wa_ref, ba_ref, wx_ref, bx_ref, lam_ref, a_s, u_s, nblk=nblk, lb=lb)
    a, u = _tile_scan(a_s[...], u_s[...], T)
    hs = u + a * h0_ref[...]
    hs_ref[...] = hs
    o_ref[...] = (jax.nn.gelu(g_ref[...]) * hs).astype(o_ref.dtype)


def _lru_weight_args(W, l, wb, index):
    ch = wb // 2
    nlb, lb = W['lru_w_a'].shape[1:3]
    nblk = nlb // 2
    cw = W['conv_w'].shape[1]

    def vec(name):
        return W[name].reshape(W[name].shape[0], 1, wb), pl.BlockSpec((None, 1, ch), index(lambda hf: (l, 0, hf)))

    def blkw(name):
        return W[name], pl.BlockSpec((None, nblk, lb, lb), index(lambda hf: (l, hf, 0, 0)))

    pairs = [(W['conv_w'], pl.BlockSpec((None, cw, ch), index(lambda hf: (l, 0, hf)))),
             vec('conv_b'), blkw('lru_w_a'), vec('lru_b_a'), blkw('lru_w_x'), vec('lru_b_x'), vec('lru_lambda')]
    return [p[0] for p in pairs], [p[1] for p in pairs], dict(cw=cw, nblk=nblk, lb=lb)


def _lru_prompt(z, W, l, *, R, nseq, T, Tb, col_x, col_g, wb):
    ch = wb // 2
    ntb = T // Tb
    wargs, wspecs, kw = _lru_weight_args(W, l, wb, lambda f: (lambda s, hf, t: f(hf)))

    def rowblk(col):
        return pl.BlockSpec((Tb, ch), lambda s, hf, t: (s * ntb + t, col + hf))

    return pl.pallas_call(
        functools.partial(_lru_prompt_kernel, Tb=Tb, ntb=ntb, **kw),
        grid=(nseq, 2, ntb),
        in_specs=[rowblk(col_x), rowblk(col_g)] + wspecs,
        out_specs=[pl.BlockSpec((Tb, ch), lambda s, hf, t: (s * ntb + t, hf)),
                   pl.BlockSpec((None, 1, ch), lambda s, hf, t: (s, 0, hf)),
                   pl.BlockSpec((None, SUBLANES, ch), lambda s, hf, t: (s, 0, hf))],
        out_shape=[jax.ShapeDtypeStruct((R, wb), BF16),
                   jax.ShapeDtypeStruct((nseq, 1, wb), F32),
                   jax.ShapeDtypeStruct((nseq, SUBLANES, wb), F32)],
        scratch_shapes=[pltpu.VMEM((SUBLANES + Tb, ch), F32), pltpu.VMEM((Tb, ch), F32),
                        pltpu.VMEM((Tb, ch), F32), pltpu.VMEM((1, ch), F32)],
        compiler_params=_params("parallel", "parallel", "arbitrary"),
        name="lru_prompt",
    )(z, z, *wargs)


def _lru_sample(lru_o, z, h0_rep, buf_rows, W, l, *, row0, T, col_x, col_g, wb):
    ch = wb // 2
    rows = h0_rep.shape[0]
    rb = row0 // rows
    wargs, wspecs, kw = _lru_weight_args(W, l, wb, lambda f: (lambda hf: f(hf)))
    st = pl.BlockSpec((rows, ch), lambda hf: (0, hf))
    return pl.pallas_call(
        functools.partial(_lru_sample_kernel, T=T, **kw),
        grid=(2,),
        in_specs=[pl.BlockSpec(memory_space=pl.ANY),
                  pl.BlockSpec((rows, ch), lambda hf: (rb, col_x + hf)),
                  pl.BlockSpec((rows, ch), lambda hf: (rb, col_g + hf)), st, st] + wspecs,
        out_specs=[pl.BlockSpec((rows, ch), lambda hf: (rb, hf)), st],
        out_shape=[jax.ShapeDtypeStruct(lru_o.shape, BF16), jax.ShapeDtypeStruct((rows, wb), F32)],
        scratch_shapes=[pltpu.VMEM((rows, ch), F32), pltpu.VMEM((rows, ch), F32)],
        input_output_aliases={0: 0},
        compiler_params=_params("parallel"),
        name="lru_sample",
    )(lru_o, z, z, h0_rep, buf_rows, *wargs)


def _chunk_kernel(*refs, nb, Tc, ngroups, gw, cps, aliased):
    refs = list(refs)
    if aliased:
        refs.pop(0)
    ulo_ref, uhi_ref, vlo_ref, vhi_ref, nw_ref, ws_ref, bst_ref, o_ref, vn_ref = refs
    rows = nb * Tc
    half = vlo_ref.shape[1]
    wc = 2 * half
    v_lo = jax.nn.gelu(vlo_ref[...])
    v_hi = jax.nn.gelu(vhi_ref[...])
    ms = (jnp.sum(v_lo * v_lo, axis=-1, keepdims=True) + jnp.sum(v_hi * v_hi, axis=-1, keepdims=True)) / wc
    inv = lax.rsqrt(ms + EPS)
    vn_halves = (v_lo * inv * nw_ref[:, 0:half], v_hi * inv * nw_ref[:, half:wc])
    u_halves = (ulo_ref, uhi_ref)

    r = lax.broadcasted_iota(jnp.int32, (rows, rows), 0)
    c = lax.broadcasted_iota(jnp.int32, (rows, rows), 1)
    mask = (r // Tc == c // Tc) & (c % Tc <= r % Tc)
    if nb > 1:
        sel = (lax.broadcasted_iota(jnp.int32, (ws_ref.shape[2], rows), 0)
               == lax.broadcasted_iota(jnp.int32, (ws_ref.shape[2], rows), 1) % Tc).astype(BF16)
        bias_rows = jnp.broadcast_to(bst_ref[0:Tc, :][None], (nb, Tc, bst_ref.shape[1])).reshape(rows, -1)
    else:
        bias_rows = bst_ref[0:Tc, :]
    gph = half // gw
    for g in range(ngroups):
        hf, gi = divmod(g, gph)
        sl = slice(gi * gw, (gi + 1) * gw)
        if nb > 1:
            t1 = jnp.broadcast_to(ws_ref[g, 0:Tc, :][None], (nb, Tc, ws_ref.shape[2])).reshape(rows, -1)
            wfull = jnp.dot(t1.astype(BF16), sel, preferred_element_type=F32)
        else:
            wfull = ws_ref[g, 0:Tc, 0:Tc]
        wm = jnp.where(mask, wfull, 0.0).astype(BF16)
        mixed = jnp.dot(wm, vn_halves[hf][:, sl].astype(BF16), preferred_element_type=F32)
        mixed = mixed + bias_rows[:, g:g + 1]
        o_ref[:, g * gw:(g + 1) * gw] = (jax.nn.gelu(u_halves[hf][:, sl]) * mixed).astype(o_ref.dtype)

    if cps == 1:
        vn_ref[:, 0:half] = vn_halves[0]
        vn_ref[:, half:wc] = vn_halves[1]
    else:
        @pl.when(pl.program_id(0) % cps == cps - 1)
        def _():
            vn_ref[:, 0:half] = vn_halves[0]
            vn_ref[:, half:wc] = vn_halves[1]


def _chunk(chunk_o, z, W, l, *, R, row0, nsteps, nb, Tc, cps, col_u, col_v, wc):
    half = wc // 2
    ngroups, chunk = W['chunk_w_s'].shape[1:3]
    gw = wc // ngroups
    rows = nb * Tc
    rb0 = row0 // rows
    aliased = chunk_o is not None

    def rowblk(col):
        return pl.BlockSpec((rows, half), lambda i: (rb0 + i, col))

    args = [z, z, z, z, W['chunk_v_norm_w'].reshape(-1, 1, wc), W['chunk_w_s'],
            jnp.swapaxes(W['chunk_b_s'], 1, 2)]
    specs = [rowblk(col_u), rowblk(col_u + 1), rowblk(col_v), rowblk(col_v + 1),
             pl.BlockSpec((None, 1, wc), lambda i: (l, 0, 0)),
             pl.BlockSpec((None, ngroups, chunk, chunk), lambda i: (l, 0, 0, 0)),
             pl.BlockSpec((None, chunk, ngroups), lambda i: (l, 0, 0))]
    io_alias = {}
    if aliased:
        args.insert(0, chunk_o)
        specs.insert(0, pl.BlockSpec(memory_space=pl.ANY))
        io_alias = {0: 0}
    return pl.pallas_call(
        functools.partial(_chunk_kernel, nb=nb, Tc=Tc, ngroups=ngroups, gw=gw, cps=cps, aliased=aliased),
        grid=(nsteps,),
        in_specs=specs,
        out_specs=[pl.BlockSpec((rows, wc), lambda i: (rb0 + i, 0)),
                   pl.BlockSpec((None, rows, wc), lambda i: (i // cps, 0, 0))],
        out_shape=[jax.ShapeDtypeStruct((R, wc), BF16),
                   jax.ShapeDtypeStruct((nsteps // cps, rows, wc), F32)],
        input_output_aliases=io_alias,
        compiler_params=_params("arbitrary"),
        name="chunk_sample" if aliased else "chunk_prompt",
    )(*args)


def _merge_kernel(a_ref, b_ref, c_ref, wa_ref, wb_ref, wc_ref, ga_ref, gb_ref, gc_ref, o_ref):
    def branch(x_ref, w_ref, g_ref):
        y = jnp.dot(x_ref[...], w_ref[...].astype(BF16), preferred_element_type=F32)
        return jax.nn.sigmoid(g_ref[...]) * y

    m = branch(a_ref, wa_ref, ga_ref) + branch(b_ref, wb_ref, gb_ref) + branch(c_ref, wc_ref, gc_ref)
    o_ref[...] = m.astype(o_ref.dtype)


def _merge(attn_o, lru_o, chunk_o, z, W, l, *, gate_col, tm, tn):
    R = attn_o.shape[0]
    D = W['w_branch_a'].shape[-1]
    g0 = gate_col // tn
    gstep = D // tn

    def xin(a):
        return pl.BlockSpec((tm, a.shape[1]), lambda i, j: (i, 0))

    def win(w):
        return pl.BlockSpec((None, w.shape[1], tn), lambda i, j: (l, 0, j))

    def gin(k):
        return pl.BlockSpec((tm, tn), lambda i, j: (i, g0 + k * gstep + j))

    return pl.pallas_call(
        _merge_kernel,
        grid=(R // tm, D // tn),
        in_specs=[xin(attn_o), xin(lru_o), xin(chunk_o),
                  win(W['w_branch_a']), win(W['w_branch_b']), win(W['w_branch_c']),
                  gin(0), gin(1), gin(2)],
        out_specs=pl.BlockSpec((tm, tn), lambda i, j: (i, j)),
        out_shape=jax.ShapeDtypeStruct((R, D), BF16),
        compiler_params=_params("parallel", "parallel"),
        name="merge",
    )(attn_o, lru_o, chunk_o, W['w_branch_a'], W['w_branch_b'], W['w_branch_c'], z, z, z)


def _glu_kernel(*refs, gated):
    if gated:
        h_ref, wg_ref, wu_ref, gate_ref, o_ref = refs
    else:
        h_ref, wg_ref, wu_ref, o_ref = refs
    h = h_ref[...]
    g = jnp.dot(h, wg_ref[...].astype(BF16), preferred_element_type=F32)
    u = jnp.dot(h, wu_ref[...].astype(BF16), preferred_element_type=F32)
    a = jax.nn.silu(g) * u
    if gated:
        e = pl.program_id(1)
        gate = gate_ref[...]
        lane = lax.broadcasted_iota(jnp.int32, gate.shape, 1)
        a = a * jnp.sum(jnp.where(lane == e, gate, 0.0), axis=-1, keepdims=True)
    o_ref[...] = a.astype(o_ref.dtype)


def _glu(h, w_gate, w_up, l0, n_exp, gate, *, tm, tn):
    R, D = h.shape
    F = w_gate.shape[-1]
    nf = F // tn
    gated = gate is not None
    wspec = pl.BlockSpec((None, D, tn), lambda i, e, j: (l0 + e, 0, j))
    args = [h, w_gate, w_up]
    specs = [pl.BlockSpec((tm, D), lambda i, e, j: (i, 0)), wspec, wspec]
    if gated:
        args.append(gate)
        specs.append(pl.BlockSpec((tm, gate.shape[1]), lambda i, e, j: (i, 0)))
    return pl.pallas_call(
        functools.partial(_glu_kernel, gated=gated),
        grid=(R // tm, n_exp, nf),
        in_specs=specs,
        out_specs=pl.BlockSpec((tm, tn), lambda i, e, j: (i, e * nf + j)),
        out_shape=jax.ShapeDtypeStruct((R, n_exp * F), BF16),
        compiler_params=_params("parallel", "parallel", "parallel"),
        name="glu",
    )(*args)


META_E1, META_E2, META_W1, META_W2, META_R1, META_R2 = range(6)


def _router_kernel(h_ref, w_ref, meta_ref, cnt_ref, carry, *, n_exp):
    i = pl.program_id(0)

    @pl.when(i == 0)
    def _():
        carry[...] = jnp.zeros_like(carry)

    logits = jnp.dot(h_ref[...], w_ref[...].astype(BF16), preferred_element_type=F32)
    tm = logits.shape[0]
    lane = lax.broadcasted_iota(jnp.int32, logits.shape, 1).astype(F32)
    big = float(logits.shape[1])
    l0 = jnp.where(lane < n_exp, logits, -jnp.inf)
    m1 = jnp.max(l0, axis=-1, keepdims=True)
    i1 = jnp.min(jnp.where(l0 == m1, lane, big), axis=-1, keepdims=True)
    l1 = jnp.where(lane == i1, -jnp.inf, l0)
    m2 = jnp.max(l1, axis=-1, keepdims=True)
    i2 = jnp.min(jnp.where(l1 == m2, lane, big), axis=-1, keepdims=True)
    e2 = jnp.exp(m2 - m1)
    den = 1.0 + e2
    hit = ((lane == i1) | (lane == i2)).astype(F32)
    earlier = (lax.broadcasted_iota(jnp.int32, (tm, tm), 0) > lax.broadcasted_iota(jnp.int32, (tm, tm), 1))
    rank = jnp.dot(earlier.astype(BF16), hit.astype(BF16), preferred_element_type=F32) + carry[...]
    r1 = jnp.sum(jnp.where(lane == i1, rank, 0.0), axis=-1, keepdims=True)
    r2 = jnp.sum(jnp.where(lane == i2, rank, 0.0), axis=-1, keepdims=True)
    meta = jnp.zeros_like(logits)
    for k, v in ((META_E1, i1), (META_E2, i2), (META_W1, 1.0 / den), (META_W2, e2 / den), (META_R1, r1),
                 (META_R2, r2)):
        meta = jnp.where(lane == k, v, meta)
    meta_ref[...] = meta
    carry[...] += jnp.sum(hit, axis=0, keepdims=True)
    cnt_ref[...] = carry[...]


def _router(h, w_router_padded, j, *, n_exp, tm):
    R, D = h.shape
    NP = w_router_padded.shape[-1]
    return pl.pallas_call(
        functools.partial(_router_kernel, n_exp=n_exp),
        grid=(R // tm,),
        in_specs=[pl.BlockSpec((tm, D), lambda i: (i, 0)),
                  pl.BlockSpec((None, D, NP), lambda i: (j, 0, 0))],
        out_specs=[pl.BlockSpec((tm, NP), lambda i: (i, 0)), pl.BlockSpec((1, NP), lambda i: (0, 0))],
        out_shape=[jax.ShapeDtypeStruct((R, NP), F32), jax.ShapeDtypeStruct((1, NP), F32)],
        scratch_shapes=[pltpu.VMEM((1, NP), F32)],
        compiler_params=_params("arbitrary"),
        name="router",
    )(h, w_router_padded)


def _row_copies_wait(src_like, dst_like, sem, n):
    for _ in range(n):
        pltpu.make_async_copy(src_like, dst_like, sem).wait()


def _dispatch_kernel(p1_ref, p2_ref, h_ref, xs_ref, sem):
    tm = h_ref.shape[0]
    i0 = pl.program_id(0) * tm

    def body(i, c):
        pltpu.make_async_copy(h_ref.at[i], xs_ref.at[p1_ref[i0 + i]], sem).start()
        pltpu.make_async_copy(h_ref.at[i], xs_ref.at[p2_ref[i0 + i]], sem).start()
        return c

    lax.fori_loop(0, tm, body, 0)
    _row_copies_wait(h_ref, xs_ref.at[pl.ds(0, tm)], sem, TOP_K)


def _dispatch(h_rows, p1, p2, *, n_slots, tm):
    R, C, _ = h_rows.shape
    return pl.pallas_call(
        _dispatch_kernel,
        grid_spec=pltpu.PrefetchScalarGridSpec(
            num_scalar_prefetch=2, grid=(R // tm,),
            in_specs=[pl.BlockSpec((tm, C, LANES), lambda i, p1, p2: (i, 0, 0))],
            out_specs=pl.BlockSpec(memory_space=pl.ANY),
            scratch_shapes=[pltpu.SemaphoreType.DMA]),
        out_shape=jax.ShapeDtypeStruct((n_slots, C, LANES), F32),
        compiler_params=_params("arbitrary"),
        name="moe_dispatch",
    )(p1, p2, h_rows)


def _per_live_rows(nv, tile, sub, body):
    for m in range(sub, tile + 1, sub):
        pl.when((nv > m - sub) & (nv <= m))(functools.partial(body, m))


def _moe_glu_kernel(te_ref, nv_ref, nu_ref, xs_ref, wg_ref, wu_ref, a_ref, xb_s, *, sub):
    t = pl.program_id(0)
    j = pl.program_id(1)
    nv = nv_ref[t]
    tile, nchunk = xs_ref.shape[0], xs_ref.shape[1]

    @pl.when(t < nu_ref[0])
    def _():
        wg = wg_ref[...].astype(BF16)
        wu = wu_ref[...].astype(BF16)

        def body(m):
            rows = slice(0, m)

            @pl.when(j == 0)
            def _():
                live = lax.broadcasted_iota(jnp.int32, (m, 1), 0) < nv
                for c in range(nchunk):
                    xb_s[rows, c * LANES:(c + 1) * LANES] = jnp.where(live, xs_ref[rows, c, :], 0.0).astype(BF16)

            x = xb_s[rows, :]
            g = jnp.dot(x, wg, preferred_element_type=F32)
            u = jnp.dot(x, wu, preferred_element_type=F32)
            a_ref[rows, :] = (jax.nn.silu(g) * u).astype(a_ref.dtype)
            if m < tile:
                a_ref[m:tile, :] = jnp.zeros((tile - m, a_ref.shape[1]), a_ref.dtype)

        _per_live_rows(nv, tile, sub, body)


def _moe_down_kernel(te_ref, nv_ref, nu_ref, a_ref, wd_ref, y_ref, acc_s, *, sub, nk):
    t = pl.program_id(0)
    k = pl.program_id(1)
    nv = nv_ref[t]
    tile, nchunk = y_ref.shape[0], y_ref.shape[1]

    @pl.when(t < nu_ref[0])
    def _():
        wd = wd_ref[...].astype(BF16)

        def body(m):
            rows = slice(0, m)
            part = jnp.dot(a_ref[rows, :], wd, preferred_element_type=F32)

            @pl.when(k == 0)
            def _():
                acc_s[rows, :] = part

            @pl.when(k > 0)
            def _():
                acc_s[rows, :] += part

            @pl.when(k == nk - 1)
            def _():
                for c in range(nchunk):
                    y_ref[rows, c, :] = acc_s[rows, c * LANES:(c + 1) * LANES]
                if m < tile:
                    y_ref[m:tile, :, :] = jnp.zeros((tile - m, nchunk, LANES), F32)

        _per_live_rows(nv, tile, sub, body)


def _moe_ffn(xs, w_gate, w_up, w_down, l0, tile_expert, tile_valid, n_used, *, tile, sub, tn, tk):
    n_slots, C, _ = xs.shape
    D = C * LANES
    F = w_gate.shape[-1]
    nf, nk = F // tn, F // tk
    nt = n_slots // tile

    def live_t(t, nu):
        return jnp.minimum(t, nu[0] - 1)

    def live_step(t, j, nu, n):
        return jnp.where(t < nu[0], j, n - 1)

    def expert(t, te, nu):
        return l0 + te[live_t(t, nu)]

    row_tile = pl.BlockSpec((tile, C, LANES), lambda t, j, te, nv, nu: (live_t(t, nu), 0, 0))
    w_col = pl.BlockSpec((None, D, tn), lambda t, j, te, nv, nu: (expert(t, te, nu), 0, live_step(t, j, nu, nf)))
    a = pl.pallas_call(
        functools.partial(_moe_glu_kernel, sub=sub),
        grid_spec=pltpu.PrefetchScalarGridSpec(
            num_scalar_prefetch=3, grid=(nt, nf),
            in_specs=[row_tile, w_col, w_col],
            out_specs=pl.BlockSpec((tile, tn), lambda t, j, te, nv, nu: (live_t(t, nu), live_step(t, j, nu, nf))),
            scratch_shapes=[pltpu.VMEM((tile, D), BF16)]),
        out_shape=jax.ShapeDtypeStruct((n_slots, F), BF16),
        compiler_params=_params("arbitrary", "arbitrary"),
        name="moe_glu",
    )(tile_expert, tile_valid, n_used, xs, w_gate, w_up)
    return pl.pallas_call(
        functools.partial(_moe_down_kernel, sub=sub, nk=nk),
        grid_spec=pltpu.PrefetchScalarGridSpec(
            num_scalar_prefetch=3, grid=(nt, nk),
            in_specs=[pl.BlockSpec((tile, tk), lambda t, k, te, nv, nu: (live_t(t, nu), live_step(t, k, nu, nk))),
                      pl.BlockSpec((None, tk, D),
                                   lambda t, k, te, nv, nu: (expert(t, te, nu), live_step(t, k, nu, nk), 0))],
            out_specs=row_tile,
            scratch_shapes=[pltpu.VMEM((tile, D), F32)]),
        out_shape=jax.ShapeDtypeStruct((n_slots, C, LANES), F32),
        compiler_params=_params("arbitrary", "arbitrary"),
        name="moe_down",
    )(tile_expert, tile_valid, n_used, a, w_down)


def _combine_kernel(p1_ref, p2_ref, meta_ref, y_ref, f_ref, a_s, b_s, sem):
    tm, nchunk = a_s.shape[0], a_s.shape[1]
    i0 = pl.program_id(0) * tm

    def body(i, c):
        pltpu.make_async_copy(y_ref.at[p1_ref[i0 + i]], a_s.at[i], sem).start()
        pltpu.make_async_copy(y_ref.at[p2_ref[i0 + i]], b_s.at[i], sem).start()
        return c

    lax.fori_loop(0, tm, body, 0)
    _row_copies_wait(y_ref.at[pl.ds(0, tm)], a_s, sem, TOP_K)
    w1 = meta_ref[:, META_W1:META_W1 + 1]
    w2 = meta_ref[:, META_W2:META_W2 + 1]
    for c in range(nchunk):
        f_ref[:, c * LANES:(c + 1) * LANES] = w1 * a_s[:, c, :] + w2 * b_s[:, c, :]


def _combine(y, meta, p1, p2, *, tm):
    _, C, _ = y.shape
    R, NP = meta.shape
    return pl.pallas_call(
        _combine_kernel,
        grid_spec=pltpu.PrefetchScalarGridSpec(
            num_scalar_prefetch=2, grid=(R // tm,),
            in_specs=[pl.BlockSpec((tm, NP), lambda i, p1, p2: (i, 0)),
                      pl.BlockSpec(memory_space=pl.ANY)],
            out_specs=pl.BlockSpec((tm, C * LANES), lambda i, p1, p2: (i, 0)),
            scratch_shapes=[pltpu.VMEM((tm, C, LANES), F32), pltpu.VMEM((tm, C, LANES), F32),
                            pltpu.SemaphoreType.DMA]),
        out_shape=jax.ShapeDtypeStruct((R, C * LANES), F32),
        compiler_params=_params("arbitrary"),
        name="moe_combine",
    )(p1, p2, meta, y)


def _moe(h, h_rows, w_router_padded, w_gate, w_up, w_down, j, *, n_exp, tm_router, tile, sub, tn, tk, tm_rows):
    R = h.shape[0]
    meta, counts = _router(h, w_router_padded, j, n_exp=n_exp, tm=tm_router)
    cnt = counts[0, :n_exp].astype(jnp.int32)
    ntile = (cnt + tile - 1) // tile
    tend = jnp.cumsum(ntile)
    tstart = tend - ntile
    n_tiles = -(-TOP_K * R // tile) + n_exp
    e1, e2 = meta[:, META_E1].astype(jnp.int32), meta[:, META_E2].astype(jnp.int32)
    p1 = tstart[e1] * tile + meta[:, META_R1].astype(jnp.int32)
    p2 = tstart[e2] * tile + meta[:, META_R2].astype(jnp.int32)
    tid = jnp.arange(n_tiles, dtype=jnp.int32)
    tile_expert = jnp.minimum(jnp.sum(tid[:, None] >= tend[None, :], axis=1), n_exp - 1).astype(jnp.int32)
    tile_valid = jnp.clip(cnt[tile_expert] - (tid - tstart[tile_expert]) * tile, 0, tile)
    tile_valid = jnp.where(tid < tend[-1], tile_valid, 0).astype(jnp.int32)
    xs = _dispatch(h_rows, p1, p2, n_slots=n_tiles * tile, tm=tm_rows)
    y = _moe_ffn(xs, w_gate, w_up, w_down, j * n_exp, tile_expert, tile_valid, tend[-1:].astype(jnp.int32),
                 tile=tile, sub=sub, tn=tn, tk=tk)
    return _combine(y, meta, p1, p2, tm=tm_rows)


def kernel(x_prompt, x_sample, cache_k_win, cache_v_win, state_rglru_h, state_conv, c_prompt, c_sample, norm1_w, norm2_w, w_ada, b_ada, w_in, q_norm_w, k_norm_w, attn_sinks, conv_w, conv_b, lru_w_a, lru_b_a, lru_w_x, lru_b_x, lru_lambda, chunk_v_norm_w, chunk_w_s, chunk_b_s, w_branch_a, w_branch_b, w_branch_c, w_out, ffn_w_gate, ffn_w_up, ffn_w_down, moe_w_router, moe_w_gate, moe_w_up, moe_w_down):
    Bp, S, D = x_prompt.shape
    Bs, Ts, _ = x_sample.shape
    L = w_in.shape[0]
    window, n_kv, hd = cache_k_win.shape[2:]
    n_q = attn_sinks.shape[1]
    wa, kvw = n_q * hd, n_kv * hd
    wb = conv_w.shape[-1]
    wc = chunk_v_norm_w.shape[-1]
    chunk = chunk_w_s.shape[-1]
    cwid = conv_w.shape[1]
    n_exp = moe_w_router.shape[-1]
    assert Ts == SUBLANES and S % (Bs * Ts) == 0 and S % chunk == 0 and S % window == 0
    Rp, Rs = Bp * S, Bs * Ts
    R = Rp + Rs
    half = wb // 2
    assert wb == wc and wa % half == 0 and kvw % half == 0 or True
    col_lx = (wa + 2 * kvw) // half
    col_lg = col_lx + 2
    col_cu = col_lg + 2
    col_cv = col_cu + 2
    gate_col = wa + 2 * kvw + 2 * wb + 2 * wc

    W = dict(conv_w=conv_w, conv_b=conv_b, lru_w_a=lru_w_a, lru_b_a=lru_b_a, lru_w_x=lru_w_x, lru_b_x=lru_b_x,
             lru_lambda=lru_lambda, chunk_v_norm_w=chunk_v_norm_w, chunk_w_s=chunk_w_s, chunk_b_s=chunk_b_s,
             w_branch_a=w_branch_a, w_branch_b=w_branch_b, w_branch_c=w_branch_c)

    n_c = Bp + Bs
    n_c_pad = -(-n_c // SUBLANES) * SUBLANES
    c_all = jnp.concatenate([c_prompt, c_sample, jnp.zeros((n_c_pad - n_c, D), F32)], axis=0)
    mod = _ada(c_all, w_ada, b_ada)
    mod_p = jnp.repeat(mod[:, :Bp], SUBLANES, axis=1)
    mod_s = jnp.repeat(mod[:, Bp:n_c], Ts, axis=1)

    x = jnp.concatenate([x_prompt.reshape(Rp, D), x_sample.reshape(Rs, D)], axis=0)
    rn = functools.partial(_resid_norm, seq=S, n_prompt=Bp)
    tm_big = _pick(R, 1408, 128) if R % 128 == 0 else _pick(R, 1408, 16)
    tm_mid = _pick(R, 768, 128) if R % 128 == 0 else _pick(R, 768, 16)
    w_router_p = jnp.pad(moe_w_router, ((0, 0), (0, 0), (0, LANES - n_exp)))
    moe_gate_w = moe_w_gate.reshape((-1,) + moe_w_gate.shape[2:])
    moe_up_w = moe_w_up.reshape((-1,) + moe_w_up.shape[2:])
    moe_down_w = moe_w_down.reshape((-1,) + moe_w_down.shape[2:])
    moe_sub = MOE_SUB_ROWS if TOP_K * R >= 2 * n_exp * MOE_SUB_ROWS * MOE_TILE_SUBS else 4 * SUBLANES

    buf_rows = jnp.pad(state_conv, ((0, 0), (0, 0), (Ts - (cwid - 1), 0), (0, 0))).reshape(L, Rs, wb)
    h0_rep = jnp.repeat(state_rglru_h, Ts, axis=1)

    _, h = rn(x, None, mod_p, mod_s, norm1_w, 0, gate_k=None, scale_k=(0, 1), shift_k=(0, 0))
    outs = []
    for l in range(L):
        z = _matmul(h, w_in, l, out_dtype=F32, tm=tm_big, tn=_pick(w_in.shape[-1], 512, half), tk=D)
        attn_o, kwp, vwp = _attn_prompt(z, attn_sinks, q_norm_w, k_norm_w, l, R=R, n_prompt=Bp, seq=S,
                                        n_q=n_q, n_kv=n_kv, hd=hd, window=window)
        attn_o, kws, vws = _attn_sample(attn_o, z, cache_k_win, cache_v_win, attn_sinks, q_norm_w, k_norm_w, l,
                                        n_sample=Bs, T=Ts, n_q=n_q, n_kv=n_kv, hd=hd, window=window)
        lru_o, hp, cp8 = _lru_prompt(z, W, l, R=R, nseq=Bp, T=S, Tb=_pick(S, 256, SUBLANES),
                                     col_x=col_lx, col_g=col_lg, wb=wb)
        lru_o, hs_all = _lru_sample(lru_o, z, h0_rep[l], buf_rows[l], W, l, row0=Rp, T=Ts,
                                    col_x=col_lx, col_g=col_lg, wb=wb)
        chunk_o, vnp = _chunk(None, z, W, l, R=R, row0=0, nsteps=Rp // chunk, nb=1, Tc=chunk,
                              cps=S // chunk, col_u=col_cu, col_v=col_cv, wc=wc)
        chunk_o, vns = _chunk(chunk_o, z, W, l, R=R, row0=Rp, nsteps=1, nb=Bs, Tc=Ts, cps=1,
                              col_u=col_cu, col_v=col_cv, wc=wc)
        m = _merge(attn_o, lru_o, chunk_o, z, W, l, gate_col=gate_col, tm=tm_big, tn=_pick(D, 256, LANES))
        y = _matmul(m, w_out, l, out_dtype=F32, tm=tm_big, tn=_pick(D, 512, LANES), tk=D)
        j = l // 2
        if l % 2 == 0:
            x, h = rn(x, y, mod_p, mod_s, norm2_w, l, gate_k=(l, 2), scale_k=(l, 4), shift_k=(l, 3))
            a = _glu(h, ffn_w_gate, ffn_w_up, j, 1, None, tm=tm_big, tn=_pick(ffn_w_gate.shape[-1], 512, LANES))
            f = _matmul(a, ffn_w_down, j, out_dtype=F32, tm=tm_big, tn=_pick(D, 1024, LANES),
                        tk=_pick(a.shape[1], 1408, LANES))
        else:
            x, h, h_rows = rn(x, y, mod_p, mod_s, norm2_w, l, gate_k=(l, 2), scale_k=(l, 4), shift_k=(l, 3),
                              rows_out=True)
            f = _moe(h, h_rows, w_router_p, moe_gate_w, moe_up_w, moe_down_w, j, n_exp=n_exp, tm_router=tm_mid,
                     tile=MOE_TILE_SUBS * moe_sub, sub=moe_sub, tn=_pick(moe_gate_w.shape[-1], 512, LANES),
                     tk=_pick(moe_gate_w.shape[-1], 512, LANES), tm_rows=Rs)
        if l + 1 < L:
            x, h = rn(x, f, mod_p, mod_s, norm1_w, l + 1, gate_k=(l, 5), scale_k=(l + 1, 1), shift_k=(l + 1, 0))
        else:
            x, _ = rn(x, f, mod_p, mod_s, None, l, gate_k=(l, 5), scale_k=None, shift_k=None)
        conv_p = cp8[:, SUBLANES - (cwid - 1):, :]
        lx_s = z[Rp:, col_lx * half:col_lx * half + wb].reshape(Bs, Ts, wb)
        conv_s = jnp.concatenate([state_conv[l], lx_s], axis=1)[:, -(cwid - 1):]
        hs = hs_all.reshape(Bs, Ts, wb)[:, Ts - 1]
        outs.append((kwp.reshape(Bp, window, n_kv, hd), vwp.reshape(Bp, window, n_kv, hd), hp.reshape(Bp, wb),
                     conv_p, vnp,
                     kws.reshape(Bs, window, n_kv, hd), vws.reshape(Bs, window, n_kv, hd), hs.reshape(Bs, wb),
                     conv_s, vns.reshape(Bs, Ts, wc)))
    st = [jnp.stack(s) for s in zip(*outs)]
    y_prompt = x[:Rp].reshape(Bp, S, D)
    y_sample = x[Rp:].reshape(Bs, Ts, D)
    return (y_prompt, y_sample, st[0], st[1], st[2], st[3], st[4], st[5], st[6], st[7], st[8], st[9])
```

```python
import functools

import jax
import jax.numpy as jnp
from jax import lax
from jax.experimental import pallas as pl
from jax.experimental.pallas import tpu as pltpu

EPS = 1e-6
LRU_C = 8.0
PAST_LEN = 16384
TOP_K = 2
MOE_SUB_ROWS = 384
MOE_TILE_SUBS = 3
SUBLANES = 8
LANES = 128
V7X_VMEM_LIMIT = 56 * 1024 * 1024

F32 = jnp.float32
BF16 = jnp.bfloat16


def _pick(n, target, mult):
    best = None
    for t in range(mult, min(n, target) + 1, mult):
        if n % t == 0:
            best = t
    assert best is not None, (n, target, mult)
    return best


def _params(*sem):
    return pltpu.CompilerParams(dimension_semantics=sem, vmem_limit_bytes=V7X_VMEM_LIMIT)


def _rms(x, w):
    return x * lax.rsqrt(jnp.mean(x * x, axis=-1, keepdims=True) + EPS) * w


def _ada_kernel(c_ref, w_ref, b_ref, o_ref):
    a = jax.nn.silu(c_ref[...]).astype(BF16)
    o_ref[...] = jnp.dot(a, w_ref[...].astype(BF16), preferred_element_type=F32) + b_ref[...]


def _ada(c_all, w_ada, b_ada):
    L, D, W6 = w_ada.shape
    Bc = c_all.shape[0]
    tn = _pick(W6, 1024, LANES)
    return pl.pallas_call(
        _ada_kernel,
        grid=(L, W6 // tn),
        in_specs=[pl.BlockSpec((Bc, D), lambda l, j: (0, 0)),
                  pl.BlockSpec((None, D, tn), lambda l, j: (l, 0, j)),
                  pl.BlockSpec((None, 1, tn), lambda l, j: (l, 0, j))],
        out_specs=pl.BlockSpec((None, Bc, tn), lambda l, j: (l, 0, j)),
        out_shape=jax.ShapeDtypeStruct((L, Bc, W6), F32),
        compiler_params=_params("parallel", "parallel"),
        name="ada_mod",
    )(c_all, w_ada, b_ada.reshape(L, 1, W6))


def _resid_norm_kernel(*refs, n_prompt_tiles, has_resid, has_norm, has_rows):
    refs = list(refs)
    x_ref = refs.pop(0)
    if has_resid:
        y_ref, gp_ref, gs_ref = refs.pop(0), refs.pop(0), refs.pop(0)
    if has_norm:
        scp_ref, scs_ref, shp_ref, shs_ref, nw_ref = (refs.pop(0) for _ in range(5))
    if has_resid:
        xo_ref = refs.pop(0)
    if has_norm:
        h_ref = refs.pop(0)
    if has_rows:
        hrow_ref = refs.pop(0)
    i = pl.program_id(0)

    def body(prompt):
        mod = (lambda p, s: p[0:1, :]) if prompt else (lambda p, s: s[...])
        x = x_ref[...]
        if has_resid:
            x = x + mod(gp_ref, gs_ref) * y_ref[...]
            xo_ref[...] = x
        if has_norm:
            hn = _rms(x, nw_ref[...])
            h = hn * (1.0 + mod(scp_ref, scs_ref)) + mod(shp_ref, shs_ref)
            h_ref[...] = h.astype(h_ref.dtype)
            if has_rows:
                for c in range(hrow_ref.shape[1]):
                    hrow_ref[:, c, :] = h[:, c * LANES:(c + 1) * LANES]

    pl.when(i < n_prompt_tiles)(lambda: body(True))
    pl.when(i >= n_prompt_tiles)(lambda: body(False))


def _resid_norm(x, y, mod_p, mod_s, norm_w, l, *, gate_k, scale_k, shift_k, seq, n_prompt, rows_out=False):
    R, D = x.shape
    TE = mod_s.shape[1]
    tpb = seq // TE
    npt = n_prompt * tpb
    has_resid = y is not None
    has_norm = norm_w is not None
    row = pl.BlockSpec((TE, D), lambda i: (i, 0))

    def mp(lk):
        return pl.BlockSpec((None, SUBLANES, D), lambda i: (lk[0], jnp.minimum(i // tpb, n_prompt - 1), lk[1]))

    def ms(lk):
        return pl.BlockSpec((None, TE, D), lambda i: (lk[0], 0, lk[1]))

    args, specs, outs, out_specs = [x], [row], [], []
    if has_resid:
        args += [y, mod_p, mod_s]
        specs += [row, mp(gate_k), ms(gate_k)]
        outs.append(jax.ShapeDtypeStruct((R, D), F32))
        out_specs.append(row)
    if has_norm:
        args += [mod_p, mod_s, mod_p, mod_s, norm_w.reshape(norm_w.shape[0], 1, D)]
        specs += [mp(scale_k), ms(scale_k), mp(shift_k), ms(shift_k),
                  pl.BlockSpec((None, 1, D), lambda i: (l, 0, 0))]
        outs.append(jax.ShapeDtypeStruct((R, D), BF16))
        out_specs.append(row)
    if rows_out:
        outs.append(jax.ShapeDtypeStruct((R, D // LANES, LANES), F32))
        out_specs.append(pl.BlockSpec((TE, D // LANES, LANES), lambda i: (i, 0, 0)))
    res = pl.pallas_call(
        functools.partial(_resid_norm_kernel, n_prompt_tiles=npt, has_resid=has_resid, has_norm=has_norm,
                          has_rows=rows_out),
        grid=(R // TE,),
        in_specs=specs, out_specs=out_specs, out_shape=outs,
        compiler_params=_params("parallel"),
        name="resid_norm",
    )(*args)
    res = list(res)
    x_new = res.pop(0) if has_resid else x
    h = res.pop(0) if has_norm else None
    if rows_out:
        return x_new, h, res.pop(0)
    return x_new, h


def _mm_kernel(a_ref, w_ref, o_ref, acc_ref, *, nk):
    part = jnp.dot(a_ref[...], w_ref[...].astype(BF16), preferred_element_type=F32)
    if nk == 1:
        o_ref[...] = part.astype(o_ref.dtype)
        return
    k = pl.program_id(2)

    @pl.when(k == 0)
    def _():
        acc_ref[...] = part

    @pl.when(k > 0)
    def _():
        acc_ref[...] += part

    @pl.when(k == nk - 1)
    def _():
        o_ref[...] = acc_ref[...].astype(o_ref.dtype)


def _matmul(a, w, l, *, out_dtype, tm, tn, tk):
    R, K = a.shape
    N = w.shape[-1]
    nk = K // tk
    acc_shape = (tm, tn) if nk > 1 else (SUBLANES, LANES)
    return pl.pallas_call(
        functools.partial(_mm_kernel, nk=nk),
        grid=(R // tm, N // tn, nk),
        in_specs=[pl.BlockSpec((tm, tk), lambda i, j, k: (i, k)),
                  pl.BlockSpec((None, tk, tn), lambda i, j, k: (l, k, j))],
        out_specs=pl.BlockSpec((tm, tn), lambda i, j, k: (i, j)),
        out_shape=jax.ShapeDtypeStruct((R, N), out_dtype),
        scratch_shapes=[pltpu.VMEM(acc_shape, F32)],
        compiler_params=_params("parallel", "parallel", "arbitrary"),
        name="matmul",
    )(a, w)


def _softmax_sink_pv(s, sink, v):
    m = jnp.maximum(jnp.max(s, axis=-1, keepdims=True), sink)
    e = jnp.exp(s - m)
    den = jnp.sum(e, axis=-1, keepdims=True) + jnp.exp(sink - m)
    return jnp.dot((e / den).astype(BF16), v, preferred_element_type=F32)


def _attn_prompt_kernel(sink_ref, q_ref, kp_ref, kc_ref, vp_ref, vc_ref, qw_ref, kw_ref,
                        o_ref, kwin_ref, vwin_ref, *, l, n_kv, q_per_kv, hd, window, nblk):
    i = pl.program_id(1)
    blk = q_ref.shape[0]
    scale = hd ** -0.5
    rows_q = q_per_kv * blk
    r = lax.broadcasted_iota(jnp.int32, (rows_q, 2 * blk), 0) % blk
    c = lax.broadcasted_iota(jnp.int32, (rows_q, 2 * blk), 1)
    dist = r + blk - c
    ok = (dist >= 0) & (dist <= window) & ((c >= blk) | (i > 0))
    hrow = lax.broadcasted_iota(jnp.int32, (rows_q, 1), 0) // blk
    qw = qw_ref[...]
    kw = kw_ref[...]
    for g in range(n_kv):
        sl = slice(g * hd, (g + 1) * hd)
        kcn = _rms(kc_ref[:, sl], kw)
        kcat = jnp.concatenate([_rms(kp_ref[:, sl], kw), kcn], axis=0).astype(BF16)
        vcat = jnp.concatenate([vp_ref[:, sl], vc_ref[:, sl]], axis=0).astype(BF16)
        heads = [g * q_per_kv + h for h in range(q_per_kv)]
        qs = jnp.concatenate([_rms(q_ref[:, hh * hd:(hh + 1) * hd], qw) for hh in heads], axis=0).astype(BF16)
        s = lax.dot_general(qs, kcat, (((1,), (1,)), ((), ())), preferred_element_type=F32) * scale
        s = jnp.where(ok, s, -1e30)
        sink = jnp.zeros((rows_q, 1), F32)
        for h, hh in enumerate(heads):
            sink = jnp.where(hrow == h, sink_ref[l, hh], sink)
        o = _softmax_sink_pv(s, sink, vcat)
        for h, hh in enumerate(heads):
            o_ref[:, hh * hd:(hh + 1) * hd] = o[h * blk:(h + 1) * blk, :].astype(o_ref.dtype)

        @pl.when(i == nblk - 1)
        def _():
            kwin_ref[:, sl] = kcn
            vwin_ref[:, sl] = vc_ref[:, sl]


def _attn_prompt(z, sinks, q_norm_w, k_norm_w, l, *, R, n_prompt, seq, n_q, n_kv, hd, window):
    blk = window
    nblk = seq // blk
    wa = n_q * hd
    kvw = n_kv * hd
    kcol = wa // kvw
    vcol = kcol + 1

    def cur(col):
        return pl.BlockSpec((blk, kvw), lambda b, i: (b * nblk + i, col))

    def prev(col):
        return pl.BlockSpec((blk, kvw), lambda b, i: (b * nblk + jnp.maximum(i - 1, 0), col))

    L = q_norm_w.shape[0]
    nw = pl.BlockSpec((None, 1, hd), lambda b, i: (l, 0, 0))
    return pl.pallas_call(
        functools.partial(_attn_prompt_kernel, l=l, n_kv=n_kv, q_per_kv=n_q // n_kv, hd=hd,
                          window=window, nblk=nblk),
        grid=(n_prompt, nblk),
        in_specs=[pl.BlockSpec(memory_space=pltpu.SMEM),
                  pl.BlockSpec((blk, wa), lambda b, i: (b * nblk + i, 0)),
                  prev(kcol), cur(kcol), prev(vcol), cur(vcol), nw, nw],
        out_specs=[pl.BlockSpec((blk, wa), lambda b, i: (b * nblk + i, 0)),
                   pl.BlockSpec((None, blk, kvw), lambda b, i: (b, 0, 0)),
                   pl.BlockSpec((None, blk, kvw), lambda b, i: (b, 0, 0))],
        out_shape=[jax.ShapeDtypeStruct((R, wa), BF16),
                   jax.ShapeDtypeStruct((n_prompt, blk, kvw), F32),
                   jax.ShapeDtypeStruct((n_prompt, blk, kvw), F32)],
        compiler_params=_params("parallel", "arbitrary"),
        name="attn_prompt",
    )(sinks, z, z, z, z, z, q_norm_w.reshape(L, 1, hd), k_norm_w.reshape(L, 1, hd))


def _attn_sample_kernel(sink_ref, o_in_ref, q_ref, kn_ref, vn_ref, ck_ref, cv_ref, qw_ref, kw_ref,
                        o_ref, kwin_ref, vwin_ref, keys_s, vals_s, ostage,
                        *, l, nb, T, n_kv, q_per_kv, hd, window):
    del o_in_ref
    W = ck_ref.shape[1]
    scale = hd ** -0.5
    rows_q = q_per_kv * T
    r = lax.broadcasted_iota(jnp.int32, (rows_q, 2 * W), 0) % T
    c = lax.broadcasted_iota(jnp.int32, (rows_q, 2 * W), 1)
    q_pos = PAST_LEN + r
    k_pos = jnp.where(c < W, PAST_LEN - W + c, PAST_LEN + c - W)
    dist = q_pos - k_pos
    ok = (dist >= 0) & (dist <= window) & (k_pos >= 0) & (c < W + T)
    hrow = lax.broadcasted_iota(jnp.int32, (rows_q, 1), 0) // T
    qw = qw_ref[...]
    kw = kw_ref[...]
    keys_s[...] = jnp.zeros_like(keys_s)
    vals_s[...] = jnp.zeros_like(vals_s)

    def step(b, carry):
        rows = pl.ds(pl.multiple_of(b * T, T), T)
        for g in range(n_kv):
            sl = slice(g * hd, (g + 1) * hd)
            ck = ck_ref[b, :, sl]
            cv = cv_ref[b, :, sl]
            knn = _rms(kn_ref[rows, sl], kw)
            vnn = vn_ref[rows, sl]
            keys_s[0:W, :] = ck
            keys_s[W:W + T, :] = knn
            vals_s[0:W, :] = cv
            vals_s[W:W + T, :] = vnn
            qs = jnp.concatenate(
                [_rms(q_ref[rows, (g * q_per_kv + h) * hd:(g * q_per_kv + h + 1) * hd], qw)
                 for h in range(q_per_kv)], axis=0).astype(BF16)
            s = lax.dot_general(qs, keys_s[...].astype(BF16), (((1,), (1,)), ((), ())),
                                preferred_element_type=F32) * scale
            s = jnp.where(ok, s, -1e30)
            sink = jnp.zeros((rows_q, 1), F32)
            for h in range(q_per_kv):
                sink = jnp.where(hrow == h, sink_ref[l, g * q_per_kv + h], sink)
            o = _softmax_sink_pv(s, sink, vals_s[...].astype(BF16))
            for h in range(q_per_kv):
                head = g * q_per_kv + h
                ostage[rows, head * hd:(head + 1) * hd] = o[h * T:(h + 1) * T, :]
            kwin_ref[b, 0:W - T, sl] = ck[T:, :]
            kwin_ref[b, W - T:W, sl] = knn
            vwin_ref[b, 0:W - T, sl] = cv[T:, :]
            vwin_ref[b, W - T:W, sl] = vnn
        return carry

    lax.fori_loop(0, nb, step, 0)
    o_ref[...] = ostage[...].astype(o_ref.dtype)


def _attn_sample(attn_o, z, cache_k, cache_v, sinks, q_norm_w, k_norm_w, l, *, n_sample, T, n_q, n_kv, hd,
                 window):
    R, wa = attn_o.shape
    kvw = n_kv * hd
    rows = n_sample * T
    blk_i = (R - rows) // rows
    kcol = wa // kvw
    L, nb, W = cache_k.shape[:3]
    ck = cache_k.reshape(L, nb, W, kvw)
    cv = cache_v.reshape(L, nb, W, kvw)
    nw = pl.BlockSpec((None, 1, hd), lambda i: (l, 0, 0))
    cache = pl.BlockSpec((None, nb, W, kvw), lambda i: (l, 0, 0, 0))
    win = pl.BlockSpec((nb, W, kvw), lambda i: (0, 0, 0))
    return pl.pallas_call(
        functools.partial(_attn_sample_kernel, l=l, nb=nb, T=T, n_kv=n_kv, q_per_kv=n_q // n_kv, hd=hd,
                          window=window),
        grid=(1,),
        in_specs=[pl.BlockSpec(memory_space=pltpu.SMEM),
                  pl.BlockSpec(memory_space=pl.ANY),
                  pl.BlockSpec((rows, wa), lambda i: (blk_i, 0)),
                  pl.BlockSpec((rows, kvw), lambda i: (blk_i, kcol)),
                  pl.BlockSpec((rows, kvw), lambda i: (blk_i, kcol + 1)),
                  cache, cache, nw, nw],
        out_specs=[pl.BlockSpec((rows, wa), lambda i: (blk_i, 0)), win, win],
        out_shape=[jax.ShapeDtypeStruct((R, wa), BF16),
                   jax.ShapeDtypeStruct((nb, W, kvw), F32),
                   jax.ShapeDtypeStruct((nb, W, kvw), F32)],
        scratch_shapes=[pltpu.VMEM((2 * W, hd), F32), pltpu.VMEM((2 * W, hd), F32),
                        pltpu.VMEM((rows, wa), F32)],
        input_output_aliases={1: 0},
        compiler_params=_params("arbitrary"),
        name="attn_sample",
    )(sinks, attn_o, z, z, z, ck, cv, q_norm_w.reshape(-1, 1, hd), k_norm_w.reshape(-1, 1, hd))


def _lru_gates(xc, wa_ref, ba_ref, wx_ref, bx_ref, lam_ref, a_s, u_s, *, nblk, lb):
    xcb = xc.astype(BF16)
    for n in range(nblk):
        sl = slice(n * lb, (n + 1) * lb)
        ra = jnp.dot(xcb[:, sl], wa_ref[n].astype(BF16), preferred_element_type=F32) + ba_ref[:, sl]
        rx = jnp.dot(xcb[:, sl], wx_ref[n].astype(BF16), preferred_element_type=F32) + bx_ref[:, sl]
        log_a = -LRU_C * jax.nn.sigmoid(ra) * jax.nn.softplus(-lam_ref[:, sl])
        a_s[:, sl] = jnp.exp(log_a)
        th = jnp.tanh(log_a)
        u_s[:, sl] = jnp.sqrt(-2.0 * th / (1.0 - th)) * jax.nn.sigmoid(rx) * xc[:, sl]

def _tile_scan(a, u, T):
    t = lax.broadcasted_iota(jnp.int32, (a.shape[0], 1), 0) % T
    d = 1
    while d < T:
        keep = t >= d
        u = jnp.where(keep, a * pltpu.roll(u, d, axis=0) + u, u)
        a = jnp.where(keep, a * pltpu.roll(a, d, axis=0), a)
        d *= 2
    return a, u


def _lru_prompt_kernel(x_ref, g_ref, cw_ref, cb_ref, wa_ref, ba_ref, wx_ref, bx_ref, lam_ref,
                       o_ref, hT_ref, cnew_ref, xp_s, a_s, u_s, h_s, *, Tb, ntb, cw, nblk, lb):
    tb = pl.program_id(2)
    P = SUBLANES

    @pl.when(tb == 0)
    def _():
        xp_s[0:P, :] = jnp.zeros((P, xp_s.shape[1]), F32)
        h_s[...] = jnp.zeros_like(h_s)

    xp_s[P:P + Tb, :] = x_ref[...]
    xc = cb_ref[...]
    for j in range(cw):
        off = P - (cw - 1) + j
        xc = xc + xp_s[off:off + Tb, :] * cw_ref[j:j + 1, :]
    _lru_gates(xc, wa_ref, ba_ref, wx_ref, bx_ref, lam_ref, a_s, u_s, nblk=nblk, lb=lb)

    a, u = _tile_scan(a_s[...], u_s[...], P)
    a_s[...] = a
    u_s[...] = u

    def step(k, h):
        rr = pl.ds(pl.multiple_of(k * P, P), P)
        hs = u_s[rr, :] + a_s[rr, :] * h
        u_s[rr, :] = hs
        return hs[P - 1:P, :]

    h = lax.fori_loop(0, Tb // P, step, h_s[...])
    h_s[...] = h
    o_ref[...] = (jax.nn.gelu(g_ref[...]) * u_s[...]).astype(o_ref.dtype)
    tail = xp_s[Tb:Tb + P, :]
    xp_s[0:P, :] = tail

    @pl.when(tb == ntb - 1)
    def _():
        hT_ref[...] = h
        cnew_ref[...] = tail


def _lru_sample_kernel(o_in_ref, x_ref, g_ref, h0_ref, buf_ref, cw_ref, cb_ref, wa_ref, ba_ref, wx_ref, bx_ref,
                       lam_ref, o_ref, hs_ref, a_s, u_s, *, T, cw, nblk, lb):
    del o_in_ref
    rows = x_ref.shape[0]
    t = lax.broadcasted_iota(jnp.int32, (rows, 1), 0) % T
    x = x_ref[...]
    buf = buf_ref[...]
    xc = cb_ref[...] + x * cw_ref[cw - 1:cw, :]
    for s in range(1, cw):
        xs = jnp.where(t >= s, pltpu.roll(x, s, axis=0), pltpu.roll(buf, rows - (T - s), axis=0))
        xc = xc + xs * cw_ref[cw - 1 - s:cw - s, :]
    _lru_gates(xc, wa_ref, ba_ref, wx_ref, bx_ref, lam_ref, a_s, u_s, nblk=nblk, lb=lb)
    a, u = _tile_scan(a_s[...], u_s[...], T)
    hs = u + a * h0_ref[...]
    hs_ref[...] = hs
    o_ref[...] = (jax.nn.gelu(g_ref[...]) * hs).astype(o_ref.dtype)


def _lru_weight_args(W, l, wb, index):
    ch = wb // 2
    nlb, lb = W['lru_w_a'].shape[1:3]
    nblk = nlb // 2
    cw = W['conv_w'].shape[1]

    def vec(name):
        return W[name].reshape(W[name].shape[0], 1, wb), pl.BlockSpec((None, 1, ch), index(lambda hf: (l, 0, hf)))

    def blkw(name):
        return W[name], pl.BlockSpec((None, nblk, lb, lb), index(lambda hf: (l, hf, 0, 0)))

    pairs = [(W['conv_w'], pl.BlockSpec((None, cw, ch), index(lambda hf: (l, 0, hf)))),
             vec('conv_b'), blkw('lru_w_a'), vec('lru_b_a'), blkw('lru_w_x'), vec('lru_b_x'), vec('lru_lambda')]
    return [p[0] for p in pairs], [p[1] for p in pairs], dict(cw=cw, nblk=nblk, lb=lb)


def _lru_prompt(z, W, l, *, R, nseq, T, Tb, col_x, col_g, wb):
    ch = wb // 2
    ntb = T // Tb
    wargs, wspecs, kw = _lru_weight_args(W, l, wb, lambda f: (lambda s, hf, t: f(hf)))

    def rowblk(col):
        return pl.BlockSpec((Tb, ch), lambda s, hf, t: (s * ntb + t, col + hf))

    return pl.pallas_call(
        functools.partial(_lru_prompt_kernel, Tb=Tb, ntb=ntb, **kw),
        grid=(nseq, 2, ntb),
        in_specs=[rowblk(col_x), rowblk(col_g)] + wspecs,
        out_specs=[pl.BlockSpec((Tb, ch), lambda s, hf, t: (s * ntb + t, hf)),
                   pl.BlockSpec((None, 1, ch), lambda s, hf, t: (s, 0, hf)),
                   pl.BlockSpec((None, SUBLANES, ch), lambda s, hf, t: (s, 0, hf))],
        out_shape=[jax.ShapeDtypeStruct((R, wb), BF16),
                   jax.ShapeDtypeStruct((nseq, 1, wb), F32),
                   jax.ShapeDtypeStruct((nseq, SUBLANES, wb), F32)],
        scratch_shapes=[pltpu.VMEM((SUBLANES + Tb, ch), F32), pltpu.VMEM((Tb, ch), F32),
                        pltpu.VMEM((Tb, ch), F32), pltpu.VMEM((1, ch), F32)],
        compiler_params=_params("parallel", "parallel", "arbitrary"),
        name="lru_prompt",
    )(z, z, *wargs)


def _lru_sample(lru_o, z, h0_rep, buf_rows, W, l, *, row0, T, col_x, col_g, wb):
    ch = wb // 2
    rows = h0_rep.shape[0]
    rb = row0 // rows
    wargs, wspecs, kw = _lru_weight_args(W, l, wb, lambda f: (lambda hf: f(hf)))
    st = pl.BlockSpec((rows, ch), lambda hf: (0, hf))
    return pl.pallas_call(
        functools.partial(_lru_sample_kernel, T=T, **kw),
        grid=(2,),
        in_specs=[pl.BlockSpec(memory_space=pl.ANY),
                  pl.BlockSpec((rows, ch), lambda hf: (rb, col_x + hf)),
                  pl.BlockSpec((rows, ch), lambda hf: (rb, col_g + hf)), st, st] + wspecs,
        out_specs=[pl.BlockSpec((rows, ch), lambda hf: (rb, hf)), st],
        out_shape=[jax.ShapeDtypeStruct(lru_o.shape, BF16), jax.ShapeDtypeStruct((rows, wb), F32)],
        scratch_shapes=[pltpu.VMEM((rows, ch), F32), pltpu.VMEM((rows, ch), F32)],
        input_output_aliases={0: 0},
        compiler_params=_params("parallel"),
        name="lru_sample",
    )(lru_o, z, z, h0_rep, buf_rows, *wargs)


def _chunk_kernel(*refs, nb, Tc, ngroups, gw, cps, aliased):
    refs = list(refs)
    if aliased:
        refs.pop(0)
    ulo_ref, uhi_ref, vlo_ref, vhi_ref, nw_ref, ws_ref, bst_ref, o_ref, vn_ref = refs
    rows = nb * Tc
    half = vlo_ref.shape[1]
    wc = 2 * half
    v_lo = jax.nn.gelu(vlo_ref[...])
    v_hi = jax.nn.gelu(vhi_ref[...])
    ms = (jnp.sum(v_lo * v_lo, axis=-1, keepdims=True) + jnp.sum(v_hi * v_hi, axis=-1, keepdims=True)) / wc
    inv = lax.rsqrt(ms + EPS)
    vn_halves = (v_lo * inv * nw_ref[:, 0:half], v_hi * inv * nw_ref[:, half:wc])
    u_halves = (ulo_ref, uhi_ref)

    r = lax.broadcasted_iota(jnp.int32, (rows, rows), 0)
    c = lax.broadcasted_iota(jnp.int32, (rows, rows), 1)
    mask = (r // Tc == c // Tc) & (c % Tc <= r % Tc)
    if nb > 1:
        sel = (lax.broadcasted_iota(jnp.int32, (ws_ref.shape[2], rows), 0)
               == lax.broadcasted_iota(jnp.int32, (ws_ref.shape[2], rows), 1) % Tc).astype(BF16)
        bias_rows = jnp.broadcast_to(bst_ref[0:Tc, :][None], (nb, Tc, bst_ref.shape[1])).reshape(rows, -1)
    else:
        bias_rows = bst_ref[0:Tc, :]
    gph = half // gw
    for g in range(ngroups):
        hf, gi = divmod(g, gph)
        sl = slice(gi * gw, (gi + 1) * gw)
        if nb > 1:
            t1 = jnp.broadcast_to(ws_ref[g, 0:Tc, :][None], (nb, Tc, ws_ref.shape[2])).reshape(rows, -1)
            wfull = jnp.dot(t1.astype(BF16), sel, preferred_element_type=F32)
        else:
            wfull = ws_ref[g, 0:Tc, 0:Tc]
        wm = jnp.where(mask, wfull, 0.0).astype(BF16)
        mixed = jnp.dot(wm, vn_halves[hf][:, sl].astype(BF16), preferred_element_type=F32)
        mixed = mixed + bias_rows[:, g:g + 1]
        o_ref[:, g * gw:(g + 1) * gw] = (jax.nn.gelu(u_halves[hf][:, sl]) * mixed).astype(o_ref.dtype)

    if cps == 1:
        vn_ref[:, 0:half] = vn_halves[0]
        vn_ref[:, half:wc] = vn_halves[1]
    else:
        @pl.when(pl.program_id(0) % cps == cps - 1)
        def _():
            vn_ref[:, 0:half] = vn_halves[0]
            vn_ref[:, half:wc] = vn_halves[1]


def _chunk(chunk_o, z, W, l, *, R, row0, nsteps, nb, Tc, cps, col_u, col_v, wc):
    half = wc // 2
    ngroups, chunk = W['chunk_w_s'].shape[1:3]
    gw = wc // ngroups
    rows = nb * Tc
    rb0 = row0 // rows
    aliased = chunk_o is not None

    def rowblk(col):
        return pl.BlockSpec((rows, half), lambda i: (rb0 + i, col))

    args = [z, z, z, z, W['chunk_v_norm_w'].reshape(-1, 1, wc), W['chunk_w_s'],
            jnp.swapaxes(W['chunk_b_s'], 1, 2)]
    specs = [rowblk(col_u), rowblk(col_u + 1), rowblk(col_v), rowblk(col_v + 1),
             pl.BlockSpec((None, 1, wc), lambda i: (l, 0, 0)),
             pl.BlockSpec((None, ngroups, chunk, chunk), lambda i: (l, 0, 0, 0)),
             pl.BlockSpec((None, chunk, ngroups), lambda i: (l, 0, 0))]
    io_alias = {}
    if aliased:
        args.insert(0, chunk_o)
        specs.insert(0, pl.BlockSpec(memory_space=pl.ANY))
        io_alias = {0: 0}
    return pl.pallas_call(
        functools.partial(_chunk_kernel, nb=nb, Tc=Tc, ngroups=ngroups, gw=gw, cps=cps, aliased=aliased),
        grid=(nsteps,),
        in_specs=specs,
        out_specs=[pl.BlockSpec((rows, wc), lambda i: (rb0 + i, 0)),
                   pl.BlockSpec((None, rows, wc), lambda i: (i // cps, 0, 0))],
        out_shape=[jax.ShapeDtypeStruct((R, wc), BF16),
                   jax.ShapeDtypeStruct((nsteps // cps, rows, wc), F32)],
        input_output_aliases=io_alias,
        compiler_params=_params("arbitrary"),
        name="chunk_sample" if aliased else "chunk_prompt",
    )(*args)


def _merge_kernel(a_ref, b_ref, c_ref, wa_ref, wb_ref, wc_ref, ga_ref, gb_ref, gc_ref, o_ref):
    def branch(x_ref, w_ref, g_ref):
        y = jnp.dot(x_ref[...], w_ref[...].astype(BF16), preferred_element_type=F32)
        return jax.nn.sigmoid(g_ref[...]) * y

    m = branch(a_ref, wa_ref, ga_ref) + branch(b_ref, wb_ref, gb_ref) + branch(c_ref, wc_ref, gc_ref)
    o_ref[...] = m.astype(o_ref.dtype)


def _merge(attn_o, lru_o, chunk_o, z, W, l, *, gate_col, tm, tn):
    R = attn_o.shape[0]
    D = W['w_branch_a'].shape[-1]
    g0 = gate_col // tn
    gstep = D // tn

    def xin(a):
        return pl.BlockSpec((tm, a.shape[1]), lambda i, j: (i, 0))

    def win(w):
        return pl.BlockSpec((None, w.shape[1], tn), lambda i, j: (l, 0, j))

    def gin(k):
        return pl.BlockSpec((tm, tn), lambda i, j: (i, g0 + k * gstep + j))

    return pl.pallas_call(
        _merge_kernel,
        grid=(R // tm, D // tn),
        in_specs=[xin(attn_o), xin(lru_o), xin(chunk_o),
                  win(W['w_branch_a']), win(W['w_branch_b']), win(W['w_branch_c']),
                  gin(0), gin(1), gin(2)],
        out_specs=pl.BlockSpec((tm, tn), lambda i, j: (i, j)),
        out_shape=jax.ShapeDtypeStruct((R, D), BF16),
        compiler_params=_params("parallel", "parallel"),
        name="merge",
    )(attn_o, lru_o, chunk_o, W['w_branch_a'], W['w_branch_b'], W['w_branch_c'], z, z, z)


def _glu_kernel(*refs, gated):
    if gated:
        h_ref, wg_ref, wu_ref, gate_ref, o_ref = refs
    else:
        h_ref, wg_ref, wu_ref, o_ref = refs
    h = h_ref[...]
    g = jnp.dot(h, wg_ref[...].astype(BF16), preferred_element_type=F32)
    u = jnp.dot(h, wu_ref[...].astype(BF16), preferred_element_type=F32)
    a = jax.nn.silu(g) * u
    if gated:
        e = pl.program_id(1)
        gate = gate_ref[...]
        lane = lax.broadcasted_iota(jnp.int32, gate.shape, 1)
        a = a * jnp.sum(jnp.where(lane == e, gate, 0.0), axis=-1, keepdims=True)
    o_ref[...] = a.astype(o_ref.dtype)


def _glu(h, w_gate, w_up, l0, n_exp, gate, *, tm, tn):
    R, D = h.shape
    F = w_gate.shape[-1]
    nf = F // tn
    gated = gate is not None
    wspec = pl.BlockSpec((None, D, tn), lambda i, e, j: (l0 + e, 0, j))
    args = [h, w_gate, w_up]
    specs = [pl.BlockSpec((tm, D), lambda i, e, j: (i, 0)), wspec, wspec]
    if gated:
        args.append(gate)
        specs.append(pl.BlockSpec((tm, gate.shape[1]), lambda i, e, j: (i, 0)))
    return pl.pallas_call(
        functools.partial(_glu_kernel, gated=gated),
        grid=(R // tm, n_exp, nf),
        in_specs=specs,
        out_specs=pl.BlockSpec((tm, tn), lambda i, e, j: (i, e * nf + j)),
        out_shape=jax.ShapeDtypeStruct((R, n_exp * F), BF16),
        compiler_params=_params("parallel", "parallel", "parallel"),
        name="glu",
    )(*args)


META_E1, META_E2, META_W1, META_W2, META_R1, META_R2 = range(6)


def _router_kernel(h_ref, w_ref, meta_ref, cnt_ref, carry, *, n_exp):
    i = pl.program_id(0)

    @pl.when(i == 0)
    def _():
        carry[...] = jnp.zeros_like(carry)

    logits = jnp.dot(h_ref[...], w_ref[...].astype(BF16), preferred_element_type=F32)
    tm = logits.shape[0]
    lane = lax.broadcasted_iota(jnp.int32, logits.shape, 1).astype(F32)
    big = float(logits.shape[1])
    l0 = jnp.where(lane < n_exp, logits, -jnp.inf)
    m1 = jnp.max(l0, axis=-1, keepdims=True)
    i1 = jnp.min(jnp.where(l0 == m1, lane, big), axis=-1, keepdims=True)
    l1 = jnp.where(lane == i1, -jnp.inf, l0)
    m2 = jnp.max(l1, axis=-1, keepdims=True)
    i2 = jnp.min(jnp.where(l1 == m2, lane, big), axis=-1, keepdims=True)
    e2 = jnp.exp(m2 - m1)
    den = 1.0 + e2
    hit = ((lane == i1) | (lane == i2)).astype(F32)
    earlier = (lax.broadcasted_iota(jnp.int32, (tm, tm), 0) > lax.broadcasted_iota(jnp.int32, (tm, tm), 1))
    rank = jnp.dot(earlier.astype(BF16), hit.astype(BF16), preferred_element_type=F32) + carry[...]
    r1 = jnp.sum(jnp.where(lane == i1, rank, 0.0), axis=-1, keepdims=True)
    r2 = jnp.sum(jnp.where(lane == i2, rank, 0.0), axis=-1, keepdims=True)
    meta = jnp.zeros_like(logits)
    for k, v in ((META_E1, i1), (META_E2, i2), (META_W1, 1.0 / den), (META_W2, e2 / den), (META_R1, r1),
                 (META_R2, r2)):
        meta = jnp.where(lane == k, v, meta)
    meta_ref[...] = meta
    carry[...] += jnp.sum(hit, axis=0, keepdims=True)
    cnt_ref[...] = carry[...]


def _router(h, w_router_padded, j, *, n_exp, tm):
    R, D = h.shape
    NP = w_router_padded.shape[-1]
    return pl.pallas_call(
        functools.partial(_router_kernel, n_exp=n_exp),
        grid=(R // tm,),
        in_specs=[pl.BlockSpec((tm, D), lambda i: (i, 0)),
                  pl.BlockSpec((None, D, NP), lambda i: (j, 0, 0))],
        out_specs=[pl.BlockSpec((tm, NP), lambda i: (i, 0)), pl.BlockSpec((1, NP), lambda i: (0, 0))],
        out_shape=[jax.ShapeDtypeStruct((R, NP), F32), jax.ShapeDtypeStruct((1, NP), F32)],
        scratch_shapes=[pltpu.VMEM((1, NP), F32)],
        compiler_params=_params("arbitrary"),
        name="router",
    )(h, w_router_padded)


def _row_copies_wait(src_like, dst_like, sem, n):
    for _ in range(n):
        pltpu.make_async_copy(src_like, dst_like, sem).wait()


def _dispatch_kernel(p1_ref, p2_ref, h_ref, xs_ref, sem):
    tm = h_ref.shape[0]
    i0 = pl.program_id(0) * tm

    def body(i, c):
        pltpu.make_async_copy(h_ref.at[i], xs_ref.at[p1_ref[i0 + i]], sem).start()
        pltpu.make_async_copy(h_ref.at[i], xs_ref.at[p2_ref[i0 + i]], sem).start()
        return c

    lax.fori_loop(0, tm, body, 0)
    _row_copies_wait(h_ref, xs_ref.at[pl.ds(0, tm)], sem, TOP_K)


def _dispatch(h_rows, p1, p2, *, n_slots, tm):
    R, C, _ = h_rows.shape
    return pl.pallas_call(
        _dispatch_kernel,
        grid_spec=pltpu.PrefetchScalarGridSpec(
            num_scalar_prefetch=2, grid=(R // tm,),
            in_specs=[pl.BlockSpec((tm, C, LANES), lambda i, p1, p2: (i, 0, 0))],
            out_specs=pl.BlockSpec(memory_space=pl.ANY),
            scratch_shapes=[pltpu.SemaphoreType.DMA]),
        out_shape=jax.ShapeDtypeStruct((n_slots, C, LANES), F32),
        compiler_params=_params("arbitrary"),
        name="moe_dispatch",
    )(p1, p2, h_rows)


def _per_live_rows(nv, tile, sub, body):
    for m in range(sub, tile + 1, sub):
        pl.when((nv > m - sub) & (nv <= m))(functools.partial(body, m))


def _moe_glu_kernel(te_ref, nv_ref, nu_ref, xs_ref, wg_ref, wu_ref, a_ref, xb_s, *, sub):
    t = pl.program_id(0)
    j = pl.program_id(1)
    nv = nv_ref[t]
    tile, nchunk = xs_ref.shape[0], xs_ref.shape[1]

    @pl.when(t < nu_ref[0])
    def _():
        wg = wg_ref[...].astype(BF16)
        wu = wu_ref[...].astype(BF16)

        def body(m):
            rows = slice(0, m)

            @pl.when(j == 0)
            def _():
                live = lax.broadcasted_iota(jnp.int32, (m, 1), 0) < nv
                for c in range(nchunk):
                    xb_s[rows, c * LANES:(c + 1) * LANES] = jnp.where(live, xs_ref[rows, c, :], 0.0).astype(BF16)

            x = xb_s[rows, :]
            g = jnp.dot(x, wg, preferred_element_type=F32)
            u = jnp.dot(x, wu, preferred_element_type=F32)
            a_ref[rows, :] = (jax.nn.silu(g) * u).astype(a_ref.dtype)
            if m < tile:
                a_ref[m:tile, :] = jnp.zeros((tile - m, a_ref.shape[1]), a_ref.dtype)

        _per_live_rows(nv, tile, sub, body)


def _moe_down_kernel(te_ref, nv_ref, nu_ref, a_ref, wd_ref, y_ref, acc_s, *, sub, nk):
    t = pl.program_id(0)
    k = pl.program_id(1)
    nv = nv_ref[t]
    tile, nchunk = y_ref.shape[0], y_ref.shape[1]

    @pl.when(t < nu_ref[0])
    def _():
        wd = wd_ref[...].astype(BF16)

        def body(m):
            rows = slice(0, m)
            part = jnp.dot(a_ref[rows, :], wd, preferred_element_type=F32)

            @pl.when(k == 0)
            def _():
                acc_s[rows, :] = part

            @pl.when(k > 0)
            def _():
                acc_s[rows, :] += part

            @pl.when(k == nk - 1)
            def _():
                for c in range(nchunk):
                    y_ref[rows, c, :] = acc_s[rows, c * LANES:(c + 1) * LANES]
                if m < tile:
                    y_ref[m:tile, :, :] = jnp.zeros((tile - m, nchunk, LANES), F32)

        _per_live_rows(nv, tile, sub, body)


def _moe_ffn(xs, w_gate, w_up, w_down, l0, tile_expert, tile_valid, n_used, *, tile, sub, tn, tk):
    n_slots, C, _ = xs.shape
    D = C * LANES
    F = w_gate.shape[-1]
    nf, nk = F // tn, F // tk
    nt = n_slots // tile

    def live_t(t, nu):
        return jnp.minimum(t, nu[0] - 1)

    def live_step(t, j, nu, n):
        return jnp.where(t < nu[0], j, n - 1)

    def expert(t, te, nu):
        return l0 + te[live_t(t, nu)]

    row_tile = pl.BlockSpec((tile, C, LANES), lambda t, j, te, nv, nu: (live_t(t, nu), 0, 0))
    w_col = pl.BlockSpec((None, D, tn), lambda t, j, te, nv, nu: (expert(t, te, nu), 0, live_step(t, j, nu, nf)))
    a = pl.pallas_call(
        functools.partial(_moe_glu_kernel, sub=sub),
        grid_spec=pltpu.PrefetchScalarGridSpec(
            num_scalar_prefetch=3, grid=(nt, nf),
            in_specs=[row_tile, w_col, w_col],
            out_specs=pl.BlockSpec((tile, tn), lambda t, j, te, nv, nu: (live_t(t, nu), live_step(t, j, nu, nf))),
            scratch_shapes=[pltpu.VMEM((tile, D), BF16)]),
        out_shape=jax.ShapeDtypeStruct((n_slots, F), BF16),
        compiler_params=_params("arbitrary", "arbitrary"),
        name="moe_glu",
    )(tile_expert, tile_valid, n_used, xs, w_gate, w_up)
    return pl.pallas_call(
        functools.partial(_moe_down_kernel, sub=sub, nk=nk),
        grid_spec=pltpu.PrefetchScalarGridSpec(
            num_scalar_prefetch=3, grid=(nt, nk),
            in_specs=[pl.BlockSpec((tile, tk), lambda t, k, te, nv, nu: (live_t(t, nu), live_step(t, k, nu, nk))),
                      pl.BlockSpec((None, tk, D),
                                   lambda t, k, te, nv, nu: (expert(t, te, nu), live_step(t, k, nu, nk), 0))],
            out_specs=pl.BlockSpec((tile, C, LANES), lambda t, j, te, nv, nu: (live_t(t, nu), 0, 0),
                                   pipeline_mode=pl.Buffered(1)),
            scratch_shapes=[pltpu.VMEM((tile, D), F32)]),
        out_shape=jax.ShapeDtypeStruct((n_slots, C, LANES), F32),
        compiler_params=_params("arbitrary", "arbitrary"),
        name="moe_down",
    )(tile_expert, tile_valid, n_used, a, w_down)


def _combine_kernel(p1_ref, p2_ref, meta_ref, y_ref, f_ref, a_s, b_s, sem):
    tm, nchunk = a_s.shape[0], a_s.shape[1]
    i0 = pl.program_id(0) * tm

    def body(i, c):
        pltpu.make_async_copy(y_ref.at[p1_ref[i0 + i]], a_s.at[i], sem).start()
        pltpu.make_async_copy(y_ref.at[p2_ref[i0 + i]], b_s.at[i], sem).start()
        return c

    lax.fori_loop(0, tm, body, 0)
    _row_copies_wait(y_ref.at[pl.ds(0, tm)], a_s, sem, TOP_K)
    w1 = meta_ref[:, META_W1:META_W1 + 1]
    w2 = meta_ref[:, META_W2:META_W2 + 1]
    for c in range(nchunk):
        f_ref[:, c * LANES:(c + 1) * LANES] = w1 * a_s[:, c, :] + w2 * b_s[:, c, :]


def _combine(y, meta, p1, p2, *, tm):
    _, C, _ = y.shape
    R, NP = meta.shape
    return pl.pallas_call(
        _combine_kernel,
        grid_spec=pltpu.PrefetchScalarGridSpec(
            num_scalar_prefetch=2, grid=(R // tm,),
            in_specs=[pl.BlockSpec((tm, NP), lambda i, p1, p2: (i, 0)),
                      pl.BlockSpec(memory_space=pl.ANY)],
            out_specs=pl.BlockSpec((tm, C * LANES), lambda i, p1, p2: (i, 0)),
            scratch_shapes=[pltpu.VMEM((tm, C, LANES), F32), pltpu.VMEM((tm, C, LANES), F32),
                            pltpu.SemaphoreType.DMA]),
        out_shape=jax.ShapeDtypeStruct((R, C * LANES), F32),
        compiler_params=_params("arbitrary"),
        name="moe_combine",
    )(p1, p2, meta, y)


def _moe(h, h_rows, w_router_padded, w_gate, w_up, w_down, j, *, n_exp, tm_router, tile, sub, tn, tk, tm_rows):
    R = h.shape[0]
    meta, counts = _router(h, w_router_padded, j, n_exp=n_exp, tm=tm_router)
    cnt = counts[0, :n_exp].astype(jnp.int32)
    ntile = (cnt + tile - 1) // tile
    tend = jnp.cumsum(ntile)
    tstart = tend - ntile
    n_tiles = -(-TOP_K * R // tile) + n_exp
    e1, e2 = meta[:, META_E1].astype(jnp.int32), meta[:, META_E2].astype(jnp.int32)
    p1 = tstart[e1] * tile + meta[:, META_R1].astype(jnp.int32)
    p2 = tstart[e2] * tile + meta[:, META_R2].astype(jnp.int32)
    tid = jnp.arange(n_tiles, dtype=jnp.int32)
    tile_expert = jnp.minimum(jnp.sum(tid[:, None] >= tend[None, :], axis=1), n_exp - 1).astype(jnp.int32)
    tile_valid = jnp.clip(cnt[tile_expert] - (tid - tstart[tile_expert]) * tile, 0, tile)
    tile_valid = jnp.where(tid < tend[-1], tile_valid, 0).astype(jnp.int32)
    xs = _dispatch(h_rows, p1, p2, n_slots=n_tiles * tile, tm=tm_rows)
    y = _moe_ffn(xs, w_gate, w_up, w_down, j * n_exp, tile_expert, tile_valid, tend[-1:].astype(jnp.int32),
                 tile=tile, sub=sub, tn=tn, tk=tk)
    return _combine(y, meta, p1, p2, tm=tm_rows)


def kernel(x_prompt, x_sample, cache_k_win, cache_v_win, state_rglru_h, state_conv, c_prompt, c_sample, norm1_w, norm2_w, w_ada, b_ada, w_in, q_norm_w, k_norm_w, attn_sinks, conv_w, conv_b, lru_w_a, lru_b_a, lru_w_x, lru_b_x, lru_lambda, chunk_v_norm_w, chunk_w_s, chunk_b_s, w_branch_a, w_branch_b, w_branch_c, w_out, ffn_w_gate, ffn_w_up, ffn_w_down, moe_w_router, moe_w_gate, moe_w_up, moe_w_down):
    Bp, S, D = x_prompt.shape
    Bs, Ts, _ = x_sample.shape
    L = w_in.shape[0]
    window, n_kv, hd = cache_k_win.shape[2:]
    n_q = attn_sinks.shape[1]
    wa, kvw = n_q * hd, n_kv * hd
    wb = conv_w.shape[-1]
    wc = chunk_v_norm_w.shape[-1]
    chunk = chunk_w_s.shape[-1]
    cwid = conv_w.shape[1]
    n_exp = moe_w_router.shape[-1]
    assert Ts == SUBLANES and S % (Bs * Ts) == 0 and S % chunk == 0 and S % window == 0
    Rp, Rs = Bp * S, Bs * Ts
    R = Rp + Rs
    half = wb // 2
    assert wb == wc and wa % half == 0 and kvw % half == 0 or True
    col_lx = (wa + 2 * kvw) // half
    col_lg = col_lx + 2
    col_cu = col_lg + 2
    col_cv = col_cu + 2
    gate_col = wa + 2 * kvw + 2 * wb + 2 * wc

    W = dict(conv_w=conv_w, conv_b=conv_b, lru_w_a=lru_w_a, lru_b_a=lru_b_a, lru_w_x=lru_w_x, lru_b_x=lru_b_x,
             lru_lambda=lru_lambda, chunk_v_norm_w=chunk_v_norm_w, chunk_w_s=chunk_w_s, chunk_b_s=chunk_b_s,
             w_branch_a=w_branch_a, w_branch_b=w_branch_b, w_branch_c=w_branch_c)

    n_c = Bp + Bs
    n_c_pad = -(-n_c // SUBLANES) * SUBLANES
    c_all = jnp.concatenate([c_prompt, c_sample, jnp.zeros((n_c_pad - n_c, D), F32)], axis=0)
    mod = _ada(c_all, w_ada, b_ada)
    mod_p = jnp.repeat(mod[:, :Bp], SUBLANES, axis=1)
    mod_s = jnp.repeat(mod[:, Bp:n_c], Ts, axis=1)

    x = jnp.concatenate([x_prompt.reshape(Rp, D), x_sample.reshape(Rs, D)], axis=0)
    rn = functools.partial(_resid_norm, seq=S, n_prompt=Bp)
    tm_big = _pick(R, 1408, 128) if R % 128 == 0 else _pick(R, 1408, 16)
    tm_mid = _pick(R, 768, 128) if R % 128 == 0 else _pick(R, 768, 16)
    w_router_p = jnp.pad(moe_w_router, ((0, 0), (0, 0), (0, LANES - n_exp)))
    moe_gate_w = moe_w_gate.reshape((-1,) + moe_w_gate.shape[2:])
    moe_up_w = moe_w_up.reshape((-1,) + moe_w_up.shape[2:])
    moe_down_w = moe_w_down.reshape((-1,) + moe_w_down.shape[2:])
    moe_sub = MOE_SUB_ROWS if TOP_K * R >= n_exp * MOE_SUB_ROWS * MOE_TILE_SUBS else 4 * SUBLANES

    buf_rows = jnp.pad(state_conv, ((0, 0), (0, 0), (Ts - (cwid - 1), 0), (0, 0))).reshape(L, Rs, wb)
    h0_rep = jnp.repeat(state_rglru_h, Ts, axis=1)

    _, h = rn(x, None, mod_p, mod_s, norm1_w, 0, gate_k=None, scale_k=(0, 1), shift_k=(0, 0))
    outs = []
    for l in range(L):
        z = _matmul(h, w_in, l, out_dtype=F32, tm=tm_big, tn=_pick(w_in.shape[-1], 512, half), tk=D)
        attn_o, kwp, vwp = _attn_prompt(z, attn_sinks, q_norm_w, k_norm_w, l, R=R, n_prompt=Bp, seq=S,
                                        n_q=n_q, n_kv=n_kv, hd=hd, window=window)
        attn_o, kws, vws = _attn_sample(attn_o, z, cache_k_win, cache_v_win, attn_sinks, q_norm_w, k_norm_w, l,
                                        n_sample=Bs, T=Ts, n_q=n_q, n_kv=n_kv, hd=hd, window=window)
        lru_o, hp, cp8 = _lru_prompt(z, W, l, R=R, nseq=Bp, T=S, Tb=_pick(S, 256, SUBLANES),
                                     col_x=col_lx, col_g=col_lg, wb=wb)
        lru_o, hs_all = _lru_sample(lru_o, z, h0_rep[l], buf_rows[l], W, l, row0=Rp, T=Ts,
                                    col_x=col_lx, col_g=col_lg, wb=wb)
        chunk_o, vnp = _chunk(None, z, W, l, R=R, row0=0, nsteps=Rp // chunk, nb=1, Tc=chunk,
                              cps=S // chunk, col_u=col_cu, col_v=col_cv, wc=wc)
        chunk_o, vns = _chunk(chunk_o, z, W, l, R=R, row0=Rp, nsteps=1, nb=Bs, Tc=Ts, cps=1,
                              col_u=col_cu, col_v=col_cv, wc=wc)
        m = _merge(attn_o, lru_o, chunk_o, z, W, l, gate_col=gate_col, tm=tm_big, tn=_pick(D, 256, LANES))
        y = _matmul(m, w_out, l, out_dtype=F32, tm=tm_big, tn=_pick(D, 512, LANES), tk=D)
        j = l // 2
        if l % 2 == 0:
            x, h = rn(x, y, mod_p, mod_s, norm2_w, l, gate_k=(l, 2), scale_k=(l, 4), shift_k=(l, 3))
            a = _glu(h, ffn_w_gate, ffn_w_up, j, 1, None, tm=tm_big, tn=_pick(ffn_w_gate.shape[-1], 512, LANES))
            f = _matmul(a, ffn_w_down, j, out_dtype=F32, tm=tm_big, tn=_pick(D, 1024, LANES),
                        tk=_pick(a.shape[1], 1408, LANES))
        else:
            x, h, h_rows = rn(x, y, mod_p, mod_s, norm2_w, l, gate_k=(l, 2), scale_k=(l, 4), shift_k=(l, 3),
                              rows_out=True)
            f = _moe(h, h_rows, w_router_p, moe_gate_w, moe_up_w, moe_down_w, j, n_exp=n_exp, tm_router=tm_mid,
                     tile=MOE_TILE_SUBS * moe_sub, sub=moe_sub, tn=_pick(moe_gate_w.shape[-1], 512, LANES),
                     tk=_pick(moe_gate_w.shape[-1], 1024, LANES), tm_rows=Rs)
        if l + 1 < L:
            x, h = rn(x, f, mod_p, mod_s, norm1_w, l + 1, gate_k=(l, 5), scale_k=(l + 1, 1), shift_k=(l + 1, 0))
        else:
            x, _ = rn(x, f, mod_p, mod_s, None, l, gate_k=(l, 5), scale_k=None, shift_k=None)
        conv_p = cp8[:, SUBLANES - (cwid - 1):, :]
        lx_s = z[Rp:, col_lx * half:col_lx * half + wb].reshape(Bs, Ts, wb)
        conv_s = jnp.concatenate([state_conv[l], lx_s], axis=1)[:, -(cwid - 1):]
        hs = hs_all.reshape(Bs, Ts, wb)[:, Ts - 1]
        outs.append((kwp.reshape(Bp, window, n_kv, hd), vwp.reshape(Bp, window, n_kv, hd), hp.reshape(Bp, wb),
                     conv_p, vnp,
                     kws.reshape(Bs, window, n_kv, hd), vws.reshape(Bs, window, n_kv, hd), hs.reshape(Bs, wb),
                     conv_s, vns.reshape(Bs, Ts, wc)))
    st = [jnp.stack(s) for s in zip(*outs)]
    y_prompt = x[:Rp].reshape(Bp, S, D)
    y_sample = x[Rp:].reshape(Bs, Ts, D)
    return (y_prompt, y_sample, st[0], st[1], st[2], st[3], st[4], st[5], st[6], st[7], st[8], st[9])
```

```python
import functools

import jax
import jax.numpy as jnp
from jax import lax
from jax.experimental import pallas as pl
from jax.experimental.pallas import tpu as pltpu

EPS = 1e-6
LRU_C = 8.0
PAST_LEN = 16384
TOP_K = 2
MOE_SUB_ROWS = 192
MOE_TILE_SUBS = 6
ROW_COPY_UNROLL = 8
SUBLANES = 8
LANES = 128
V7X_VMEM_LIMIT = 56 * 1024 * 1024

F32 = jnp.float32
BF16 = jnp.bfloat16


def _pick(n, target, mult):
    best = None
    for t in range(mult, min(n, target) + 1, mult):
        if n % t == 0:
            best = t
    assert best is not None, (n, target, mult)
    return best


def _params(*sem):
    return pltpu.CompilerParams(dimension_semantics=sem, vmem_limit_bytes=V7X_VMEM_LIMIT)


def _rms(x, w):
    return x * lax.rsqrt(jnp.mean(x * x, axis=-1, keepdims=True) + EPS) * w


def _ada_kernel(c_ref, w_ref, b_ref, o_ref):
    a = jax.nn.silu(c_ref[...]).astype(BF16)
    o_ref[...] = jnp.dot(a, w_ref[...].astype(BF16), preferred_element_type=F32) + b_ref[...]


def _ada(c_all, w_ada, b_ada):
    L, D, W6 = w_ada.shape
    Bc = c_all.shape[0]
    tn = _pick(W6, 1024, LANES)
    return pl.pallas_call(
        _ada_kernel,
        grid=(L, W6 // tn),
        in_specs=[pl.BlockSpec((Bc, D), lambda l, j: (0, 0)),
                  pl.BlockSpec((None, D, tn), lambda l, j: (l, 0, j)),
                  pl.BlockSpec((None, 1, tn), lambda l, j: (l, 0, j))],
        out_specs=pl.BlockSpec((None, Bc, tn), lambda l, j: (l, 0, j)),
        out_shape=jax.ShapeDtypeStruct((L, Bc, W6), F32),
        compiler_params=_params("parallel", "parallel"),
        name="ada_mod",
    )(c_all, w_ada, b_ada.reshape(L, 1, W6))


def _resid_norm_kernel(*refs, n_prompt_tiles, has_resid, has_norm, has_rows):
    refs = list(refs)
    x_ref = refs.pop(0)
    if has_resid:
        y_ref, gp_ref, gs_ref = refs.pop(0), refs.pop(0), refs.pop(0)
    if has_norm:
        scp_ref, scs_ref, shp_ref, shs_ref, nw_ref = (refs.pop(0) for _ in range(5))
    if has_resid:
        xo_ref = refs.pop(0)
    if has_norm:
        h_ref = refs.pop(0)
    if has_rows:
        hrow_ref = refs.pop(0)
    i = pl.program_id(0)

    def body(prompt):
        mod = (lambda p, s: p[0:1, :]) if prompt else (lambda p, s: s[...])
        x = x_ref[...]
        if has_resid:
            x = x + mod(gp_ref, gs_ref) * y_ref[...]
            xo_ref[...] = x
        if has_norm:
            hn = _rms(x, nw_ref[...])
            h = hn * (1.0 + mod(scp_ref, scs_ref)) + mod(shp_ref, shs_ref)
            h_ref[...] = h.astype(h_ref.dtype)
            if has_rows:
                for c in range(hrow_ref.shape[1]):
                    hrow_ref[:, c, :] = h[:, c * LANES:(c + 1) * LANES]

    pl.when(i < n_prompt_tiles)(lambda: body(True))
    pl.when(i >= n_prompt_tiles)(lambda: body(False))


def _resid_norm(x, y, mod_p, mod_s, norm_w, l, *, gate_k, scale_k, shift_k, seq, n_prompt, rows_out=False):
    R, D = x.shape
    TE = mod_s.shape[1]
    tpb = seq // TE
    npt = n_prompt * tpb
    has_resid = y is not None
    has_norm = norm_w is not None
    row = pl.BlockSpec((TE, D), lambda i: (i, 0))

    def mp(lk):
        return pl.BlockSpec((None, SUBLANES, D), lambda i: (lk[0], jnp.minimum(i // tpb, n_prompt - 1), lk[1]))

    def ms(lk):
        return pl.BlockSpec((None, TE, D), lambda i: (lk[0], 0, lk[1]))

    args, specs, outs, out_specs = [x], [row], [], []
    if has_resid:
        args += [y, mod_p, mod_s]
        specs += [row, mp(gate_k), ms(gate_k)]
        outs.append(jax.ShapeDtypeStruct((R, D), F32))
        out_specs.append(row)
    if has_norm:
        args += [mod_p, mod_s, mod_p, mod_s, norm_w.reshape(norm_w.shape[0], 1, D)]
        specs += [mp(scale_k), ms(scale_k), mp(shift_k), ms(shift_k),
                  pl.BlockSpec((None, 1, D), lambda i: (l, 0, 0))]
        outs.append(jax.ShapeDtypeStruct((R, D), BF16))
        out_specs.append(row)
    if rows_out:
        outs.append(jax.ShapeDtypeStruct((R, D // LANES, LANES), F32))
        out_specs.append(pl.BlockSpec((TE, D // LANES, LANES), lambda i: (i, 0, 0)))
    res = pl.pallas_call(
        functools.partial(_resid_norm_kernel, n_prompt_tiles=npt, has_resid=has_resid, has_norm=has_norm,
                          has_rows=rows_out),
        grid=(R // TE,),
        in_specs=specs, out_specs=out_specs, out_shape=outs,
        compiler_params=_params("parallel"),
        name="resid_norm",
    )(*args)
    res = list(res)
    x_new = res.pop(0) if has_resid else x
    h = res.pop(0) if has_norm else None
    if rows_out:
        return x_new, h, res.pop(0)
    return x_new, h


def _mm_kernel(a_ref, w_ref, o_ref, acc_ref, *, nk):
    part = jnp.dot(a_ref[...], w_ref[...].astype(BF16), preferred_element_type=F32)
    if nk == 1:
        o_ref[...] = part.astype(o_ref.dtype)
        return
    k = pl.program_id(2)

    @pl.when(k == 0)
    def _():
        acc_ref[...] = part

    @pl.when(k > 0)
    def _():
        acc_ref[...] += part

    @pl.when(k == nk - 1)
    def _():
        o_ref[...] = acc_ref[...].astype(o_ref.dtype)


def _matmul(a, w, l, *, out_dtype, tm, tn, tk):
    R, K = a.shape
    N = w.shape[-1]
    nk = K // tk
    acc_shape = (tm, tn) if nk > 1 else (SUBLANES, LANES)
    return pl.pallas_call(
        functools.partial(_mm_kernel, nk=nk),
        grid=(R // tm, N // tn, nk),
        in_specs=[pl.BlockSpec((tm, tk), lambda i, j, k: (i, k)),
                  pl.BlockSpec((None, tk, tn), lambda i, j, k: (l, k, j))],
        out_specs=pl.BlockSpec((tm, tn), lambda i, j, k: (i, j)),
        out_shape=jax.ShapeDtypeStruct((R, N), out_dtype),
        scratch_shapes=[pltpu.VMEM(acc_shape, F32)],
        compiler_params=_params("parallel", "parallel", "arbitrary"),
        name="matmul",
    )(a, w)


def _softmax_sink_pv(s, sink, v):
    m = jnp.maximum(jnp.max(s, axis=-1, keepdims=True), sink)
    e = jnp.exp(s - m)
    den = jnp.sum(e, axis=-1, keepdims=True) + jnp.exp(sink - m)
    return jnp.dot((e / den).astype(BF16), v, preferred_element_type=F32)


def _attn_prompt_kernel(sink_ref, q_ref, kp_ref, kc_ref, vp_ref, vc_ref, qw_ref, kw_ref,
                        o_ref, kwin_ref, vwin_ref, *, l, n_kv, q_per_kv, hd, window, nblk):
    i = pl.program_id(1)
    blk = q_ref.shape[0]
    scale = hd ** -0.5
    rows_q = q_per_kv * blk
    r = lax.broadcasted_iota(jnp.int32, (rows_q, 2 * blk), 0) % blk
    c = lax.broadcasted_iota(jnp.int32, (rows_q, 2 * blk), 1)
    dist = r + blk - c
    ok = (dist >= 0) & (dist <= window) & ((c >= blk) | (i > 0))
    hrow = lax.broadcasted_iota(jnp.int32, (rows_q, 1), 0) // blk
    qw = qw_ref[...]
    kw = kw_ref[...]
    for g in range(n_kv):
        sl = slice(g * hd, (g + 1) * hd)
        kcn = _rms(kc_ref[:, sl], kw)
        kcat = jnp.concatenate([_rms(kp_ref[:, sl], kw), kcn], axis=0).astype(BF16)
        vcat = jnp.concatenate([vp_ref[:, sl], vc_ref[:, sl]], axis=0).astype(BF16)
        heads = [g * q_per_kv + h for h in range(q_per_kv)]
        qs = jnp.concatenate([_rms(q_ref[:, hh * hd:(hh + 1) * hd], qw) for hh in heads], axis=0).astype(BF16)
        s = lax.dot_general(qs, kcat, (((1,), (1,)), ((), ())), preferred_element_type=F32) * scale
        s = jnp.where(ok, s, -1e30)
        sink = jnp.zeros((rows_q, 1), F32)
        for h, hh in enumerate(heads):
            sink = jnp.where(hrow == h, sink_ref[l, hh], sink)
        o = _softmax_sink_pv(s, sink, vcat)
        for h, hh in enumerate(heads):
            o_ref[:, hh * hd:(hh + 1) * hd] = o[h * blk:(h + 1) * blk, :].astype(o_ref.dtype)

        @pl.when(i == nblk - 1)
        def _():
            kwin_ref[:, sl] = kcn
            vwin_ref[:, sl] = vc_ref[:, sl]


def _attn_prompt(z, sinks, q_norm_w, k_norm_w, l, *, R, n_prompt, seq, n_q, n_kv, hd, window):
    blk = window
    nblk = seq // blk
    wa = n_q * hd
    kvw = n_kv * hd
    kcol = wa // kvw
    vcol = kcol + 1

    def cur(col):
        return pl.BlockSpec((blk, kvw), lambda b, i: (b * nblk + i, col))

    def prev(col):
        return pl.BlockSpec((blk, kvw), lambda b, i: (b * nblk + jnp.maximum(i - 1, 0), col))

    L = q_norm_w.shape[0]
    nw = pl.BlockSpec((None, 1, hd), lambda b, i: (l, 0, 0))
    return pl.pallas_call(
        functools.partial(_attn_prompt_kernel, l=l, n_kv=n_kv, q_per_kv=n_q // n_kv, hd=hd,
                          window=window, nblk=nblk),
        grid=(n_prompt, nblk),
        in_specs=[pl.BlockSpec(memory_space=pltpu.SMEM),
                  pl.BlockSpec((blk, wa), lambda b, i: (b * nblk + i, 0)),
                  prev(kcol), cur(kcol), prev(vcol), cur(vcol), nw, nw],
        out_specs=[pl.BlockSpec((blk, wa), lambda b, i: (b * nblk + i, 0)),
                   pl.BlockSpec((None, blk, kvw), lambda b, i: (b, 0, 0)),
                   pl.BlockSpec((None, blk, kvw), lambda b, i: (b, 0, 0))],
        out_shape=[jax.ShapeDtypeStruct((R, wa), BF16),
                   jax.ShapeDtypeStruct((n_prompt, blk, kvw), F32),
                   jax.ShapeDtypeStruct((n_prompt, blk, kvw), F32)],
        compiler_params=_params("parallel", "arbitrary"),
        name="attn_prompt",
    )(sinks, z, z, z, z, z, q_norm_w.reshape(L, 1, hd), k_norm_w.reshape(L, 1, hd))


def _attn_sample_kernel(sink_ref, o_in_ref, q_ref, kn_ref, vn_ref, ck_ref, cv_ref, qw_ref, kw_ref,
                        o_ref, kwin_ref, vwin_ref, keys_s, vals_s, ostage,
                        *, l, nb, T, n_kv, q_per_kv, hd, window):
    del o_in_ref
    W = ck_ref.shape[1]
    scale = hd ** -0.5
    rows_q = q_per_kv * T
    r = lax.broadcasted_iota(jnp.int32, (rows_q, 2 * W), 0) % T
    c = lax.broadcasted_iota(jnp.int32, (rows_q, 2 * W), 1)
    q_pos = PAST_LEN + r
    k_pos = jnp.where(c < W, PAST_LEN - W + c, PAST_LEN + c - W)
    dist = q_pos - k_pos
    ok = (dist >= 0) & (dist <= window) & (k_pos >= 0) & (c < W + T)
    hrow = lax.broadcasted_iota(jnp.int32, (rows_q, 1), 0) // T
    qw = qw_ref[...]
    kw = kw_ref[...]
    keys_s[...] = jnp.zeros_like(keys_s)
    vals_s[...] = jnp.zeros_like(vals_s)

    def step(b, carry):
        rows = pl.ds(pl.multiple_of(b * T, T), T)
        for g in range(n_kv):
            sl = slice(g * hd, (g + 1) * hd)
            ck = ck_ref[b, :, sl]
            cv = cv_ref[b, :, sl]
            knn = _rms(kn_ref[rows, sl], kw)
            vnn = vn_ref[rows, sl]
            keys_s[0:W, :] = ck
            keys_s[W:W + T, :] = knn
            vals_s[0:W, :] = cv
            vals_s[W:W + T, :] = vnn
            qs = jnp.concatenate(
                [_rms(q_ref[rows, (g * q_per_kv + h) * hd:(g * q_per_kv + h + 1) * hd], qw)
                 for h in range(q_per_kv)], axis=0).astype(BF16)
            s = lax.dot_general(qs, keys_s[...].astype(BF16), (((1,), (1,)), ((), ())),
                                preferred_element_type=F32) * scale
            s = jnp.where(ok, s, -1e30)
            sink = jnp.zeros((rows_q, 1), F32)
            for h in range(q_per_kv):
                sink = jnp.where(hrow == h, sink_ref[l, g * q_per_kv + h], sink)
            o = _softmax_sink_pv(s, sink, vals_s[...].astype(BF16))
            for h in range(q_per_kv):
                head = g * q_per_kv + h
                ostage[rows, head * hd:(head + 1) * hd] = o[h * T:(h + 1) * T, :]
            kwin_ref[b, 0:W - T, sl] = ck[T:, :]
            kwin_ref[b, W - T:W, sl] = knn
            vwin_ref[b, 0:W - T, sl] = cv[T:, :]
            vwin_ref[b, W - T:W, sl] = vnn
        return carry

    lax.fori_loop(0, nb, step, 0)
    o_ref[...] = ostage[...].astype(o_ref.dtype)


def _attn_sample(attn_o, z, cache_k, cache_v, sinks, q_norm_w, k_norm_w, l, *, n_sample, T, n_q, n_kv, hd,
                 window):
    R, wa = attn_o.shape
    kvw = n_kv * hd
    rows = n_sample * T
    blk_i = (R - rows) // rows
    kcol = wa // kvw
    L, nb, W = cache_k.shape[:3]
    ck = cache_k.reshape(L, nb, W, kvw)
    cv = cache_v.reshape(L, nb, W, kvw)
    nw = pl.BlockSpec((None, 1, hd), lambda i: (l, 0, 0))
    cache = pl.BlockSpec((None, nb, W, kvw), lambda i: (l, 0, 0, 0))
    win = pl.BlockSpec((nb, W, kvw), lambda i: (0, 0, 0))
    return pl.pallas_call(
        functools.partial(_attn_sample_kernel, l=l, nb=nb, T=T, n_kv=n_kv, q_per_kv=n_q // n_kv, hd=hd,
                          window=window),
        grid=(1,),
        in_specs=[pl.BlockSpec(memory_space=pltpu.SMEM),
                  pl.BlockSpec(memory_space=pl.ANY),
                  pl.BlockSpec((rows, wa), lambda i: (blk_i, 0)),
                  pl.BlockSpec((rows, kvw), lambda i: (blk_i, kcol)),
                  pl.BlockSpec((rows, kvw), lambda i: (blk_i, kcol + 1)),
                  cache, cache, nw, nw],
        out_specs=[pl.BlockSpec((rows, wa), lambda i: (blk_i, 0)), win, win],
        out_shape=[jax.ShapeDtypeStruct((R, wa), BF16),
                   jax.ShapeDtypeStruct((nb, W, kvw), F32),
                   jax.ShapeDtypeStruct((nb, W, kvw), F32)],
        scratch_shapes=[pltpu.VMEM((2 * W, hd), F32), pltpu.VMEM((2 * W, hd), F32),
                        pltpu.VMEM((rows, wa), F32)],
        input_output_aliases={1: 0},
        compiler_params=_params("arbitrary"),
        name="attn_sample",
    )(sinks, attn_o, z, z, z, ck, cv, q_norm_w.reshape(-1, 1, hd), k_norm_w.reshape(-1, 1, hd))


def _lru_gates(xc, wa_ref, ba_ref, wx_ref, bx_ref, lam_ref, a_s, u_s, *, nblk, lb):
    xcb = xc.astype(BF16)
    for n in range(nblk):
        sl = slice(n * lb, (n + 1) * lb)
        ra = jnp.dot(xcb[:, sl], wa_ref[n].astype(BF16), preferred_element_type=F32) + ba_ref[:, sl]
        rx = jnp.dot(xcb[:, sl], wx_ref[n].astype(BF16), preferred_element_type=F32) + bx_ref[:, sl]
        log_a = -LRU_C * jax.nn.sigmoid(ra) * jax.nn.softplus(-lam_ref[:, sl])
        a_s[:, sl] = jnp.exp(log_a)
        th = jnp.tanh(log_a)
        u_s[:, sl] = jnp.sqrt(-2.0 * th / (1.0 - th)) * jax.nn.sigmoid(rx) * xc[:, sl]

def _tile_scan(a, u, T):
    t = lax.broadcasted_iota(jnp.int32, (a.shape[0], 1), 0) % T
    d = 1
    while d < T:
        keep = t >= d
        u = jnp.where(keep, a * pltpu.roll(u, d, axis=0) + u, u)
        a = jnp.where(keep, a * pltpu.roll(a, d, axis=0), a)
        d *= 2
    return a, u


def _lru_prompt_kernel(x_ref, g_ref, cw_ref, cb_ref, wa_ref, ba_ref, wx_ref, bx_ref, lam_ref,
                       o_ref, hT_ref, cnew_ref, xp_s, a_s, u_s, h_s, *, Tb, ntb, cw, nblk, lb):
    tb = pl.program_id(2)
    P = SUBLANES

    @pl.when(tb == 0)
    def _():
        xp_s[0:P, :] = jnp.zeros((P, xp_s.shape[1]), F32)
        h_s[...] = jnp.zeros_like(h_s)

    xp_s[P:P + Tb, :] = x_ref[...]
    xc = cb_ref[...]
    for j in range(cw):
        off = P - (cw - 1) + j
        xc = xc + xp_s[off:off + Tb, :] * cw_ref[j:j + 1, :]
    _lru_gates(xc, wa_ref, ba_ref, wx_ref, bx_ref, lam_ref, a_s, u_s, nblk=nblk, lb=lb)

    a, u = _tile_scan(a_s[...], u_s[...], P)
    a_s[...] = a
    u_s[...] = u

    def step(k, h):
        rr = pl.ds(pl.multiple_of(k * P, P), P)
        hs = u_s[rr, :] + a_s[rr, :] * h
        u_s[rr, :] = hs
        return hs[P - 1:P, :]

    h = lax.fori_loop(0, Tb // P, step, h_s[...])
    h_s[...] = h
    o_ref[...] = (jax.nn.gelu(g_ref[...]) * u_s[...]).astype(o_ref.dtype)
    tail = xp_s[Tb:Tb + P, :]
    xp_s[0:P, :] = tail

    @pl.when(tb == ntb - 1)
    def _():
        hT_ref[...] = h
        cnew_ref[...] = tail


def _lru_sample_kernel(o_in_ref, x_ref, g_ref, h0_ref, buf_ref, cw_ref, cb_ref, wa_ref, ba_ref, wx_ref, bx_ref,
                       lam_ref, o_ref, hs_ref, a_s, u_s, *, T, cw, nblk, lb):
    del o_in_ref
    rows = x_ref.shape[0]
    t = lax.broadcasted_iota(jnp.int32, (rows, 1), 0) % T
    x = x_ref[...]
    buf = buf_ref[...]
    xc = cb_ref[...] + x * cw_ref[cw - 1:cw, :]
    for s in range(1, cw):
        xs = jnp.where(t >= s, pltpu.roll(x, s, axis=0), pltpu.roll(buf, rows - (T - s), axis=0))
        xc = xc + xs * cw_ref[cw - 1 - s:cw - s, :]
    _lru_gates(xc, wa_ref, ba_ref, wx_ref, bx_ref, lam_ref, a_s, u_s, nblk=nblk, lb=lb)
    a, u = _tile_scan(a_s[...], u_s[...], T)
    hs = u + a * h0_ref[...]
    hs_ref[...] = hs
    o_ref[...] = (jax.nn.gelu(g_ref[...]) * hs).astype(o_ref.dtype)


def _lru_weight_args(W, l, wb, index):
    ch = wb // 2
    nlb, lb = W['lru_w_a'].shape[1:3]
    nblk = nlb // 2
    cw = W['conv_w'].shape[1]

    def vec(name):
        return W[name].reshape(W[name].shape[0], 1, wb), pl.BlockSpec((None, 1, ch), index(lambda hf: (l, 0, hf)))

    def blkw(name):
        return W[name], pl.BlockSpec((None, nblk, lb, lb), index(lambda hf: (l, hf, 0, 0)))

    pairs = [(W['conv_w'], pl.BlockSpec((None, cw, ch), index(lambda hf: (l, 0, hf)))),
             vec('conv_b'), blkw('lru_w_a'), vec('lru_b_a'), blkw('lru_w_x'), vec('lru_b_x'), vec('lru_lambda')]
    return [p[0] for p in pairs], [p[1] for p in pairs], dict(cw=cw, nblk=nblk, lb=lb)


def _lru_prompt(z, W, l, *, R, nseq, T, Tb, col_x, col_g, wb):
    ch = wb // 2
    ntb = T // Tb
    wargs, wspecs, kw = _lru_weight_args(W, l, wb, lambda f: (lambda s, hf, t: f(hf)))

    def rowblk(col):
        return pl.BlockSpec((Tb, ch), lambda s, hf, t: (s * ntb + t, col + hf))

    return pl.pallas_call(
        functools.partial(_lru_prompt_kernel, Tb=Tb, ntb=ntb, **kw),
        grid=(nseq, 2, ntb),
        in_specs=[rowblk(col_x), rowblk(col_g)] + wspecs,
        out_specs=[pl.BlockSpec((Tb, ch), lambda s, hf, t: (s * ntb + t, hf)),
                   pl.BlockSpec((None, 1, ch), lambda s, hf, t: (s, 0, hf)),
                   pl.BlockSpec((None, SUBLANES, ch), lambda s, hf, t: (s, 0, hf))],
        out_shape=[jax.ShapeDtypeStruct((R, wb), BF16),
                   jax.ShapeDtypeStruct((nseq, 1, wb), F32),
                   jax.ShapeDtypeStruct((nseq, SUBLANES, wb), F32)],
        scratch_shapes=[pltpu.VMEM((SUBLANES + Tb, ch), F32), pltpu.VMEM((Tb, ch), F32),
                        pltpu.VMEM((Tb, ch), F32), pltpu.VMEM((1, ch), F32)],
        compiler_params=_params("parallel", "parallel", "arbitrary"),
        name="lru_prompt",
    )(z, z, *wargs)


def _lru_sample(lru_o, z, h0_rep, buf_rows, W, l, *, row0, T, col_x, col_g, wb):
    ch = wb // 2
    rows = h0_rep.shape[0]
    rb = row0 // rows
    wargs, wspecs, kw = _lru_weight_args(W, l, wb, lambda f: (lambda hf: f(hf)))
    st = pl.BlockSpec((rows, ch), lambda hf: (0, hf))
    return pl.pallas_call(
        functools.partial(_lru_sample_kernel, T=T, **kw),
        grid=(2,),
        in_specs=[pl.BlockSpec(memory_space=pl.ANY),
                  pl.BlockSpec((rows, ch), lambda hf: (rb, col_x + hf)),
                  pl.BlockSpec((rows, ch), lambda hf: (rb, col_g + hf)), st, st] + wspecs,
        out_specs=[pl.BlockSpec((rows, ch), lambda hf: (rb, hf)), st],
        out_shape=[jax.ShapeDtypeStruct(lru_o.shape, BF16), jax.ShapeDtypeStruct((rows, wb), F32)],
        scratch_shapes=[pltpu.VMEM((rows, ch), F32), pltpu.VMEM((rows, ch), F32)],
        input_output_aliases={0: 0},
        compiler_params=_params("parallel"),
        name="lru_sample",
    )(lru_o, z, z, h0_rep, buf_rows, *wargs)


def _chunk_kernel(*refs, nb, Tc, ngroups, gw, cps, aliased):
    refs = list(refs)
    if aliased:
        refs.pop(0)
    ulo_ref, uhi_ref, vlo_ref, vhi_ref, nw_ref, ws_ref, bst_ref, o_ref, vn_ref = refs
    rows = nb * Tc
    half = vlo_ref.shape[1]
    wc = 2 * half
    v_lo = jax.nn.gelu(vlo_ref[...])
    v_hi = jax.nn.gelu(vhi_ref[...])
    ms = (jnp.sum(v_lo * v_lo, axis=-1, keepdims=True) + jnp.sum(v_hi * v_hi, axis=-1, keepdims=True)) / wc
    inv = lax.rsqrt(ms + EPS)
    vn_halves = (v_lo * inv * nw_ref[:, 0:half], v_hi * inv * nw_ref[:, half:wc])
    u_halves = (ulo_ref, uhi_ref)

    r = lax.broadcasted_iota(jnp.int32, (rows, rows), 0)
    c = lax.broadcasted_iota(jnp.int32, (rows, rows), 1)
    mask = (r // Tc == c // Tc) & (c % Tc <= r % Tc)
    if nb > 1:
        sel = (lax.broadcasted_iota(jnp.int32, (ws_ref.shape[2], rows), 0)
               == lax.broadcasted_iota(jnp.int32, (ws_ref.shape[2], rows), 1) % Tc).astype(BF16)
        bias_rows = jnp.broadcast_to(bst_ref[0:Tc, :][None], (nb, Tc, bst_ref.shape[1])).reshape(rows, -1)
    else:
        bias_rows = bst_ref[0:Tc, :]
    gph = half // gw
    for g in range(ngroups):
        hf, gi = divmod(g, gph)
        sl = slice(gi * gw, (gi + 1) * gw)
        if nb > 1:
            t1 = jnp.broadcast_to(ws_ref[g, 0:Tc, :][None], (nb, Tc, ws_ref.shape[2])).reshape(rows, -1)
            wfull = jnp.dot(t1.astype(BF16), sel, preferred_element_type=F32)
        else:
            wfull = ws_ref[g, 0:Tc, 0:Tc]
        wm = jnp.where(mask, wfull, 0.0).astype(BF16)
        mixed = jnp.dot(wm, vn_halves[hf][:, sl].astype(BF16), preferred_element_type=F32)
        mixed = mixed + bias_rows[:, g:g + 1]
        o_ref[:, g * gw:(g + 1) * gw] = (jax.nn.gelu(u_halves[hf][:, sl]) * mixed).astype(o_ref.dtype)

    if cps == 1:
        vn_ref[:, 0:half] = vn_halves[0]
        vn_ref[:, half:wc] = vn_halves[1]
    else:
        @pl.when(pl.program_id(0) % cps == cps - 1)
        def _():
            vn_ref[:, 0:half] = vn_halves[0]
            vn_ref[:, half:wc] = vn_halves[1]


def _chunk(chunk_o, z, W, l, *, R, row0, nsteps, nb, Tc, cps, col_u, col_v, wc):
    half = wc // 2
    ngroups, chunk = W['chunk_w_s'].shape[1:3]
    gw = wc // ngroups
    rows = nb * Tc
    rb0 = row0 // rows
    aliased = chunk_o is not None

    def rowblk(col):
        return pl.BlockSpec((rows, half), lambda i: (rb0 + i, col))

    args = [z, z, z, z, W['chunk_v_norm_w'].reshape(-1, 1, wc), W['chunk_w_s'],
            jnp.swapaxes(W['chunk_b_s'], 1, 2)]
    specs = [rowblk(col_u), rowblk(col_u + 1), rowblk(col_v), rowblk(col_v + 1),
             pl.BlockSpec((None, 1, wc), lambda i: (l, 0, 0)),
             pl.BlockSpec((None, ngroups, chunk, chunk), lambda i: (l, 0, 0, 0)),
             pl.BlockSpec((None, chunk, ngroups), lambda i: (l, 0, 0))]
    io_alias = {}
    if aliased:
        args.insert(0, chunk_o)
        specs.insert(0, pl.BlockSpec(memory_space=pl.ANY))
        io_alias = {0: 0}
    return pl.pallas_call(
        functools.partial(_chunk_kernel, nb=nb, Tc=Tc, ngroups=ngroups, gw=gw, cps=cps, aliased=aliased),
        grid=(nsteps,),
        in_specs=specs,
        out_specs=[pl.BlockSpec((rows, wc), lambda i: (rb0 + i, 0)),
                   pl.BlockSpec((None, rows, wc), lambda i: (i // cps, 0, 0))],
        out_shape=[jax.ShapeDtypeStruct((R, wc), BF16),
                   jax.ShapeDtypeStruct((nsteps // cps, rows, wc), F32)],
        input_output_aliases=io_alias,
        compiler_params=_params("arbitrary"),
        name="chunk_sample" if aliased else "chunk_prompt",
    )(*args)


def _merge_kernel(a_ref, b_ref, c_ref, wa_ref, wb_ref, wc_ref, ga_ref, gb_ref, gc_ref, o_ref):
    def branch(x_ref, w_ref, g_ref):
        y = jnp.dot(x_ref[...], w_ref[...].astype(BF16), preferred_element_type=F32)
        return jax.nn.sigmoid(g_ref[...]) * y

    m = branch(a_ref, wa_ref, ga_ref) + branch(b_ref, wb_ref, gb_ref) + branch(c_ref, wc_ref, gc_ref)
    o_ref[...] = m.astype(o_ref.dtype)


def _merge(attn_o, lru_o, chunk_o, z, W, l, *, gate_col, tm, tn):
    R = attn_o.shape[0]
    D = W['w_branch_a'].shape[-1]
    g0 = gate_col // tn
    gstep = D // tn

    def xin(a):
        return pl.BlockSpec((tm, a.shape[1]), lambda i, j: (i, 0))

    def win(w):
        return pl.BlockSpec((None, w.shape[1], tn), lambda i, j: (l, 0, j))

    def gin(k):
        return pl.BlockSpec((tm, tn), lambda i, j: (i, g0 + k * gstep + j))

    return pl.pallas_call(
        _merge_kernel,
        grid=(R // tm, D // tn),
        in_specs=[xin(attn_o), xin(lru_o), xin(chunk_o),
                  win(W['w_branch_a']), win(W['w_branch_b']), win(W['w_branch_c']),
                  gin(0), gin(1), gin(2)],
        out_specs=pl.BlockSpec((tm, tn), lambda i, j: (i, j)),
        out_shape=jax.ShapeDtypeStruct((R, D), BF16),
        compiler_params=_params("parallel", "parallel"),
        name="merge",
    )(attn_o, lru_o, chunk_o, W['w_branch_a'], W['w_branch_b'], W['w_branch_c'], z, z, z)


def _glu_kernel(*refs, gated):
    if gated:
        h_ref, wg_ref, wu_ref, gate_ref, o_ref = refs
    else:
        h_ref, wg_ref, wu_ref, o_ref = refs
    h = h_ref[...]
    g = jnp.dot(h, wg_ref[...].astype(BF16), preferred_element_type=F32)
    u = jnp.dot(h, wu_ref[...].astype(BF16), preferred_element_type=F32)
    a = jax.nn.silu(g) * u
    if gated:
        e = pl.program_id(1)
        gate = gate_ref[...]
        lane = lax.broadcasted_iota(jnp.int32, gate.shape, 1)
        a = a * jnp.sum(jnp.where(lane == e, gate, 0.0), axis=-1, keepdims=True)
    o_ref[...] = a.astype(o_ref.dtype)


def _glu(h, w_gate, w_up, l0, n_exp, gate, *, tm, tn):
    R, D = h.shape
    F = w_gate.shape[-1]
    nf = F // tn
    gated = gate is not None
    wspec = pl.BlockSpec((None, D, tn), lambda i, e, j: (l0 + e, 0, j))
    args = [h, w_gate, w_up]
    specs = [pl.BlockSpec((tm, D), lambda i, e, j: (i, 0)), wspec, wspec]
    if gated:
        args.append(gate)
        specs.append(pl.BlockSpec((tm, gate.shape[1]), lambda i, e, j: (i, 0)))
    return pl.pallas_call(
        functools.partial(_glu_kernel, gated=gated),
        grid=(R // tm, n_exp, nf),
        in_specs=specs,
        out_specs=pl.BlockSpec((tm, tn), lambda i, e, j: (i, e * nf + j)),
        out_shape=jax.ShapeDtypeStruct((R, n_exp * F), BF16),
        compiler_params=_params("parallel", "parallel", "parallel"),
        name="glu",
    )(*args)


META_E1, META_E2, META_W1, META_W2, META_R1, META_R2 = range(6)


def _router_kernel(h_ref, w_ref, meta_ref, cnt_ref, carry, *, n_exp):
    i = pl.program_id(0)

    @pl.when(i == 0)
    def _():
        carry[...] = jnp.zeros_like(carry)

    logits = jnp.dot(h_ref[...], w_ref[...].astype(BF16), preferred_element_type=F32)
    tm = logits.shape[0]
    lane = lax.broadcasted_iota(jnp.int32, logits.shape, 1).astype(F32)
    big = float(logits.shape[1])
    l0 = jnp.where(lane < n_exp, logits, -jnp.inf)
    m1 = jnp.max(l0, axis=-1, keepdims=True)
    i1 = jnp.min(jnp.where(l0 == m1, lane, big), axis=-1, keepdims=True)
    l1 = jnp.where(lane == i1, -jnp.inf, l0)
    m2 = jnp.max(l1, axis=-1, keepdims=True)
    i2 = jnp.min(jnp.where(l1 == m2, lane, big), axis=-1, keepdims=True)
    e2 = jnp.exp(m2 - m1)
    den = 1.0 + e2
    hit = ((lane == i1) | (lane == i2)).astype(F32)
    earlier = (lax.broadcasted_iota(jnp.int32, (tm, tm), 0) > lax.broadcasted_iota(jnp.int32, (tm, tm), 1))
    rank = jnp.dot(earlier.astype(BF16), hit.astype(BF16), preferred_element_type=F32) + carry[...]
    r1 = jnp.sum(jnp.where(lane == i1, rank, 0.0), axis=-1, keepdims=True)
    r2 = jnp.sum(jnp.where(lane == i2, rank, 0.0), axis=-1, keepdims=True)
    meta = jnp.zeros_like(logits)
    for k, v in ((META_E1, i1), (META_E2, i2), (META_W1, 1.0 / den), (META_W2, e2 / den), (META_R1, r1),
                 (META_R2, r2)):
        meta = jnp.where(lane == k, v, meta)
    meta_ref[...] = meta
    carry[...] += jnp.sum(hit, axis=0, keepdims=True)
    cnt_ref[...] = carry[...]


def _router(h, w_router_padded, j, *, n_exp, tm):
    R, D = h.shape
    NP = w_router_padded.shape[-1]
    return pl.pallas_call(
        functools.partial(_router_kernel, n_exp=n_exp),
        grid=(R // tm,),
        in_specs=[pl.BlockSpec((tm, D), lambda i: (i, 0)),
                  pl.BlockSpec((None, D, NP), lambda i: (j, 0, 0))],
        out_specs=[pl.BlockSpec((tm, NP), lambda i: (i, 0)), pl.BlockSpec((1, NP), lambda i: (0, 0))],
        out_shape=[jax.ShapeDtypeStruct((R, NP), F32), jax.ShapeDtypeStruct((1, NP), F32)],
        scratch_shapes=[pltpu.VMEM((1, NP), F32)],
        compiler_params=_params("arbitrary"),
        name="router",
    )(h, w_router_padded)


def _row_copies_wait(src_like, dst_like, sem, n):
    for _ in range(n):
        pltpu.make_async_copy(src_like, dst_like, sem).wait()


def _dispatch_kernel(p1_ref, p2_ref, h_ref, xs_ref, sem):
    tm = h_ref.shape[0]
    i0 = pl.program_id(0) * tm

    def body(i, c):
        pltpu.make_async_copy(h_ref.at[i], xs_ref.at[p1_ref[i0 + i]], sem).start()
        pltpu.make_async_copy(h_ref.at[i], xs_ref.at[p2_ref[i0 + i]], sem).start()
        return c

    lax.fori_loop(0, tm, body, 0, unroll=ROW_COPY_UNROLL)
    _row_copies_wait(h_ref, xs_ref.at[pl.ds(0, tm)], sem, TOP_K)


def _dispatch(h_rows, p1, p2, *, n_slots, tm):
    R, C, _ = h_rows.shape
    return pl.pallas_call(
        _dispatch_kernel,
        grid_spec=pltpu.PrefetchScalarGridSpec(
            num_scalar_prefetch=2, grid=(R // tm,),
            in_specs=[pl.BlockSpec((tm, C, LANES), lambda i, p1, p2: (i, 0, 0))],
            out_specs=pl.BlockSpec(memory_space=pl.ANY),
            scratch_shapes=[pltpu.SemaphoreType.DMA]),
        out_shape=jax.ShapeDtypeStruct((n_slots, C, LANES), F32),
        compiler_params=_params("arbitrary"),
        name="moe_dispatch",
    )(p1, p2, h_rows)


def _per_live_rows(nv, tile, sub, body):
    for m in range(sub, tile + 1, sub):
        pl.when((nv > m - sub) & (nv <= m))(functools.partial(body, m))


def _moe_glu_kernel(te_ref, nv_ref, nu_ref, xs_ref, wg_ref, wu_ref, a_ref, xb_s, *, sub):
    t = pl.program_id(0)
    j = pl.program_id(1)
    nv = nv_ref[t]
    tile, nchunk = xs_ref.shape[0], xs_ref.shape[1]

    @pl.when(t < nu_ref[0])
    def _():
        wg = wg_ref[...].astype(BF16)
        wu = wu_ref[...].astype(BF16)

        def body(m):
            rows = slice(0, m)

            @pl.when(j == 0)
            def _():
                live = lax.broadcasted_iota(jnp.int32, (m, 1), 0) < nv
                for c in range(nchunk):
                    xb_s[rows, c * LANES:(c + 1) * LANES] = jnp.where(live, xs_ref[rows, c, :], 0.0).astype(BF16)

            x = xb_s[rows, :]
            g = jnp.dot(x, wg, preferred_element_type=F32)
            u = jnp.dot(x, wu, preferred_element_type=F32)
            a_ref[rows, :] = (jax.nn.silu(g) * u).astype(a_ref.dtype)
            if m < tile:
                a_ref[m:tile, :] = jnp.zeros((tile - m, a_ref.shape[1]), a_ref.dtype)

        _per_live_rows(nv, tile, sub, body)


def _moe_down_kernel(te_ref, nv_ref, nu_ref, a_ref, wd_ref, y_ref, acc_s, *, sub, nk):
    t = pl.program_id(0)
    k = pl.program_id(1)
    nv = nv_ref[t]
    tile, nchunk = y_ref.shape[0], y_ref.shape[1]

    @pl.when(t < nu_ref[0])
    def _():
        wd = wd_ref[...].astype(BF16)

        def body(m):
            rows = slice(0, m)
            part = jnp.dot(a_ref[rows, :], wd, preferred_element_type=F32)

            @pl.when(k == 0)
            def _():
                acc_s[rows, :] = part

            @pl.when(k > 0)
            def _():
                acc_s[rows, :] += part

            @pl.when(k == nk - 1)
            def _():
                for c in range(nchunk):
                    y_ref[rows, c, :] = acc_s[rows, c * LANES:(c + 1) * LANES]
                if m < tile:
                    y_ref[m:tile, :, :] = jnp.zeros((tile - m, nchunk, LANES), F32)

        _per_live_rows(nv, tile, sub, body)


def _moe_ffn(xs, w_gate, w_up, w_down, l0, tile_expert, tile_valid, n_used, *, tile, sub, tn, tk):
    n_slots, C, _ = xs.shape
    D = C * LANES
    F = w_gate.shape[-1]
    nf, nk = F // tn, F // tk
    nt = n_slots // tile

    def live_t(t, nu):
        return jnp.minimum(t, nu[0] - 1)

    def live_step(t, j, nu, n):
        return jnp.where(t < nu[0], j, n - 1)

    def expert(t, te, nu):
        return l0 + te[live_t(t, nu)]

    row_tile = pl.BlockSpec((tile, C, LANES), lambda t, j, te, nv, nu: (live_t(t, nu), 0, 0))
    w_col = pl.BlockSpec((None, D, tn), lambda t, j, te, nv, nu: (expert(t, te, nu), 0, live_step(t, j, nu, nf)))
    a = pl.pallas_call(
        functools.partial(_moe_glu_kernel, sub=sub),
        grid_spec=pltpu.PrefetchScalarGridSpec(
            num_scalar_prefetch=3, grid=(nt, nf),
            in_specs=[row_tile, w_col, w_col],
            out_specs=pl.BlockSpec((tile, tn), lambda t, j, te, nv, nu: (live_t(t, nu), live_step(t, j, nu, nf))),
            scratch_shapes=[pltpu.VMEM((tile, D), BF16)]),
        out_shape=jax.ShapeDtypeStruct((n_slots, F), BF16),
        compiler_params=_params("arbitrary", "arbitrary"),
        name="moe_glu",
    )(tile_expert, tile_valid, n_used, xs, w_gate, w_up)
    return pl.pallas_call(
        functools.partial(_moe_down_kernel, sub=sub, nk=nk),
        grid_spec=pltpu.PrefetchScalarGridSpec(
            num_scalar_prefetch=3, grid=(nt, nk),
            in_specs=[pl.BlockSpec((tile, tk), lambda t, k, te, nv, nu: (live_t(t, nu), live_step(t, k, nu, nk))),
                      pl.BlockSpec((None, tk, D),
                                   lambda t, k, te, nv, nu: (expert(t, te, nu), live_step(t, k, nu, nk), 0))],
            out_specs=pl.BlockSpec((tile, C, LANES), lambda t, j, te, nv, nu: (live_t(t, nu), 0, 0),
                                   pipeline_mode=pl.Buffered(1)),
            scratch_shapes=[pltpu.VMEM((tile, D), F32)]),
        out_shape=jax.ShapeDtypeStruct((n_slots, C, LANES), F32),
        compiler_params=_params("arbitrary", "arbitrary"),
        name="moe_down",
    )(tile_expert, tile_valid, n_used, a, w_down)


def _combine_kernel(p1_ref, p2_ref, meta_ref, y_ref, f_ref, a_s, b_s, sem):
    tm, nchunk = a_s.shape[0], a_s.shape[1]
    i0 = pl.program_id(0) * tm

    def body(i, c):
        pltpu.make_async_copy(y_ref.at[p1_ref[i0 + i]], a_s.at[i], sem).start()
        pltpu.make_async_copy(y_ref.at[p2_ref[i0 + i]], b_s.at[i], sem).start()
        return c

    lax.fori_loop(0, tm, body, 0, unroll=ROW_COPY_UNROLL)
    _row_copies_wait(y_ref.at[pl.ds(0, tm)], a_s, sem, TOP_K)
    w1 = meta_ref[:, META_W1:META_W1 + 1]
    w2 = meta_ref[:, META_W2:META_W2 + 1]
    for c in range(nchunk):
        f_ref[:, c * LANES:(c + 1) * LANES] = w1 * a_s[:, c, :] + w2 * b_s[:, c, :]


def _combine(y, meta, p1, p2, *, tm):
    _, C, _ = y.shape
    R, NP = meta.shape
    return pl.pallas_call(
        _combine_kernel,
        grid_spec=pltpu.PrefetchScalarGridSpec(
            num_scalar_prefetch=2, grid=(R // tm,),
            in_specs=[pl.BlockSpec((tm, NP), lambda i, p1, p2: (i, 0)),
                      pl.BlockSpec(memory_space=pl.ANY)],
            out_specs=pl.BlockSpec((tm, C * LANES), lambda i, p1, p2: (i, 0)),
            scratch_shapes=[pltpu.VMEM((tm, C, LANES), F32), pltpu.VMEM((tm, C, LANES), F32),
                            pltpu.SemaphoreType.DMA]),
        out_shape=jax.ShapeDtypeStruct((R, C * LANES), F32),
        compiler_params=_params("arbitrary"),
        name="moe_combine",
    )(p1, p2, meta, y)


def _moe(h, h_rows, w_router_padded, w_gate, w_up, w_down, j, *, n_exp, tm_router, tile, sub, tn, tk, tm_rows):
    R = h.shape[0]
    meta, counts = _router(h, w_router_padded, j, n_exp=n_exp, tm=tm_router)
    cnt = counts[0, :n_exp].astype(jnp.int32)
    ntile = (cnt + tile - 1) // tile
    tend = jnp.cumsum(ntile)
    tstart = tend - ntile
    n_tiles = -(-TOP_K * R // tile) + n_exp
    e1, e2 = meta[:, META_E1].astype(jnp.int32), meta[:, META_E2].astype(jnp.int32)
    p1 = tstart[e1] * tile + meta[:, META_R1].astype(jnp.int32)
    p2 = tstart[e2] * tile + meta[:, META_R2].astype(jnp.int32)
    tid = jnp.arange(n_tiles, dtype=jnp.int32)
    tile_expert = jnp.minimum(jnp.sum(tid[:, None] >= tend[None, :], axis=1), n_exp - 1).astype(jnp.int32)
    tile_valid = jnp.clip(cnt[tile_expert] - (tid - tstart[tile_expert]) * tile, 0, tile)
    tile_valid = jnp.where(tid < tend[-1], tile_valid, 0).astype(jnp.int32)
    xs = _dispatch(h_rows, p1, p2, n_slots=n_tiles * tile, tm=tm_rows)
    y = _moe_ffn(xs, w_gate, w_up, w_down, j * n_exp, tile_expert, tile_valid, tend[-1:].astype(jnp.int32),
                 tile=tile, sub=sub, tn=tn, tk=tk)
    return _combine(y, meta, p1, p2, tm=tm_rows)


def kernel(x_prompt, x_sample, cache_k_win, cache_v_win, state_rglru_h, state_conv, c_prompt, c_sample, norm1_w, norm2_w, w_ada, b_ada, w_in, q_norm_w, k_norm_w, attn_sinks, conv_w, conv_b, lru_w_a, lru_b_a, lru_w_x, lru_b_x, lru_lambda, chunk_v_norm_w, chunk_w_s, chunk_b_s, w_branch_a, w_branch_b, w_branch_c, w_out, ffn_w_gate, ffn_w_up, ffn_w_down, moe_w_router, moe_w_gate, moe_w_up, moe_w_down):
    Bp, S, D = x_prompt.shape
    Bs, Ts, _ = x_sample.shape
    L = w_in.shape[0]
    window, n_kv, hd = cache_k_win.shape[2:]
    n_q = attn_sinks.shape[1]
    wa, kvw = n_q * hd, n_kv * hd
    wb = conv_w.shape[-1]
    wc = chunk_v_norm_w.shape[-1]
    chunk = chunk_w_s.shape[-1]
    cwid = conv_w.shape[1]
    n_exp = moe_w_router.shape[-1]
    assert Ts == SUBLANES and S % (Bs * Ts) == 0 and S % chunk == 0 and S % window == 0
    Rp, Rs = Bp * S, Bs * Ts
    R = Rp + Rs
    half = wb // 2
    assert wb == wc and wa % half == 0 and kvw % half == 0 or True
    col_lx = (wa + 2 * kvw) // half
    col_lg = col_lx + 2
    col_cu = col_lg + 2
    col_cv = col_cu + 2
    gate_col = wa + 2 * kvw + 2 * wb + 2 * wc

    W = dict(conv_w=conv_w, conv_b=conv_b, lru_w_a=lru_w_a, lru_b_a=lru_b_a, lru_w_x=lru_w_x, lru_b_x=lru_b_x,
             lru_lambda=lru_lambda, chunk_v_norm_w=chunk_v_norm_w, chunk_w_s=chunk_w_s, chunk_b_s=chunk_b_s,
             w_branch_a=w_branch_a, w_branch_b=w_branch_b, w_branch_c=w_branch_c)

    n_c = Bp + Bs
    n_c_pad = -(-n_c // SUBLANES) * SUBLANES
    c_all = jnp.concatenate([c_prompt, c_sample, jnp.zeros((n_c_pad - n_c, D), F32)], axis=0)
    mod = _ada(c_all, w_ada, b_ada)
    mod_p = jnp.repeat(mod[:, :Bp], SUBLANES, axis=1)
    mod_s = jnp.repeat(mod[:, Bp:n_c], Ts, axis=1)

    x = jnp.concatenate([x_prompt.reshape(Rp, D), x_sample.reshape(Rs, D)], axis=0)
    rn = functools.partial(_resid_norm, seq=S, n_prompt=Bp)
    tm_big = _pick(R, 1408, 128) if R % 128 == 0 else _pick(R, 1408, 16)
    tm_mid = _pick(R, 768, 128) if R % 128 == 0 else _pick(R, 768, 16)
    w_router_p = jnp.pad(moe_w_router, ((0, 0), (0, 0), (0, LANES - n_exp)))
    moe_gate_w = moe_w_gate.reshape((-1,) + moe_w_gate.shape[2:])
    moe_up_w = moe_w_up.reshape((-1,) + moe_w_up.shape[2:])
    moe_down_w = moe_w_down.reshape((-1,) + moe_w_down.shape[2:])
    moe_sub = MOE_SUB_ROWS if TOP_K * R >= n_exp * MOE_SUB_ROWS * MOE_TILE_SUBS else 4 * SUBLANES

    buf_rows = jnp.pad(state_conv, ((0, 0), (0, 0), (Ts - (cwid - 1), 0), (0, 0))).reshape(L, Rs, wb)
    h0_rep = jnp.repeat(state_rglru_h, Ts, axis=1)

    _, h = rn(x, None, mod_p, mod_s, norm1_w, 0, gate_k=None, scale_k=(0, 1), shift_k=(0, 0))
    outs = []
    for l in range(L):
        z = _matmul(h, w_in, l, out_dtype=F32, tm=tm_big, tn=_pick(w_in.shape[-1], 512, half), tk=D)
        attn_o, kwp, vwp = _attn_prompt(z, attn_sinks, q_norm_w, k_norm_w, l, R=R, n_prompt=Bp, seq=S,
                                        n_q=n_q, n_kv=n_kv, hd=hd, window=window)
        attn_o, kws, vws = _attn_sample(attn_o, z, cache_k_win, cache_v_win, attn_sinks, q_norm_w, k_norm_w, l,
                                        n_sample=Bs, T=Ts, n_q=n_q, n_kv=n_kv, hd=hd, window=window)
        lru_o, hp, cp8 = _lru_prompt(z, W, l, R=R, nseq=Bp, T=S, Tb=_pick(S, 512, SUBLANES),
                                     col_x=col_lx, col_g=col_lg, wb=wb)
        lru_o, hs_all = _lru_sample(lru_o, z, h0_rep[l], buf_rows[l], W, l, row0=Rp, T=Ts,
                                    col_x=col_lx, col_g=col_lg, wb=wb)
        chunk_o, vnp = _chunk(None, z, W, l, R=R, row0=0, nsteps=Rp // chunk, nb=1, Tc=chunk,
                              cps=S // chunk, col_u=col_cu, col_v=col_cv, wc=wc)
        chunk_o, vns = _chunk(chunk_o, z, W, l, R=R, row0=Rp, nsteps=1, nb=Bs, Tc=Ts, cps=1,
                              col_u=col_cu, col_v=col_cv, wc=wc)
        m = _merge(attn_o, lru_o, chunk_o, z, W, l, gate_col=gate_col, tm=tm_big, tn=_pick(D, 256, LANES))
        y = _matmul(m, w_out, l, out_dtype=F32, tm=tm_big, tn=_pick(D, 512, LANES), tk=D)
        j = l // 2
        if l % 2 == 0:
            x, h = rn(x, y, mod_p, mod_s, norm2_w, l, gate_k=(l, 2), scale_k=(l, 4), shift_k=(l, 3))
            a = _glu(h, ffn_w_gate, ffn_w_up, j, 1, None, tm=tm_big, tn=_pick(ffn_w_gate.shape[-1], 512, LANES))
            f = _matmul(a, ffn_w_down, j, out_dtype=F32, tm=tm_big, tn=_pick(D, 1024, LANES),
                        tk=_pick(a.shape[1], 1408, LANES))
        else:
            x, h, h_rows = rn(x, y, mod_p, mod_s, norm2_w, l, gate_k=(l, 2), scale_k=(l, 4), shift_k=(l, 3),
                              rows_out=True)
            f = _moe(h, h_rows, w_router_p, moe_gate_w, moe_up_w, moe_down_w, j, n_exp=n_exp, tm_router=tm_mid,
                     tile=MOE_TILE_SUBS * moe_sub, sub=moe_sub, tn=_pick(moe_gate_w.shape[-1], 512, LANES),
                     tk=_pick(moe_gate_w.shape[-1], 1024, LANES), tm_rows=Rs)
        if l + 1 < L:
            x, h = rn(x, f, mod_p, mod_s, norm1_w, l + 1, gate_k=(l, 5), scale_k=(l + 1, 1), shift_k=(l + 1, 0))
        else:
            x, _ = rn(x, f, mod_p, mod_s, None, l, gate_k=(l, 5), scale_k=None, shift_k=None)
        conv_p = cp8[:, SUBLANES - (cwid - 1):, :]
        lx_s = z[Rp:, col_lx * half:col_lx * half + wb].reshape(Bs, Ts, wb)
        conv_s = jnp.concatenate([state_conv[l], lx_s], axis=1)[:, -(cwid - 1):]
        hs = hs_all.reshape(Bs, Ts, wb)[:, Ts - 1]
        outs.append((kwp.reshape(Bp, window, n_kv, hd), vwp.reshape(Bp, window, n_kv, hd), hp.reshape(Bp, wb),
                     conv_p, vnp,
                     kws.reshape(Bs, window, n_kv, hd), vws.reshape(Bs, window, n_kv, hd), hs.reshape(Bs, wb),
                     conv_s, vns.reshape(Bs, Ts, wc)))
    st = [jnp.stack(s) for s in zip(*outs)]
    y_prompt = x[:Rp].reshape(Bp, S, D)
    y_sample = x[Rp:].reshape(Bs, Ts, D)
    return (y_prompt, y_sample, st[0], st[1], st[2], st[3], st[4], st[5], st[6], st[7], st[8], st[9])
```

```python
import functools

import jax
import jax.numpy as jnp
from jax import lax
from jax.experimental import pallas as pl
from jax.experimental.pallas import tpu as pltpu

EPS = 1e-6
LRU_C = 8.0
PAST_LEN = 16384
TOP_K = 2
MOE_SUB_ROWS = 384
MOE_TILE_SUBS = 3
SUBLANES = 8
LANES = 128
V7X_VMEM_LIMIT = 56 * 1024 * 1024

F32 = jnp.float32
BF16 = jnp.bfloat16


def _pick(n, target, mult):
    best = None
    for t in range(mult, min(n, target) + 1, mult):
        if n % t == 0:
            best = t
    assert best is not None, (n, target, mult)
    return best


def _params(*sem):
    return pltpu.CompilerParams(dimension_semantics=sem, vmem_limit_bytes=V7X_VMEM_LIMIT)


def _rms(x, w):
    return x * lax.rsqrt(jnp.mean(x * x, axis=-1, keepdims=True) + EPS) * w


def _ada_kernel(c_ref, w_ref, b_ref, o_ref):
    a = jax.nn.silu(c_ref[...]).astype(BF16)
    o_ref[...] = jnp.dot(a, w_ref[...].astype(BF16), preferred_element_type=F32) + b_ref[...]


def _ada(c_all, w_ada, b_ada):
    L, D, W6 = w_ada.shape
    Bc = c_all.shape[0]
    tn = _pick(W6, 1024, LANES)
    return pl.pallas_call(
        _ada_kernel,
        grid=(L, W6 // tn),
        in_specs=[pl.BlockSpec((Bc, D), lambda l, j: (0, 0)),
                  pl.BlockSpec((None, D, tn), lambda l, j: (l, 0, j)),
                  pl.BlockSpec((None, 1, tn), lambda l, j: (l, 0, j))],
        out_specs=pl.BlockSpec((None, Bc, tn), lambda l, j: (l, 0, j)),
        out_shape=jax.ShapeDtypeStruct((L, Bc, W6), F32),
        compiler_params=_params("parallel", "parallel"),
        name="ada_mod",
    )(c_all, w_ada, b_ada.reshape(L, 1, W6))


def _resid_norm_kernel(*refs, n_prompt_tiles, has_resid, has_norm, has_rows):
    refs = list(refs)
    x_ref = refs.pop(0)
    if has_resid:
        y_ref, gp_ref, gs_ref = refs.pop(0), refs.pop(0), refs.pop(0)
    if has_norm:
        scp_ref, scs_ref, shp_ref, shs_ref, nw_ref = (refs.pop(0) for _ in range(5))
    if has_resid:
        xo_ref = refs.pop(0)
    if has_norm:
        h_ref = refs.pop(0)
    if has_rows:
        hrow_ref = refs.pop(0)
    i = pl.program_id(0)

    def body(prompt):
        mod = (lambda p, s: p[0:1, :]) if prompt else (lambda p, s: s[...])
        x = x_ref[...]
        if has_resid:
            x = x + mod(gp_ref, gs_ref) * y_ref[...]
            xo_ref[...] = x
        if has_norm:
            hn = _rms(x, nw_ref[...])
            h = hn * (1.0 + mod(scp_ref, scs_ref)) + mod(shp_ref, shs_ref)
            h_ref[...] = h.astype(h_ref.dtype)
            if has_rows:
                for c in range(hrow_ref.shape[1]):
                    hrow_ref[:, c, :] = h[:, c * LANES:(c + 1) * LANES]

    pl.when(i < n_prompt_tiles)(lambda: body(True))
    pl.when(i >= n_prompt_tiles)(lambda: body(False))


def _resid_norm(x, y, mod_p, mod_s, norm_w, l, *, gate_k, scale_k, shift_k, seq, n_prompt, rows_out=False):
    R, D = x.shape
    TE = mod_s.shape[1]
    tpb = seq // TE
    npt = n_prompt * tpb
    has_resid = y is not None
    has_norm = norm_w is not None
    row = pl.BlockSpec((TE, D), lambda i: (i, 0))

    def mp(lk):
        return pl.BlockSpec((None, SUBLANES, D), lambda i: (lk[0], jnp.minimum(i // tpb, n_prompt - 1), lk[1]))

    def ms(lk):
        return pl.BlockSpec((None, TE, D), lambda i: (lk[0], 0, lk[1]))

    args, specs, outs, out_specs = [x], [row], [], []
    if has_resid:
        args += [y, mod_p, mod_s]
        specs += [row, mp(gate_k), ms(gate_k)]
        outs.append(jax.ShapeDtypeStruct((R, D), F32))
        out_specs.append(row)
    if has_norm:
        args += [mod_p, mod_s, mod_p, mod_s, norm_w.reshape(norm_w.shape[0], 1, D)]
        specs += [mp(scale_k), ms(scale_k), mp(shift_k), ms(shift_k),
                  pl.BlockSpec((None, 1, D), lambda i: (l, 0, 0))]
        outs.append(jax.ShapeDtypeStruct((R, D), BF16))
        out_specs.append(row)
    if rows_out:
        outs.append(jax.ShapeDtypeStruct((R, D // LANES, LANES), F32))
        out_specs.append(pl.BlockSpec((TE, D // LANES, LANES), lambda i: (i, 0, 0)))
    res = pl.pallas_call(
        functools.partial(_resid_norm_kernel, n_prompt_tiles=npt, has_resid=has_resid, has_norm=has_norm,
                          has_rows=rows_out),
        grid=(R // TE,),
        in_specs=specs, out_specs=out_specs, out_shape=outs,
        compiler_params=_params("parallel"),
        name="resid_norm",
    )(*args)
    res = list(res)
    x_new = res.pop(0) if has_resid else x
    h = res.pop(0) if has_norm else None
    if rows_out:
        return x_new, h, res.pop(0)
    return x_new, h


def _mm_kernel(a_ref, w_ref, o_ref, acc_ref, *, nk):
    part = jnp.dot(a_ref[...], w_ref[...].astype(BF16), preferred_element_type=F32)
    if nk == 1:
        o_ref[...] = part.astype(o_ref.dtype)
        return
    k = pl.program_id(2)

    @pl.when(k == 0)
    def _():
        acc_ref[...] = part

    @pl.when(k > 0)
    def _():
        acc_ref[...] += part

    @pl.when(k == nk - 1)
    def _():
        o_ref[...] = acc_ref[...].astype(o_ref.dtype)


def _matmul(a, w, l, *, out_dtype, tm, tn, tk):
    R, K = a.shape
    N = w.shape[-1]
    nk = K // tk
    acc_shape = (tm, tn) if nk > 1 else (SUBLANES, LANES)
    return pl.pallas_call(
        functools.partial(_mm_kernel, nk=nk),
        grid=(R // tm, N // tn, nk),
        in_specs=[pl.BlockSpec((tm, tk), lambda i, j, k: (i, k)),
                  pl.BlockSpec((None, tk, tn), lambda i, j, k: (l, k, j))],
        out_specs=pl.BlockSpec((tm, tn), lambda i, j, k: (i, j)),
        out_shape=jax.ShapeDtypeStruct((R, N), out_dtype),
        scratch_shapes=[pltpu.VMEM(acc_shape, F32)],
        compiler_params=_params("parallel", "parallel", "arbitrary"),
        name="matmul",
    )(a, w)


def _softmax_sink_pv(s, sink, v):
    m = jnp.maximum(jnp.max(s, axis=-1, keepdims=True), sink)
    e = jnp.exp(s - m)
    den = jnp.sum(e, axis=-1, keepdims=True) + jnp.exp(sink - m)
    return jnp.dot((e / den).astype(BF16), v, preferred_element_type=F32)


def _attn_prompt_kernel(sink_ref, q_ref, kp_ref, kc_ref, vp_ref, vc_ref, qw_ref, kw_ref,
                        o_ref, kwin_ref, vwin_ref, *, l, n_kv, q_per_kv, hd, window, nblk):
    i = pl.program_id(1)
    blk = q_ref.shape[0]
    scale = hd ** -0.5
    rows_q = q_per_kv * blk
    r = lax.broadcasted_iota(jnp.int32, (rows_q, 2 * blk), 0) % blk
    c = lax.broadcasted_iota(jnp.int32, (rows_q, 2 * blk), 1)
    dist = r + blk - c
    ok = (dist >= 0) & (dist <= window) & ((c >= blk) | (i > 0))
    hrow = lax.broadcasted_iota(jnp.int32, (rows_q, 1), 0) // blk
    qw = qw_ref[...]
    kw = kw_ref[...]
    for g in range(n_kv):
        sl = slice(g * hd, (g + 1) * hd)
        kcn = _rms(kc_ref[:, sl], kw)
        kcat = jnp.concatenate([_rms(kp_ref[:, sl], kw), kcn], axis=0).astype(BF16)
        vcat = jnp.concatenate([vp_ref[:, sl], vc_ref[:, sl]], axis=0).astype(BF16)
        heads = [g * q_per_kv + h for h in range(q_per_kv)]
        qs = jnp.concatenate([_rms(q_ref[:, hh * hd:(hh + 1) * hd], qw) for hh in heads], axis=0).astype(BF16)
        s = lax.dot_general(qs, kcat, (((1,), (1,)), ((), ())), preferred_element_type=F32) * scale
        s = jnp.where(ok, s, -1e30)
        sink = jnp.zeros((rows_q, 1), F32)
        for h, hh in enumerate(heads):
            sink = jnp.where(hrow == h, sink_ref[l, hh], sink)
        o = _softmax_sink_pv(s, sink, vcat)
        for h, hh in enumerate(heads):
            o_ref[:, hh * hd:(hh + 1) * hd] = o[h * blk:(h + 1) * blk, :].astype(o_ref.dtype)

        @pl.when(i == nblk - 1)
        def _():
            kwin_ref[:, sl] = kcn
            vwin_ref[:, sl] = vc_ref[:, sl]


def _attn_prompt(z, sinks, q_norm_w, k_norm_w, l, *, R, n_prompt, seq, n_q, n_kv, hd, window):
    blk = window
    nblk = seq // blk
    wa = n_q * hd
    kvw = n_kv * hd
    kcol = wa // kvw
    vcol = kcol + 1

    def cur(col):
        return pl.BlockSpec((blk, kvw), lambda b, i: (b * nblk + i, col))

    def prev(col):
        return pl.BlockSpec((blk, kvw), lambda b, i: (b * nblk + jnp.maximum(i - 1, 0), col))

    L = q_norm_w.shape[0]
    nw = pl.BlockSpec((None, 1, hd), lambda b, i: (l, 0, 0))
    return pl.pallas_call(
        functools.partial(_attn_prompt_kernel, l=l, n_kv=n_kv, q_per_kv=n_q // n_kv, hd=hd,
                          window=window, nblk=nblk),
        grid=(n_prompt, nblk),
        in_specs=[pl.BlockSpec(memory_space=pltpu.SMEM),
                  pl.BlockSpec((blk, wa), lambda b, i: (b * nblk + i, 0)),
                  prev(kcol), cur(kcol), prev(vcol), cur(vcol), nw, nw],
        out_specs=[pl.BlockSpec((blk, wa), lambda b, i: (b * nblk + i, 0)),
                   pl.BlockSpec((None, blk, kvw), lambda b, i: (b, 0, 0)),
                   pl.BlockSpec((None, blk, kvw), lambda b, i: (b, 0, 0))],
        out_shape=[jax.ShapeDtypeStruct((R, wa), BF16),
                   jax.ShapeDtypeStruct((n_prompt, blk, kvw), F32),
                   jax.ShapeDtypeStruct((n_prompt, blk, kvw), F32)],
        compiler_params=_params("parallel", "arbitrary"),
        name="attn_prompt",
    )(sinks, z, z, z, z, z, q_norm_w.reshape(L, 1, hd), k_norm_w.reshape(L, 1, hd))


def _attn_sample_kernel(sink_ref, o_in_ref, q_ref, kn_ref, vn_ref, ck_ref, cv_ref, qw_ref, kw_ref,
                        o_ref, kwin_ref, vwin_ref, keys_s, vals_s, ostage,
                        *, l, nb, T, n_kv, q_per_kv, hd, window):
    del o_in_ref
    W = ck_ref.shape[1]
    scale = hd ** -0.5
    rows_q = q_per_kv * T
    r = lax.broadcasted_iota(jnp.int32, (rows_q, 2 * W), 0) % T
    c = lax.broadcasted_iota(jnp.int32, (rows_q, 2 * W), 1)
    q_pos = PAST_LEN + r
    k_pos = jnp.where(c < W, PAST_LEN - W + c, PAST_LEN + c - W)
    dist = q_pos - k_pos
    ok = (dist >= 0) & (dist <= window) & (k_pos >= 0) & (c < W + T)
    hrow = lax.broadcasted_iota(jnp.int32, (rows_q, 1), 0) // T
    qw = qw_ref[...]
    kw = kw_ref[...]
    keys_s[...] = jnp.zeros_like(keys_s)
    vals_s[...] = jnp.zeros_like(vals_s)

    def step(b, carry):
        rows = pl.ds(pl.multiple_of(b * T, T), T)
        for g in range(n_kv):
            sl = slice(g * hd, (g + 1) * hd)
            ck = ck_ref[b, :, sl]
            cv = cv_ref[b, :, sl]
            knn = _rms(kn_ref[rows, sl], kw)
            vnn = vn_ref[rows, sl]
            keys_s[0:W, :] = ck
            keys_s[W:W + T, :] = knn
            vals_s[0:W, :] = cv
            vals_s[W:W + T, :] = vnn
            qs = jnp.concatenate(
                [_rms(q_ref[rows, (g * q_per_kv + h) * hd:(g * q_per_kv + h + 1) * hd], qw)
                 for h in range(q_per_kv)], axis=0).astype(BF16)
            s = lax.dot_general(qs, keys_s[...].astype(BF16), (((1,), (1,)), ((), ())),
                                preferred_element_type=F32) * scale
            s = jnp.where(ok, s, -1e30)
            sink = jnp.zeros((rows_q, 1), F32)
            for h in range(q_per_kv):
                sink = jnp.where(hrow == h, sink_ref[l, g * q_per_kv + h], sink)
            o = _softmax_sink_pv(s, sink, vals_s[...].astype(BF16))
            for h in range(q_per_kv):
                head = g * q_per_kv + h
                ostage[rows, head * hd:(head + 1) * hd] = o[h * T:(h + 1) * T, :]
            kwin_ref[b, 0:W - T, sl] = ck[T:, :]
            kwin_ref[b, W - T:W, sl] = knn
            vwin_ref[b, 0:W - T, sl] = cv[T:, :]
            vwin_ref[b, W - T:W, sl] = vnn
        return carry

    lax.fori_loop(0, nb, step, 0)
    o_ref[...] = ostage[...].astype(o_ref.dtype)


def _attn_sample(attn_o, z, cache_k, cache_v, sinks, q_norm_w, k_norm_w, l, *, n_sample, T, n_q, n_kv, hd,
                 window):
    R, wa = attn_o.shape
    kvw = n_kv * hd
    rows = n_sample * T
    blk_i = (R - rows) // rows
    kcol = wa // kvw
    L, nb, W = cache_k.shape[:3]
    ck = cache_k.reshape(L, nb, W, kvw)
    cv = cache_v.reshape(L, nb, W, kvw)
    nw = pl.BlockSpec((None, 1, hd), lambda i: (l, 0, 0))
    cache = pl.BlockSpec((None, nb, W, kvw), lambda i: (l, 0, 0, 0))
    win = pl.BlockSpec((nb, W, kvw), lambda i: (0, 0, 0))
    return pl.pallas_call(
        functools.partial(_attn_sample_kernel, l=l, nb=nb, T=T, n_kv=n_kv, q_per_kv=n_q // n_kv, hd=hd,
                          window=window),
        grid=(1,),
        in_specs=[pl.BlockSpec(memory_space=pltpu.SMEM),
                  pl.BlockSpec(memory_space=pl.ANY),
                  pl.BlockSpec((rows, wa), lambda i: (blk_i, 0)),
                  pl.BlockSpec((rows, kvw), lambda i: (blk_i, kcol)),
                  pl.BlockSpec((rows, kvw), lambda i: (blk_i, kcol + 1)),
                  cache, cache, nw, nw],
        out_specs=[pl.BlockSpec((rows, wa), lambda i: (blk_i, 0)), win, win],
        out_shape=[jax.ShapeDtypeStruct((R, wa), BF16),
                   jax.ShapeDtypeStruct((nb, W, kvw), F32),
                   jax.ShapeDtypeStruct((nb, W, kvw), F32)],
        scratch_shapes=[pltpu.VMEM((2 * W, hd), F32), pltpu.VMEM((2 * W, hd), F32),
                        pltpu.VMEM((rows, wa), F32)],
        input_output_aliases={1: 0},
        compiler_params=_params("arbitrary"),
        name="attn_sample",
    )(sinks, attn_o, z, z, z, ck, cv, q_norm_w.reshape(-1, 1, hd), k_norm_w.reshape(-1, 1, hd))


def _lru_gates(xc, wa_ref, ba_ref, wx_ref, bx_ref, lam_ref, a_s, u_s, *, nblk, lb):
    xcb = xc.astype(BF16)
    for n in range(nblk):
        sl = slice(n * lb, (n + 1) * lb)
        ra = jnp.dot(xcb[:, sl], wa_ref[n].astype(BF16), preferred_element_type=F32) + ba_ref[:, sl]
        rx = jnp.dot(xcb[:, sl], wx_ref[n].astype(BF16), preferred_element_type=F32) + bx_ref[:, sl]
        log_a = -LRU_C * jax.nn.sigmoid(ra) * jax.nn.softplus(-lam_ref[:, sl])
        a_s[:, sl] = jnp.exp(log_a)
        th = jnp.tanh(log_a)
        u_s[:, sl] = jnp.sqrt(-2.0 * th / (1.0 - th)) * jax.nn.sigmoid(rx) * xc[:, sl]

def _tile_scan(a, u, T):
    t = lax.broadcasted_iota(jnp.int32, (a.shape[0], 1), 0) % T
    d = 1
    while d < T:
        keep = t >= d
        u = jnp.where(keep, a * pltpu.roll(u, d, axis=0) + u, u)
        a = jnp.where(keep, a * pltpu.roll(a, d, axis=0), a)
        d *= 2
    return a, u


def _lru_prompt_kernel(x_ref, g_ref, cw_ref, cb_ref, wa_ref, ba_ref, wx_ref, bx_ref, lam_ref,
                       o_ref, hT_ref, cnew_ref, xp_s, a_s, u_s, h_s, *, Tb, ntb, cw, nblk, lb):
    tb = pl.program_id(2)
    P = SUBLANES

    @pl.when(tb == 0)
    def _():
        xp_s[0:P, :] = jnp.zeros((P, xp_s.shape[1]), F32)
        h_s[...] = jnp.zeros_like(h_s)

    xp_s[P:P + Tb, :] = x_ref[...]
    xc = cb_ref[...]
    for j in range(cw):
        off = P - (cw - 1) + j
        xc = xc + xp_s[off:off + Tb, :] * cw_ref[j:j + 1, :]
    _lru_gates(xc, wa_ref, ba_ref, wx_ref, bx_ref, lam_ref, a_s, u_s, nblk=nblk, lb=lb)

    a, u = _tile_scan(a_s[...], u_s[...], P)
    a_s[...] = a
    u_s[...] = u

    def step(k, h):
        rr = pl.ds(pl.multiple_of(k * P, P), P)
        hs = u_s[rr, :] + a_s[rr, :] * h
        u_s[rr, :] = hs
        return hs[P - 1:P, :]

    h = lax.fori_loop(0, Tb // P, step, h_s[...])
    h_s[...] = h
    o_ref[...] = (jax.nn.gelu(g_ref[...]) * u_s[...]).astype(o_ref.dtype)
    tail = xp_s[Tb:Tb + P, :]
    xp_s[0:P, :] = tail

    @pl.when(tb == ntb - 1)
    def _():
        hT_ref[...] = h
        cnew_ref[...] = tail


def _lru_sample_kernel(o_in_ref, x_ref, g_ref, h0_ref, buf_ref, cw_ref, cb_ref, wa_ref, ba_ref, wx_ref, bx_ref,
                       lam_ref, o_ref, hs_ref, a_s, u_s, *, T, cw, nblk, lb):
    del o_in_ref
    rows = x_ref.shape[0]
    t = lax.broadcasted_iota(jnp.int32, (rows, 1), 0) % T
    x = x_ref[...]
    buf = buf_ref[...]
    xc = cb_ref[...] + x * cw_ref[cw - 1:cw, :]
    for s in range(1, cw):
        xs = jnp.where(t >= s, pltpu.roll(x, s, axis=0), pltpu.roll(buf, rows - (T - s), axis=0))
        xc = xc + xs * cw_ref[cw - 1 - s:cw - s, :]
    _lru_gates(xc, wa_ref, ba_ref, wx_ref, bx_ref, lam_ref, a_s, u_s, nblk=nblk, lb=lb)
    a, u = _tile_scan(a_s[...], u_s[...], T)
    hs = u + a * h0_ref[...]
    hs_ref[...] = hs
    o_ref[...] = (jax.nn.gelu(g_ref[...]) * hs).astype(o_ref.dtype)


def _lru_weight_args(W, l, wb, index):
    ch = wb // 2
    nlb, lb = W['lru_w_a'].shape[1:3]
    nblk = nlb // 2
    cw = W['conv_w'].shape[1]

    def vec(name):
        return W[name].reshape(W[name].shape[0], 1, wb), pl.BlockSpec((None, 1, ch), index(lambda hf: (l, 0, hf)))

    def blkw(name):
        return W[name], pl.BlockSpec((None, nblk, lb, lb), index(lambda hf: (l, hf, 0, 0)))

    pairs = [(W['conv_w'], pl.BlockSpec((None, cw, ch), index(lambda hf: (l, 0, hf)))),
             vec('conv_b'), blkw('lru_w_a'), vec('lru_b_a'), blkw('lru_w_x'), vec('lru_b_x'), vec('lru_lambda')]
    return [p[0] for p in pairs], [p[1] for p in pairs], dict(cw=cw, nblk=nblk, lb=lb)


def _lru_prompt(z, W, l, *, R, nseq, T, Tb, col_x, col_g, wb):
    ch = wb // 2
    ntb = T // Tb
    wargs, wspecs, kw = _lru_weight_args(W, l, wb, lambda f: (lambda s, hf, t: f(hf)))

    def rowblk(col):
        return pl.BlockSpec((Tb, ch), lambda s, hf, t: (s * ntb + t, col + hf))

    return pl.pallas_call(
        functools.partial(_lru_prompt_kernel, Tb=Tb, ntb=ntb, **kw),
        grid=(nseq, 2, ntb),
        in_specs=[rowblk(col_x), rowblk(col_g)] + wspecs,
        out_specs=[pl.BlockSpec((Tb, ch), lambda s, hf, t: (s * ntb + t, hf)),
                   pl.BlockSpec((None, 1, ch), lambda s, hf, t: (s, 0, hf)),
                   pl.BlockSpec((None, SUBLANES, ch), lambda s, hf, t: (s, 0, hf))],
        out_shape=[jax.ShapeDtypeStruct((R, wb), BF16),
                   jax.ShapeDtypeStruct((nseq, 1, wb), F32),
                   jax.ShapeDtypeStruct((nseq, SUBLANES, wb), F32)],
        scratch_shapes=[pltpu.VMEM((SUBLANES + Tb, ch), F32), pltpu.VMEM((Tb, ch), F32),
                        pltpu.VMEM((Tb, ch), F32), pltpu.VMEM((1, ch), F32)],
        compiler_params=_params("parallel", "parallel", "arbitrary"),
        name="lru_prompt",
    )(z, z, *wargs)


def _lru_sample(lru_o, z, h0_rep, buf_rows, W, l, *, row0, T, col_x, col_g, wb):
    ch = wb // 2
    rows = h0_rep.shape[0]
    rb = row0 // rows
    wargs, wspecs, kw = _lru_weight_args(W, l, wb, lambda f: (lambda hf: f(hf)))
    st = pl.BlockSpec((rows, ch), lambda hf: (0, hf))
    return pl.pallas_call(
        functools.partial(_lru_sample_kernel, T=T, **kw),
        grid=(2,),
        in_specs=[pl.BlockSpec(memory_space=pl.ANY),
                  pl.BlockSpec((rows, ch), lambda hf: (rb, col_x + hf)),
                  pl.BlockSpec((rows, ch), lambda hf: (rb, col_g + hf)), st, st] + wspecs,
        out_specs=[pl.BlockSpec((rows, ch), lambda hf: (rb, hf)), st],
        out_shape=[jax.ShapeDtypeStruct(lru_o.shape, BF16), jax.ShapeDtypeStruct((rows, wb), F32)],
        scratch_shapes=[pltpu.VMEM((rows, ch), F32), pltpu.VMEM((rows, ch), F32)],
        input_output_aliases={0: 0},
        compiler_params=_params("parallel"),
        name="lru_sample",
    )(lru_o, z, z, h0_rep, buf_rows, *wargs)


def _chunk_kernel(*refs, nb, Tc, ngroups, gw, cps, aliased):
    refs = list(refs)
    if aliased:
        refs.pop(0)
    ulo_ref, uhi_ref, vlo_ref, vhi_ref, nw_ref, ws_ref, bst_ref, o_ref, vn_ref = refs
    rows = nb * Tc
    half = vlo_ref.shape[1]
    wc = 2 * half
    v_lo = jax.nn.gelu(vlo_ref[...])
    v_hi = jax.nn.gelu(vhi_ref[...])
    ms = (jnp.sum(v_lo * v_lo, axis=-1, keepdims=True) + jnp.sum(v_hi * v_hi, axis=-1, keepdims=True)) / wc
    inv = lax.rsqrt(ms + EPS)
    vn_halves = (v_lo * inv * nw_ref[:, 0:half], v_hi * inv * nw_ref[:, half:wc])
    u_halves = (ulo_ref, uhi_ref)

    r = lax.broadcasted_iota(jnp.int32, (rows, rows), 0)
    c = lax.broadcasted_iota(jnp.int32, (rows, rows), 1)
    mask = (r // Tc == c // Tc) & (c % Tc <= r % Tc)
    if nb > 1:
        sel = (lax.broadcasted_iota(jnp.int32, (ws_ref.shape[2], rows), 0)
               == lax.broadcasted_iota(jnp.int32, (ws_ref.shape[2], rows), 1) % Tc).astype(BF16)
        bias_rows = jnp.broadcast_to(bst_ref[0:Tc, :][None], (nb, Tc, bst_ref.shape[1])).reshape(rows, -1)
    else:
        bias_rows = bst_ref[0:Tc, :]
    gph = half // gw
    for g in range(ngroups):
        hf, gi = divmod(g, gph)
        sl = slice(gi * gw, (gi + 1) * gw)
        if nb > 1:
            t1 = jnp.broadcast_to(ws_ref[g, 0:Tc, :][None], (nb, Tc, ws_ref.shape[2])).reshape(rows, -1)
            wfull = jnp.dot(t1.astype(BF16), sel, preferred_element_type=F32)
        else:
            wfull = ws_ref[g, 0:Tc, 0:Tc]
        wm = jnp.where(mask, wfull, 0.0).astype(BF16)
        mixed = jnp.dot(wm, vn_halves[hf][:, sl].astype(BF16), preferred_element_type=F32)
        mixed = mixed + bias_rows[:, g:g + 1]
        o_ref[:, g * gw:(g + 1) * gw] = (jax.nn.gelu(u_halves[hf][:, sl]) * mixed).astype(o_ref.dtype)

    if cps == 1:
        vn_ref[:, 0:half] = vn_halves[0]
        vn_ref[:, half:wc] = vn_halves[1]
    else:
        @pl.when(pl.program_id(0) % cps == cps - 1)
        def _():
            vn_ref[:, 0:half] = vn_halves[0]
            vn_ref[:, half:wc] = vn_halves[1]


def _chunk(chunk_o, z, W, l, *, R, row0, nsteps, nb, Tc, cps, col_u, col_v, wc):
    half = wc // 2
    ngroups, chunk = W['chunk_w_s'].shape[1:3]
    gw = wc // ngroups
    rows = nb * Tc
    rb0 = row0 // rows
    aliased = chunk_o is not None

    def rowblk(col):
        return pl.BlockSpec((rows, half), lambda i: (rb0 + i, col))

    args = [z, z, z, z, W['chunk_v_norm_w'].reshape(-1, 1, wc), W['chunk_w_s'],
            jnp.swapaxes(W['chunk_b_s'], 1, 2)]
    specs = [rowblk(col_u), rowblk(col_u + 1), rowblk(col_v), rowblk(col_v + 1),
             pl.BlockSpec((None, 1, wc), lambda i: (l, 0, 0)),
             pl.BlockSpec((None, ngroups, chunk, chunk), lambda i: (l, 0, 0, 0)),
             pl.BlockSpec((None, chunk, ngroups), lambda i: (l, 0, 0))]
    io_alias = {}
    if aliased:
        args.insert(0, chunk_o)
        specs.insert(0, pl.BlockSpec(memory_space=pl.ANY))
        io_alias = {0: 0}
    return pl.pallas_call(
        functools.partial(_chunk_kernel, nb=nb, Tc=Tc, ngroups=ngroups, gw=gw, cps=cps, aliased=aliased),
        grid=(nsteps,),
        in_specs=specs,
        out_specs=[pl.BlockSpec((rows, wc), lambda i: (rb0 + i, 0)),
                   pl.BlockSpec((None, rows, wc), lambda i: (i // cps, 0, 0))],
        out_shape=[jax.ShapeDtypeStruct((R, wc), BF16),
                   jax.ShapeDtypeStruct((nsteps // cps, rows, wc), F32)],
        input_output_aliases=io_alias,
        compiler_params=_params("arbitrary"),
        name="chunk_sample" if aliased else "chunk_prompt",
    )(*args)


def _merge_kernel(a_ref, b_ref, c_ref, wa_ref, wb_ref, wc_ref, ga_ref, gb_ref, gc_ref, o_ref):
    def branch(x_ref, w_ref, g_ref):
        y = jnp.dot(x_ref[...], w_ref[...].astype(BF16), preferred_element_type=F32)
        return jax.nn.sigmoid(g_ref[...]) * y

    m = branch(a_ref, wa_ref, ga_ref) + branch(b_ref, wb_ref, gb_ref) + branch(c_ref, wc_ref, gc_ref)
    o_ref[...] = m.astype(o_ref.dtype)


def _merge(attn_o, lru_o, chunk_o, z, W, l, *, gate_col, tm, tn):
    R = attn_o.shape[0]
    D = W['w_branch_a'].shape[-1]
    g0 = gate_col // tn
    gstep = D // tn

    def xin(a):
        return pl.BlockSpec((tm, a.shape[1]), lambda i, j: (i, 0))

    def win(w):
        return pl.BlockSpec((None, w.shape[1], tn), lambda i, j: (l, 0, j))

    def gin(k):
        return pl.BlockSpec((tm, tn), lambda i, j: (i, g0 + k * gstep + j))

    return pl.pallas_call(
        _merge_kernel,
        grid=(R // tm, D // tn),
        in_specs=[xin(attn_o), xin(lru_o), xin(chunk_o),
                  win(W['w_branch_a']), win(W['w_branch_b']), win(W['w_branch_c']),
                  gin(0), gin(1), gin(2)],
        out_specs=pl.BlockSpec((tm, tn), lambda i, j: (i, j)),
        out_shape=jax.ShapeDtypeStruct((R, D), BF16),
        compiler_params=_params("parallel", "parallel"),
        name="merge",
    )(attn_o, lru_o, chunk_o, W['w_branch_a'], W['w_branch_b'], W['w_branch_c'], z, z, z)


def _glu_kernel(*refs, gated):
    if gated:
        h_ref, wg_ref, wu_ref, gate_ref, o_ref = refs
    else:
        h_ref, wg_ref, wu_ref, o_ref = refs
    h = h_ref[...]
    g = jnp.dot(h, wg_ref[...].astype(BF16), preferred_element_type=F32)
    u = jnp.dot(h, wu_ref[...].astype(BF16), preferred_element_type=F32)
    a = jax.nn.silu(g) * u
    if gated:
        e = pl.program_id(1)
        gate = gate_ref[...]
        lane = lax.broadcasted_iota(jnp.int32, gate.shape, 1)
        a = a * jnp.sum(jnp.where(lane == e, gate, 0.0), axis=-1, keepdims=True)
    o_ref[...] = a.astype(o_ref.dtype)


def _glu(h, w_gate, w_up, l0, n_exp, gate, *, tm, tn):
    R, D = h.shape
    F = w_gate.shape[-1]
    nf = F // tn
    gated = gate is not None
    wspec = pl.BlockSpec((None, D, tn), lambda i, e, j: (l0 + e, 0, j))
    args = [h, w_gate, w_up]
    specs = [pl.BlockSpec((tm, D), lambda i, e, j: (i, 0)), wspec, wspec]
    if gated:
        args.append(gate)
        specs.append(pl.BlockSpec((tm, gate.shape[1]), lambda i, e, j: (i, 0)))
    return pl.pallas_call(
        functools.partial(_glu_kernel, gated=gated),
        grid=(R // tm, n_exp, nf),
        in_specs=specs,
        out_specs=pl.BlockSpec((tm, tn), lambda i, e, j: (i, e * nf + j)),
        out_shape=jax.ShapeDtypeStruct((R, n_exp * F), BF16),
        compiler_params=_params("parallel", "parallel", "parallel"),
        name="glu",
    )(*args)


META_E1, META_E2, META_W1, META_W2, META_R1, META_R2 = range(6)


def _router_kernel(h_ref, w_ref, meta_ref, cnt_ref, carry, *, n_exp):
    i = pl.program_id(0)

    @pl.when(i == 0)
    def _():
        carry[...] = jnp.zeros_like(carry)

    logits = jnp.dot(h_ref[...], w_ref[...].astype(BF16), preferred_element_type=F32)
    tm = logits.shape[0]
    lane = lax.broadcasted_iota(jnp.int32, logits.shape, 1).astype(F32)
    big = float(logits.shape[1])
    l0 = jnp.where(lane < n_exp, logits, -jnp.inf)
    m1 = jnp.max(l0, axis=-1, keepdims=True)
    i1 = jnp.min(jnp.where(l0 == m1, lane, big), axis=-1, keepdims=True)
    l1 = jnp.where(lane == i1, -jnp.inf, l0)
    m2 = jnp.max(l1, axis=-1, keepdims=True)
    i2 = jnp.min(jnp.where(l1 == m2, lane, big), axis=-1, keepdims=True)
    e2 = jnp.exp(m2 - m1)
    den = 1.0 + e2
    hit = ((lane == i1) | (lane == i2)).astype(F32)
    earlier = (lax.broadcasted_iota(jnp.int32, (tm, tm), 0) > lax.broadcasted_iota(jnp.int32, (tm, tm), 1))
    rank = jnp.dot(earlier.astype(BF16), hit.astype(BF16), preferred_element_type=F32) + carry[...]
    r1 = jnp.sum(jnp.where(lane == i1, rank, 0.0), axis=-1, keepdims=True)
    r2 = jnp.sum(jnp.where(lane == i2, rank, 0.0), axis=-1, keepdims=True)
    meta = jnp.zeros_like(logits)
    for k, v in ((META_E1, i1), (META_E2, i2), (META_W1, 1.0 / den), (META_W2, e2 / den), (META_R1, r1),
                 (META_R2, r2)):
        meta = jnp.where(lane == k, v, meta)
    meta_ref[...] = meta
    carry[...] += jnp.sum(hit, axis=0, keepdims=True)
    cnt_ref[...] = carry[...]


def _router(h, w_router_padded, j, *, n_exp, tm):
    R, D = h.shape
    NP = w_router_padded.shape[-1]
    return pl.pallas_call(
        functools.partial(_router_kernel, n_exp=n_exp),
        grid=(R // tm,),
        in_specs=[pl.BlockSpec((tm, D), lambda i: (i, 0)),
                  pl.BlockSpec((None, D, NP), lambda i: (j, 0, 0))],
        out_specs=[pl.BlockSpec((tm, NP), lambda i: (i, 0)), pl.BlockSpec((1, NP), lambda i: (0, 0))],
        out_shape=[jax.ShapeDtypeStruct((R, NP), F32), jax.ShapeDtypeStruct((1, NP), F32)],
        scratch_shapes=[pltpu.VMEM((1, NP), F32)],
        compiler_params=_params("arbitrary"),
        name="router",
    )(h, w_router_padded)


def _row_copies_wait(src_like, dst_like, sem, n):
    for _ in range(n):
        pltpu.make_async_copy(src_like, dst_like, sem).wait()


def _dispatch_kernel(p1_ref, p2_ref, h_ref, xs_ref, sem):
    tm = h_ref.shape[0]
    i0 = pl.program_id(0) * tm

    def body(i, c):
        pltpu.make_async_copy(h_ref.at[i], xs_ref.at[p1_ref[i0 + i]], sem).start()
        pltpu.make_async_copy(h_ref.at[i], xs_ref.at[p2_ref[i0 + i]], sem).start()
        return c

    lax.fori_loop(0, tm, body, 0)
    _row_copies_wait(h_ref, xs_ref.at[pl.ds(0, tm)], sem, TOP_K)


def _dispatch(h_rows, p1, p2, *, n_slots, tm):
    R, C, _ = h_rows.shape
    return pl.pallas_call(
        _dispatch_kernel,
        grid_spec=pltpu.PrefetchScalarGridSpec(
            num_scalar_prefetch=2, grid=(R // tm,),
            in_specs=[pl.BlockSpec((tm, C, LANES), lambda i, p1, p2: (i, 0, 0))],
            out_specs=pl.BlockSpec(memory_space=pl.ANY),
            scratch_shapes=[pltpu.SemaphoreType.DMA]),
        out_shape=jax.ShapeDtypeStruct((n_slots, C, LANES), F32),
        compiler_params=_params("arbitrary"),
        name="moe_dispatch",
    )(p1, p2, h_rows)


def _per_live_rows(nv, tile, sub, body):
    for m in range(sub, tile + 1, sub):
        pl.when((nv > m - sub) & (nv <= m))(functools.partial(body, m))


def _moe_glu_kernel(te_ref, nv_ref, nu_ref, xs_ref, wg_ref, wu_ref, a_ref, xb_s, *, sub):
    t = pl.program_id(0)
    j = pl.program_id(1)
    nv = nv_ref[t]
    tile, nchunk = xs_ref.shape[0], xs_ref.shape[1]

    @pl.when(t < nu_ref[0])
    def _():
        wg = wg_ref[...].astype(BF16)
        wu = wu_ref[...].astype(BF16)

        def body(m):
            rows = slice(0, m)

            @pl.when(j == 0)
            def _():
                live = lax.broadcasted_iota(jnp.int32, (m, 1), 0) < nv
                for c in range(nchunk):
                    xb_s[rows, c * LANES:(c + 1) * LANES] = jnp.where(live, xs_ref[rows, c, :], 0.0).astype(BF16)

            x = xb_s[rows, :]
            g = jnp.dot(x, wg, preferred_element_type=F32)
            u = jnp.dot(x, wu, preferred_element_type=F32)
            a_ref[rows, :] = (jax.nn.silu(g) * u).astype(a_ref.dtype)
            if m < tile:
                a_ref[m:tile, :] = jnp.zeros((tile - m, a_ref.shape[1]), a_ref.dtype)

        _per_live_rows(nv, tile, sub, body)


def _moe_down_kernel(te_ref, nv_ref, nu_ref, a_ref, wd_ref, y_ref, acc_s, *, sub, nk):
    t = pl.program_id(0)
    k = pl.program_id(1)
    nv = nv_ref[t]
    tile, nchunk = y_ref.shape[0], y_ref.shape[1]

    @pl.when(t < nu_ref[0])
    def _():
        wd = wd_ref[...].astype(BF16)

        def body(m):
            rows = slice(0, m)
            part = jnp.dot(a_ref[rows, :], wd, preferred_element_type=F32)

            @pl.when(k == 0)
            def _():
                acc_s[rows, :] = part

            @pl.when(k > 0)
            def _():
                acc_s[rows, :] += part

            @pl.when(k == nk - 1)
            def _():
                for c in range(nchunk):
                    y_ref[rows, c, :] = acc_s[rows, c * LANES:(c + 1) * LANES]
                if m < tile:
                    y_ref[m:tile, :, :] = jnp.zeros((tile - m, nchunk, LANES), F32)

        _per_live_rows(nv, tile, sub, body)


def _moe_ffn(xs, w_gate, w_up, w_down, l0, tile_expert, tile_valid, n_used, *, tile, sub, tn, tk):
    n_slots, C, _ = xs.shape
    D = C * LANES
    F = w_gate.shape[-1]
    nf, nk = F // tn, F // tk
    nt = n_slots // tile

    def live_t(t, nu):
        return jnp.minimum(t, nu[0] - 1)

    def live_step(t, j, nu, n):
        return jnp.where(t < nu[0], j, n - 1)

    def expert(t, te, nu):
        return l0 + te[live_t(t, nu)]

    row_tile = pl.BlockSpec((tile, C, LANES), lambda t, j, te, nv, nu: (live_t(t, nu), 0, 0))
    w_col = pl.BlockSpec((None, D, tn), lambda t, j, te, nv, nu: (expert(t, te, nu), 0, live_step(t, j, nu, nf)))
    a = pl.pallas_call(
        functools.partial(_moe_glu_kernel, sub=sub),
        grid_spec=pltpu.PrefetchScalarGridSpec(
            num_scalar_prefetch=3, grid=(nt, nf),
            in_specs=[row_tile, w_col, w_col],
            out_specs=pl.BlockSpec((tile, tn), lambda t, j, te, nv, nu: (live_t(t, nu), live_step(t, j, nu, nf))),
            scratch_shapes=[pltpu.VMEM((tile, D), BF16)]),
        out_shape=jax.ShapeDtypeStruct((n_slots, F), BF16),
        compiler_params=_params("arbitrary", "arbitrary"),
        name="moe_glu",
    )(tile_expert, tile_valid, n_used, xs, w_gate, w_up)
    return pl.pallas_call(
        functools.partial(_moe_down_kernel, sub=sub, nk=nk),
        grid_spec=pltpu.PrefetchScalarGridSpec(
            num_scalar_prefetch=3, grid=(nt, nk),
            in_specs=[pl.BlockSpec((tile, tk), lambda t, k, te, nv, nu: (live_t(t, nu), live_step(t, k, nu, nk))),
                      pl.BlockSpec((None, tk, D),
                                   lambda t, k, te, nv, nu: (expert(t, te, nu), live_step(t, k, nu, nk), 0))],
            out_specs=pl.BlockSpec((tile, C, LANES), lambda t, j, te, nv, nu: (live_t(t, nu), 0, 0),
                                   pipeline_mode=pl.Buffered(1)),
            scratch_shapes=[pltpu.VMEM((tile, D), F32)]),
        out_shape=jax.ShapeDtypeStruct((n_slots, C, LANES), F32),
        compiler_params=_params("arbitrary", "arbitrary"),
        name="moe_down",
    )(tile_expert, tile_valid, n_used, a, w_down)


def _combine_kernel(p1_ref, p2_ref, meta_ref, y_ref, f_ref, a_s, b_s, sem):
    tm, nchunk = a_s.shape[0], a_s.shape[1]
    i0 = pl.program_id(0) * tm

    def body(i, c):
        pltpu.make_async_copy(y_ref.at[p1_ref[i0 + i]], a_s.at[i], sem).start()
        pltpu.make_async_copy(y_ref.at[p2_ref[i0 + i]], b_s.at[i], sem).start()
        return c

    lax.fori_loop(0, tm, body, 0)
    _row_copies_wait(y_ref.at[pl.ds(0, tm)], a_s, sem, TOP_K)
    w1 = meta_ref[:, META_W1:META_W1 + 1]
    w2 = meta_ref[:, META_W2:META_W2 + 1]
    for c in range(nchunk):
        f_ref[:, c * LANES:(c + 1) * LANES] = w1 * a_s[:, c, :] + w2 * b_s[:, c, :]


def _combine(y, meta, p1, p2, *, tm):
    _, C, _ = y.shape
    R, NP = meta.shape
    return pl.pallas_call(
        _combine_kernel,
        grid_spec=pltpu.PrefetchScalarGridSpec(
            num_scalar_prefetch=2, grid=(R // tm,),
            in_specs=[pl.BlockSpec((tm, NP), lambda i, p1, p2: (i, 0)),
                      pl.BlockSpec(memory_space=pl.ANY)],
            out_specs=pl.BlockSpec((tm, C * LANES), lambda i, p1, p2: (i, 0)),
            scratch_shapes=[pltpu.VMEM((tm, C, LANES), F32), pltpu.VMEM((tm, C, LANES), F32),
                            pltpu.SemaphoreType.DMA]),
        out_shape=jax.ShapeDtypeStruct((R, C * LANES), F32),
        compiler_params=_params("arbitrary"),
        name="moe_combine",
    )(p1, p2, meta, y)


def _moe(h, h_rows, w_router_padded, w_gate, w_up, w_down, j, *, n_exp, tm_router, tile, sub, tn, tk, tm_rows):
    R = h.shape[0]
    meta, counts = _router(h, w_router_padded, j, n_exp=n_exp, tm=tm_router)
    cnt = counts[0, :n_exp].astype(jnp.int32)
    ntile = (cnt + tile - 1) // tile
    tend = jnp.cumsum(ntile)
    tstart = tend - ntile
    n_tiles = -(-TOP_K * R // tile) + n_exp
    e1, e2 = meta[:, META_E1].astype(jnp.int32), meta[:, META_E2].astype(jnp.int32)
    p1 = tstart[e1] * tile + meta[:, META_R1].astype(jnp.int32)
    p2 = tstart[e2] * tile + meta[:, META_R2].astype(jnp.int32)
    tid = jnp.arange(n_tiles, dtype=jnp.int32)
    tile_expert = jnp.minimum(jnp.sum(tid[:, None] >= tend[None, :], axis=1), n_exp - 1).astype(jnp.int32)
    tile_valid = jnp.clip(cnt[tile_expert] - (tid - tstart[tile_expert]) * tile, 0, tile)
    tile_valid = jnp.where(tid < tend[-1], tile_valid, 0).astype(jnp.int32)
    xs = _dispatch(h_rows, p1, p2, n_slots=n_tiles * tile, tm=tm_rows)
    y = _moe_ffn(xs, w_gate, w_up, w_down, j * n_exp, tile_expert, tile_valid, tend[-1:].astype(jnp.int32),
                 tile=tile, sub=sub, tn=tn, tk=tk)
    return _combine(y, meta, p1, p2, tm=tm_rows)


def kernel(x_prompt, x_sample, cache_k_win, cache_v_win, state_rglru_h, state_conv, c_prompt, c_sample, norm1_w, norm2_w, w_ada, b_ada, w_in, q_norm_w, k_norm_w, attn_sinks, conv_w, conv_b, lru_w_a, lru_b_a, lru_w_x, lru_b_x, lru_lambda, chunk_v_norm_w, chunk_w_s, chunk_b_s, w_branch_a, w_branch_b, w_branch_c, w_out, ffn_w_gate, ffn_w_up, ffn_w_down, moe_w_router, moe_w_gate, moe_w_up, moe_w_down):
    Bp, S, D = x_prompt.shape
    Bs, Ts, _ = x_sample.shape
    L = w_in.shape[0]
    window, n_kv, hd = cache_k_win.shape[2:]
    n_q = attn_sinks.shape[1]
    wa, kvw = n_q * hd, n_kv * hd
    wb = conv_w.shape[-1]
    wc = chunk_v_norm_w.shape[-1]
    chunk = chunk_w_s.shape[-1]
    cwid = conv_w.shape[1]
    n_exp = moe_w_router.shape[-1]
    assert Ts == SUBLANES and S % (Bs * Ts) == 0 and S % chunk == 0 and S % window == 0
    Rp, Rs = Bp * S, Bs * Ts
    R = Rp + Rs
    half = wb // 2
    assert wb == wc and wa % half == 0 and kvw % half == 0 or True
    col_lx = (wa + 2 * kvw) // half
    col_lg = col_lx + 2
    col_cu = col_lg + 2
    col_cv = col_cu + 2
    gate_col = wa + 2 * kvw + 2 * wb + 2 * wc

    W = dict(conv_w=conv_w, conv_b=conv_b, lru_w_a=lru_w_a, lru_b_a=lru_b_a, lru_w_x=lru_w_x, lru_b_x=lru_b_x,
             lru_lambda=lru_lambda, chunk_v_norm_w=chunk_v_norm_w, chunk_w_s=chunk_w_s, chunk_b_s=chunk_b_s,
             w_branch_a=w_branch_a, w_branch_b=w_branch_b, w_branch_c=w_branch_c)

    n_c = Bp + Bs
    n_c_pad = -(-n_c // SUBLANES) * SUBLANES
    c_all = jnp.concatenate([c_prompt, c_sample, jnp.zeros((n_c_pad - n_c, D), F32)], axis=0)
    mod = _ada(c_all, w_ada, b_ada)
    mod_p = jnp.repeat(mod[:, :Bp], SUBLANES, axis=1)
    mod_s = jnp.repeat(mod[:, Bp:n_c], Ts, axis=1)

    x = jnp.concatenate([x_prompt.reshape(Rp, D), x_sample.reshape(Rs, D)], axis=0)
    rn = functools.partial(_resid_norm, seq=S, n_prompt=Bp)
    tm_big = _pick(R, 1408, 128) if R % 128 == 0 else _pick(R, 1408, 16)
    tm_mid = _pick(R, 768, 128) if R % 128 == 0 else _pick(R, 768, 16)
    tm_wide = _pick(R, 2816, 128) if R % 128 == 0 else _pick(R, 2816, 16)
    w_router_p = jnp.pad(moe_w_router, ((0, 0), (0, 0), (0, LANES - n_exp)))
    moe_gate_w = moe_w_gate.reshape((-1,) + moe_w_gate.shape[2:])
    moe_up_w = moe_w_up.reshape((-1,) + moe_w_up.shape[2:])
    moe_down_w = moe_w_down.reshape((-1,) + moe_w_down.shape[2:])
    moe_sub = MOE_SUB_ROWS if TOP_K * R >= n_exp * MOE_SUB_ROWS * MOE_TILE_SUBS else 4 * SUBLANES

    buf_rows = jnp.pad(state_conv, ((0, 0), (0, 0), (Ts - (cwid - 1), 0), (0, 0))).reshape(L, Rs, wb)
    h0_rep = jnp.repeat(state_rglru_h, Ts, axis=1)

    _, h = rn(x, None, mod_p, mod_s, norm1_w, 0, gate_k=None, scale_k=(0, 1), shift_k=(0, 0))
    outs = []
    for l in range(L):
        z = _matmul(h, w_in, l, out_dtype=F32, tm=tm_wide, tn=_pick(w_in.shape[-1], 512, half), tk=D)
        attn_o, kwp, vwp = _attn_prompt(z, attn_sinks, q_norm_w, k_norm_w, l, R=R, n_prompt=Bp, seq=S,
                                        n_q=n_q, n_kv=n_kv, hd=hd, window=window)
        attn_o, kws, vws = _attn_sample(attn_o, z, cache_k_win, cache_v_win, attn_sinks, q_norm_w, k_norm_w, l,
                                        n_sample=Bs, T=Ts, n_q=n_q, n_kv=n_kv, hd=hd, window=window)
        lru_o, hp, cp8 = _lru_prompt(z, W, l, R=R, nseq=Bp, T=S, Tb=_pick(S, 512, SUBLANES),
                                     col_x=col_lx, col_g=col_lg, wb=wb)
        lru_o, hs_all = _lru_sample(lru_o, z, h0_rep[l], buf_rows[l], W, l, row0=Rp, T=Ts,
                                    col_x=col_lx, col_g=col_lg, wb=wb)
        chunk_o, vnp = _chunk(None, z, W, l, R=R, row0=0, nsteps=Rp // chunk, nb=1, Tc=chunk,
                              cps=S // chunk, col_u=col_cu, col_v=col_cv, wc=wc)
        chunk_o, vns = _chunk(chunk_o, z, W, l, R=R, row0=Rp, nsteps=1, nb=Bs, Tc=Ts, cps=1,
                              col_u=col_cu, col_v=col_cv, wc=wc)
        m = _merge(attn_o, lru_o, chunk_o, z, W, l, gate_col=gate_col, tm=tm_big, tn=_pick(D, 256, LANES))
        y = _matmul(m, w_out, l, out_dtype=F32, tm=tm_wide, tn=_pick(D, 512, LANES), tk=D)
        j = l // 2
        if l % 2 == 0:
            x, h = rn(x, y, mod_p, mod_s, norm2_w, l, gate_k=(l, 2), scale_k=(l, 4), shift_k=(l, 3))
            a = _glu(h, ffn_w_gate, ffn_w_up, j, 1, None, tm=tm_wide, tn=_pick(ffn_w_gate.shape[-1], 256, LANES))
            f = _matmul(a, ffn_w_down, j, out_dtype=F32, tm=tm_big, tn=_pick(D, 1024, LANES),
                        tk=_pick(a.shape[1], 1408, LANES))
        else:
            x, h, h_rows = rn(x, y, mod_p, mod_s, norm2_w, l, gate_k=(l, 2), scale_k=(l, 4), shift_k=(l, 3),
                              rows_out=True)
            f = _moe(h, h_rows, w_router_p, moe_gate_w, moe_up_w, moe_down_w, j, n_exp=n_exp, tm_router=tm_mid,
                     tile=MOE_TILE_SUBS * moe_sub, sub=moe_sub, tn=_pick(moe_gate_w.shape[-1], 512, LANES),
                     tk=_pick(moe_gate_w.shape[-1], 1024, LANES), tm_rows=Rs)
        if l + 1 < L:
            x, h = rn(x, f, mod_p, mod_s, norm1_w, l + 1, gate_k=(l, 5), scale_k=(l + 1, 1), shift_k=(l + 1, 0))
        else:
            x, _ = rn(x, f, mod_p, mod_s, None, l, gate_k=(l, 5), scale_k=None, shift_k=None)
        conv_p = cp8[:, SUBLANES - (cwid - 1):, :]
        lx_s = z[Rp:, col_lx * half:col_lx * half + wb].reshape(Bs, Ts, wb)
        conv_s = jnp.concatenate([state_conv[l], lx_s], axis=1)[:, -(cwid - 1):]
        hs = hs_all.reshape(Bs, Ts, wb)[:, Ts - 1]
        outs.append((kwp.reshape(Bp, window, n_kv, hd), vwp.reshape(Bp, window, n_kv, hd), hp.reshape(Bp, wb),
                     conv_p, vnp,
                     kws.reshape(Bs, window, n_kv, hd), vws.reshape(Bs, window, n_kv, hd), hs.reshape(Bs, wb),
                     conv_s, vns.reshape(Bs, Ts, wc)))
    st = [jnp.stack(s) for s in zip(*outs)]
    y_prompt = x[:Rp].reshape(Bp, S, D)
    y_sample = x[Rp:].reshape(Bs, Ts, D)
    return (y_prompt, y_sample, st[0], st[1], st[2], st[3], st[4], st[5], st[6], st[7], st[8], st[9])
```

```python
import functools

import jax
import jax.numpy as jnp
from jax import lax
from jax.experimental import pallas as pl
from jax.experimental.pallas import tpu as pltpu

EPS = 1e-6
LRU_C = 8.0
PAST_LEN = 16384
TOP_K = 2
MOE_SUB_ROWS = 384
MOE_TILE_SUBS = 3
SUBLANES = 8
LANES = 128
V7X_VMEM_LIMIT = 56 * 1024 * 1024

F32 = jnp.float32
BF16 = jnp.bfloat16


def _pick(n, target, mult):
    best = None
    for t in range(mult, min(n, target) + 1, mult):
        if n % t == 0:
            best = t
    assert best is not None, (n, target, mult)
    return best


def _params(*sem):
    return pltpu.CompilerParams(dimension_semantics=sem, vmem_limit_bytes=V7X_VMEM_LIMIT)


def _rms(x, w):
    return x * lax.rsqrt(jnp.mean(x * x, axis=-1, keepdims=True) + EPS) * w


def _ada_kernel(c_ref, w_ref, b_ref, o_ref):
    a = jax.nn.silu(c_ref[...]).astype(BF16)
    o_ref[...] = jnp.dot(a, w_ref[...].astype(BF16), preferred_element_type=F32) + b_ref[...]


def _ada(c_all, w_ada, b_ada):
    L, D, W6 = w_ada.shape
    Bc = c_all.shape[0]
    tn = _pick(W6, 1024, LANES)
    return pl.pallas_call(
        _ada_kernel,
        grid=(L, W6 // tn),
        in_specs=[pl.BlockSpec((Bc, D), lambda l, j: (0, 0)),
                  pl.BlockSpec((None, D, tn), lambda l, j: (l, 0, j)),
                  pl.BlockSpec((None, 1, tn), lambda l, j: (l, 0, j))],
        out_specs=pl.BlockSpec((None, Bc, tn), lambda l, j: (l, 0, j)),
        out_shape=jax.ShapeDtypeStruct((L, Bc, W6), F32),
        compiler_params=_params("parallel", "parallel"),
        name="ada_mod",
    )(c_all, w_ada, b_ada.reshape(L, 1, W6))


def _resid_norm_kernel(*refs, n_prompt_tiles, has_resid, has_norm, has_rows):
    refs = list(refs)
    x_ref = refs.pop(0)
    if has_resid:
        y_ref, gp_ref, gs_ref = refs.pop(0), refs.pop(0), refs.pop(0)
    if has_norm:
        scp_ref, scs_ref, shp_ref, shs_ref, nw_ref = (refs.pop(0) for _ in range(5))
    if has_resid:
        xo_ref = refs.pop(0)
    if has_norm:
        h_ref = refs.pop(0)
    if has_rows:
        hrow_ref = refs.pop(0)
    i = pl.program_id(0)

    def body(prompt):
        mod = (lambda p, s: p[0:1, :]) if prompt else (lambda p, s: s[...])
        x = x_ref[...]
        if has_resid:
            x = x + mod(gp_ref, gs_ref) * y_ref[...]
            xo_ref[...] = x
        if has_norm:
            hn = _rms(x, nw_ref[...])
            h = hn * (1.0 + mod(scp_ref, scs_ref)) + mod(shp_ref, shs_ref)
            h_ref[...] = h.astype(h_ref.dtype)
            if has_rows:
                hrow_ref[...] = h.reshape(hrow_ref.shape)

    pl.when(i < n_prompt_tiles)(lambda: body(True))
    pl.when(i >= n_prompt_tiles)(lambda: body(False))


def _resid_norm(x, y, mod_p, mod_s, norm_w, l, *, gate_k, scale_k, shift_k, seq, n_prompt, rows_out=False):
    R, D = x.shape
    TE = mod_s.shape[1]
    tpb = seq // TE
    npt = n_prompt * tpb
    has_resid = y is not None
    has_norm = norm_w is not None
    row = pl.BlockSpec((TE, D), lambda i: (i, 0))

    def mp(lk):
        return pl.BlockSpec((None, SUBLANES, D), lambda i: (lk[0], jnp.minimum(i // tpb, n_prompt - 1), lk[1]))

    def ms(lk):
        return pl.BlockSpec((None, TE, D), lambda i: (lk[0], 0, lk[1]))

    args, specs, outs, out_specs = [x], [row], [], []
    if has_resid:
        args += [y, mod_p, mod_s]
        specs += [row, mp(gate_k), ms(gate_k)]
        outs.append(jax.ShapeDtypeStruct((R, D), F32))
        out_specs.append(row)
    if has_norm:
        args += [mod_p, mod_s, mod_p, mod_s, norm_w.reshape(norm_w.shape[0], 1, D)]
        specs += [mp(scale_k), ms(scale_k), mp(shift_k), ms(shift_k),
                  pl.BlockSpec((None, 1, D), lambda i: (l, 0, 0))]
        outs.append(jax.ShapeDtypeStruct((R, D), BF16))
        out_specs.append(row)
    if rows_out:
        outs.append(jax.ShapeDtypeStruct((R, D // LANES, LANES), F32))
        out_specs.append(pl.BlockSpec((TE, D // LANES, LANES), lambda i: (i, 0, 0)))
    res = pl.pallas_call(
        functools.partial(_resid_norm_kernel, n_prompt_tiles=npt, has_resid=has_resid, has_norm=has_norm,
                          has_rows=rows_out),
        grid=(R // TE,),
        in_specs=specs, out_specs=out_specs, out_shape=outs,
        compiler_params=_params("parallel"),
        name="resid_norm",
    )(*args)
    res = list(res)
    x_new = res.pop(0) if has_resid else x
    h = res.pop(0) if has_norm else None
    if rows_out:
        return x_new, h, res.pop(0)
    return x_new, h


def _mm_kernel(a_ref, w_ref, o_ref, acc_ref, *, nk):
    part = jnp.dot(a_ref[...], w_ref[...].astype(BF16), preferred_element_type=F32)
    if nk == 1:
        o_ref[...] = part.astype(o_ref.dtype)
        return
    k = pl.program_id(2)

    @pl.when(k == 0)
    def _():
        acc_ref[...] = part

    @pl.when(k > 0)
    def _():
        acc_ref[...] += part

    @pl.when(k == nk - 1)
    def _():
        o_ref[...] = acc_ref[...].astype(o_ref.dtype)


def _matmul(a, w, l, *, out_dtype, tm, tn, tk):
    R, K = a.shape
    N = w.shape[-1]
    nk = K // tk
    acc_shape = (tm, tn) if nk > 1 else (SUBLANES, LANES)
    return pl.pallas_call(
        functools.partial(_mm_kernel, nk=nk),
        grid=(R // tm, N // tn, nk),
        in_specs=[pl.BlockSpec((tm, tk), lambda i, j, k: (i, k)),
                  pl.BlockSpec((None, tk, tn), lambda i, j, k: (l, k, j))],
        out_specs=pl.BlockSpec((tm, tn), lambda i, j, k: (i, j)),
        out_shape=jax.ShapeDtypeStruct((R, N), out_dtype),
        scratch_shapes=[pltpu.VMEM(acc_shape, F32)],
        compiler_params=_params("parallel", "parallel", "arbitrary"),
        name="matmul",
    )(a, w)


def _softmax_sink_pv(s, sink, v):
    m = jnp.maximum(jnp.max(s, axis=-1, keepdims=True), sink)
    e = jnp.exp(s - m)
    den = jnp.sum(e, axis=-1, keepdims=True) + jnp.exp(sink - m)
    return jnp.dot((e / den).astype(BF16), v, preferred_element_type=F32)


def _attn_prompt_kernel(sink_ref, q_ref, kp_ref, kc_ref, vp_ref, vc_ref, qw_ref, kw_ref,
                        o_ref, kwin_ref, vwin_ref, *, l, n_kv, q_per_kv, hd, window, nblk):
    i = pl.program_id(1)
    blk = q_ref.shape[0]
    scale = hd ** -0.5
    rows_q = q_per_kv * blk
    r = lax.broadcasted_iota(jnp.int32, (rows_q, 2 * blk), 0) % blk
    c = lax.broadcasted_iota(jnp.int32, (rows_q, 2 * blk), 1)
    dist = r + blk - c
    ok = (dist >= 0) & (dist <= window) & ((c >= blk) | (i > 0))
    hrow = lax.broadcasted_iota(jnp.int32, (rows_q, 1), 0) // blk
    qw = qw_ref[...]
    kw = kw_ref[...]
    for g in range(n_kv):
        sl = slice(g * hd, (g + 1) * hd)
        kcn = _rms(kc_ref[:, sl], kw)
        kcat = jnp.concatenate([_rms(kp_ref[:, sl], kw), kcn], axis=0).astype(BF16)
        vcat = jnp.concatenate([vp_ref[:, sl], vc_ref[:, sl]], axis=0).astype(BF16)
        heads = [g * q_per_kv + h for h in range(q_per_kv)]
        qs = jnp.concatenate([_rms(q_ref[:, hh * hd:(hh + 1) * hd], qw) for hh in heads], axis=0).astype(BF16)
        s = lax.dot_general(qs, kcat, (((1,), (1,)), ((), ())), preferred_element_type=F32) * scale
        s = jnp.where(ok, s, -1e30)
        sink = jnp.zeros((rows_q, 1), F32)
        for h, hh in enumerate(heads):
            sink = jnp.where(hrow == h, sink_ref[l, hh], sink)
        o = _softmax_sink_pv(s, sink, vcat)
        for h, hh in enumerate(heads):
            o_ref[:, hh * hd:(hh + 1) * hd] = o[h * blk:(h + 1) * blk, :].astype(o_ref.dtype)

        @pl.when(i == nblk - 1)
        def _():
            kwin_ref[:, sl] = kcn
            vwin_ref[:, sl] = vc_ref[:, sl]


def _attn_prompt(z, sinks, q_norm_w, k_norm_w, l, *, R, n_prompt, seq, n_q, n_kv, hd, window):
    blk = window
    nblk = seq // blk
    wa = n_q * hd
    kvw = n_kv * hd
    kcol = wa // kvw
    vcol = kcol + 1

    def cur(col):
        return pl.BlockSpec((blk, kvw), lambda b, i: (b * nblk + i, col))

    def prev(col):
        return pl.BlockSpec((blk, kvw), lambda b, i: (b * nblk + jnp.maximum(i - 1, 0), col))

    L = q_norm_w.shape[0]
    nw = pl.BlockSpec((None, 1, hd), lambda b, i: (l, 0, 0))
    return pl.pallas_call(
        functools.partial(_attn_prompt_kernel, l=l, n_kv=n_kv, q_per_kv=n_q // n_kv, hd=hd,
                          window=window, nblk=nblk),
        grid=(n_prompt, nblk),
        in_specs=[pl.BlockSpec(memory_space=pltpu.SMEM),
                  pl.BlockSpec((blk, wa), lambda b, i: (b * nblk + i, 0)),
                  prev(kcol), cur(kcol), prev(vcol), cur(vcol), nw, nw],
        out_specs=[pl.BlockSpec((blk, wa), lambda b, i: (b * nblk + i, 0)),
                   pl.BlockSpec((None, blk, kvw), lambda b, i: (b, 0, 0)),
                   pl.BlockSpec((None, blk, kvw), lambda b, i: (b, 0, 0))],
        out_shape=[jax.ShapeDtypeStruct((R, wa), BF16),
                   jax.ShapeDtypeStruct((n_prompt, blk, kvw), F32),
                   jax.ShapeDtypeStruct((n_prompt, blk, kvw), F32)],
        compiler_params=_params("parallel", "arbitrary"),
        name="attn_prompt",
    )(sinks, z, z, z, z, z, q_norm_w.reshape(L, 1, hd), k_norm_w.reshape(L, 1, hd))


def _attn_sample_kernel(sink_ref, o_in_ref, q_ref, kn_ref, vn_ref, ck_ref, cv_ref, qw_ref, kw_ref,
                        o_ref, kwin_ref, vwin_ref, keys_s, vals_s, ostage,
                        *, l, nb, T, n_kv, q_per_kv, hd, window):
    del o_in_ref
    W = ck_ref.shape[1]
    scale = hd ** -0.5
    rows_q = q_per_kv * T
    r = lax.broadcasted_iota(jnp.int32, (rows_q, 2 * W), 0) % T
    c = lax.broadcasted_iota(jnp.int32, (rows_q, 2 * W), 1)
    q_pos = PAST_LEN + r
    k_pos = jnp.where(c < W, PAST_LEN - W + c, PAST_LEN + c - W)
    dist = q_pos - k_pos
    ok = (dist >= 0) & (dist <= window) & (k_pos >= 0) & (c < W + T)
    hrow = lax.broadcasted_iota(jnp.int32, (rows_q, 1), 0) // T
    qw = qw_ref[...]
    kw = kw_ref[...]
    keys_s[...] = jnp.zeros_like(keys_s)
    vals_s[...] = jnp.zeros_like(vals_s)

    def step(b, carry):
        rows = pl.ds(pl.multiple_of(b * T, T), T)
        for g in range(n_kv):
            sl = slice(g * hd, (g + 1) * hd)
            ck = ck_ref[b, :, sl]
            cv = cv_ref[b, :, sl]
            knn = _rms(kn_ref[rows, sl], kw)
            vnn = vn_ref[rows, sl]
            keys_s[0:W, :] = ck
            keys_s[W:W + T, :] = knn
            vals_s[0:W, :] = cv
            vals_s[W:W + T, :] = vnn
            qs = jnp.concatenate(
                [_rms(q_ref[rows, (g * q_per_kv + h) * hd:(g * q_per_kv + h + 1) * hd], qw)
                 for h in range(q_per_kv)], axis=0).astype(BF16)
            s = lax.dot_general(qs, keys_s[...].astype(BF16), (((1,), (1,)), ((), ())),
                                preferred_element_type=F32) * scale
            s = jnp.where(ok, s, -1e30)
            sink = jnp.zeros((rows_q, 1), F32)
            for h in range(q_per_kv):
                sink = jnp.where(hrow == h, sink_ref[l, g * q_per_kv + h], sink)
            o = _softmax_sink_pv(s, sink, vals_s[...].astype(BF16))
            for h in range(q_per_kv):
                head = g * q_per_kv + h
                ostage[rows, head * hd:(head + 1) * hd] = o[h * T:(h + 1) * T, :]
            kwin_ref[b, 0:W - T, sl] = ck[T:, :]
            kwin_ref[b, W - T:W, sl] = knn
            vwin_ref[b, 0:W - T, sl] = cv[T:, :]
            vwin_ref[b, W - T:W, sl] = vnn
        return carry

    lax.fori_loop(0, nb, step, 0)
    o_ref[...] = ostage[...].astype(o_ref.dtype)


def _attn_sample(attn_o, z, cache_k, cache_v, sinks, q_norm_w, k_norm_w, l, *, n_sample, T, n_q, n_kv, hd,
                 window):
    R, wa = attn_o.shape
    kvw = n_kv * hd
    rows = n_sample * T
    blk_i = (R - rows) // rows
    kcol = wa // kvw
    L, nb, W = cache_k.shape[:3]
    ck = cache_k.reshape(L, nb, W, kvw)
    cv = cache_v.reshape(L, nb, W, kvw)
    nw = pl.BlockSpec((None, 1, hd), lambda i: (l, 0, 0))
    cache = pl.BlockSpec((None, nb, W, kvw), lambda i: (l, 0, 0, 0))
    win = pl.BlockSpec((nb, W, kvw), lambda i: (0, 0, 0))
    return pl.pallas_call(
        functools.partial(_attn_sample_kernel, l=l, nb=nb, T=T, n_kv=n_kv, q_per_kv=n_q // n_kv, hd=hd,
                          window=window),
        grid=(1,),
        in_specs=[pl.BlockSpec(memory_space=pltpu.SMEM),
                  pl.BlockSpec(memory_space=pl.ANY),
                  pl.BlockSpec((rows, wa), lambda i: (blk_i, 0)),
                  pl.BlockSpec((rows, kvw), lambda i: (blk_i, kcol)),
                  pl.BlockSpec((rows, kvw), lambda i: (blk_i, kcol + 1)),
                  cache, cache, nw, nw],
        out_specs=[pl.BlockSpec((rows, wa), lambda i: (blk_i, 0)), win, win],
        out_shape=[jax.ShapeDtypeStruct((R, wa), BF16),
                   jax.ShapeDtypeStruct((nb, W, kvw), F32),
                   jax.ShapeDtypeStruct((nb, W, kvw), F32)],
        scratch_shapes=[pltpu.VMEM((2 * W, hd), F32), pltpu.VMEM((2 * W, hd), F32),
                        pltpu.VMEM((rows, wa), F32)],
        input_output_aliases={1: 0},
        compiler_params=_params("arbitrary"),
        name="attn_sample",
    )(sinks, attn_o, z, z, z, ck, cv, q_norm_w.reshape(-1, 1, hd), k_norm_w.reshape(-1, 1, hd))


def _lru_gates(xc, wa_ref, ba_ref, wx_ref, bx_ref, lam_ref, a_s, u_s, *, nblk, lb):
    xcb = xc.astype(BF16)
    for n in range(nblk):
        sl = slice(n * lb, (n + 1) * lb)
        ra = jnp.dot(xcb[:, sl], wa_ref[n].astype(BF16), preferred_element_type=F32) + ba_ref[:, sl]
        rx = jnp.dot(xcb[:, sl], wx_ref[n].astype(BF16), preferred_element_type=F32) + bx_ref[:, sl]
        log_a = -LRU_C * jax.nn.sigmoid(ra) * jax.nn.softplus(-lam_ref[:, sl])
        a_s[:, sl] = jnp.exp(log_a)
        th = jnp.tanh(log_a)
        u_s[:, sl] = jnp.sqrt(-2.0 * th / (1.0 - th)) * jax.nn.sigmoid(rx) * xc[:, sl]

def _tile_scan(a, u, T):
    t = lax.broadcasted_iota(jnp.int32, (a.shape[0], 1), 0) % T
    d = 1
    while d < T:
        keep = t >= d
        u = jnp.where(keep, a * pltpu.roll(u, d, axis=0) + u, u)
        a = jnp.where(keep, a * pltpu.roll(a, d, axis=0), a)
        d *= 2
    return a, u


def _lru_prompt_kernel(x_ref, g_ref, cw_ref, cb_ref, wa_ref, ba_ref, wx_ref, bx_ref, lam_ref,
                       o_ref, hT_ref, cnew_ref, xp_s, a_s, u_s, h_s, *, Tb, ntb, cw, nblk, lb):
    tb = pl.program_id(2)
    P = SUBLANES

    @pl.when(tb == 0)
    def _():
        xp_s[0:P, :] = jnp.zeros((P, xp_s.shape[1]), F32)
        h_s[...] = jnp.zeros_like(h_s)

    xp_s[P:P + Tb, :] = x_ref[...]
    xc = cb_ref[...]
    for j in range(cw):
        off = P - (cw - 1) + j
        xc = xc + xp_s[off:off + Tb, :] * cw_ref[j:j + 1, :]
    _lru_gates(xc, wa_ref, ba_ref, wx_ref, bx_ref, lam_ref, a_s, u_s, nblk=nblk, lb=lb)

    a, u = _tile_scan(a_s[...], u_s[...], P)
    a_s[...] = a
    u_s[...] = u

    def step(k, h):
        rr = pl.ds(pl.multiple_of(k * P, P), P)
        hs = u_s[rr, :] + a_s[rr, :] * h
        u_s[rr, :] = hs
        return hs[P - 1:P, :]

    h = lax.fori_loop(0, Tb // P, step, h_s[...])
    h_s[...] = h
    o_ref[...] = (jax.nn.gelu(g_ref[...]) * u_s[...]).astype(o_ref.dtype)
    tail = xp_s[Tb:Tb + P, :]
    xp_s[0:P, :] = tail

    @pl.when(tb == ntb - 1)
    def _():
        hT_ref[...] = h
        cnew_ref[...] = tail


def _lru_sample_kernel(o_in_ref, x_ref, g_ref, h0_ref, buf_ref, cw_ref, cb_ref, wa_ref, ba_ref, wx_ref, bx_ref,
                       lam_ref, o_ref, hs_ref, a_s, u_s, *, T, cw, nblk, lb):
    del o_in_ref
    rows = x_ref.shape[0]
    t = lax.broadcasted_iota(jnp.int32, (rows, 1), 0) % T
    x = x_ref[...]
    buf = buf_ref[...]
    xc = cb_ref[...] + x * cw_ref[cw - 1:cw, :]
    for s in range(1, cw):
        xs = jnp.where(t >= s, pltpu.roll(x, s, axis=0), pltpu.roll(buf, rows - (T - s), axis=0))
        xc = xc + xs * cw_ref[cw - 1 - s:cw - s, :]
    _lru_gates(xc, wa_ref, ba_ref, wx_ref, bx_ref, lam_ref, a_s, u_s, nblk=nblk, lb=lb)
    a, u = _tile_scan(a_s[...], u_s[...], T)
    hs = u + a * h0_ref[...]
    hs_ref[...] = hs
    o_ref[...] = (jax.nn.gelu(g_ref[...]) * hs).astype(o_ref.dtype)


def _lru_weight_args(W, l, wb, index):
    ch = wb // 2
    nlb, lb = W['lru_w_a'].shape[1:3]
    nblk = nlb // 2
    cw = W['conv_w'].shape[1]

    def vec(name):
        return W[name].reshape(W[name].shape[0], 1, wb), pl.BlockSpec((None, 1, ch), index(lambda hf: (l, 0, hf)))

    def blkw(name):
        return W[name], pl.BlockSpec((None, nblk, lb, lb), index(lambda hf: (l, hf, 0, 0)))

    pairs = [(W['conv_w'], pl.BlockSpec((None, cw, ch), index(lambda hf: (l, 0, hf)))),
             vec('conv_b'), blkw('lru_w_a'), vec('lru_b_a'), blkw('lru_w_x'), vec('lru_b_x'), vec('lru_lambda')]
    return [p[0] for p in pairs], [p[1] for p in pairs], dict(cw=cw, nblk=nblk, lb=lb)


def _lru_prompt(z, W, l, *, R, nseq, T, Tb, col_x, col_g, wb):
    ch = wb // 2
    ntb = T // Tb
    wargs, wspecs, kw = _lru_weight_args(W, l, wb, lambda f: (lambda s, hf, t: f(hf)))

    def rowblk(col):
        return pl.BlockSpec((Tb, ch), lambda s, hf, t: (s * ntb + t, col + hf))

    return pl.pallas_call(
        functools.partial(_lru_prompt_kernel, Tb=Tb, ntb=ntb, **kw),
        grid=(nseq, 2, ntb),
        in_specs=[rowblk(col_x), rowblk(col_g)] + wspecs,
        out_specs=[pl.BlockSpec((Tb, ch), lambda s, hf, t: (s * ntb + t, hf)),
                   pl.BlockSpec((None, 1, ch), lambda s, hf, t: (s, 0, hf)),
                   pl.BlockSpec((None, SUBLANES, ch), lambda s, hf, t: (s, 0, hf))],
        out_shape=[jax.ShapeDtypeStruct((R, wb), BF16),
                   jax.ShapeDtypeStruct((nseq, 1, wb), F32),
                   jax.ShapeDtypeStruct((nseq, SUBLANES, wb), F32)],
        scratch_shapes=[pltpu.VMEM((SUBLANES + Tb, ch), F32), pltpu.VMEM((Tb, ch), F32),
                        pltpu.VMEM((Tb, ch), F32), pltpu.VMEM((1, ch), F32)],
        compiler_params=_params("parallel", "parallel", "arbitrary"),
        name="lru_prompt",
    )(z, z, *wargs)


def _lru_sample(lru_o, z, h0_rep, buf_rows, W, l, *, row0, T, col_x, col_g, wb):
    ch = wb // 2
    rows = h0_rep.shape[0]
    rb = row0 // rows
    wargs, wspecs, kw = _lru_weight_args(W, l, wb, lambda f: (lambda hf: f(hf)))
    st = pl.BlockSpec((rows, ch), lambda hf: (0, hf))
    return pl.pallas_call(
        functools.partial(_lru_sample_kernel, T=T, **kw),
        grid=(2,),
        in_specs=[pl.BlockSpec(memory_space=pl.ANY),
                  pl.BlockSpec((rows, ch), lambda hf: (rb, col_x + hf)),
                  pl.BlockSpec((rows, ch), lambda hf: (rb, col_g + hf)), st, st] + wspecs,
        out_specs=[pl.BlockSpec((rows, ch), lambda hf: (rb, hf)), st],
        out_shape=[jax.ShapeDtypeStruct(lru_o.shape, BF16), jax.ShapeDtypeStruct((rows, wb), F32)],
        scratch_shapes=[pltpu.VMEM((rows, ch), F32), pltpu.VMEM((rows, ch), F32)],
        input_output_aliases={0: 0},
        compiler_params=_params("parallel"),
        name="lru_sample",
    )(lru_o, z, z, h0_rep, buf_rows, *wargs)


def _chunk_kernel(*refs, nb, Tc, ngroups, gw, cps, aliased):
    refs = list(refs)
    if aliased:
        refs.pop(0)
    ulo_ref, uhi_ref, vlo_ref, vhi_ref, nw_ref, ws_ref, bst_ref, o_ref, vn_ref = refs
    rows = nb * Tc
    half = vlo_ref.shape[1]
    wc = 2 * half
    v_lo = jax.nn.gelu(vlo_ref[...])
    v_hi = jax.nn.gelu(vhi_ref[...])
    ms = (jnp.sum(v_lo * v_lo, axis=-1, keepdims=True) + jnp.sum(v_hi * v_hi, axis=-1, keepdims=True)) / wc
    inv = lax.rsqrt(ms + EPS)
    vn_halves = (v_lo * inv * nw_ref[:, 0:half], v_hi * inv * nw_ref[:, half:wc])
    u_halves = (ulo_ref, uhi_ref)

    r = lax.broadcasted_iota(jnp.int32, (rows, rows), 0)
    c = lax.broadcasted_iota(jnp.int32, (rows, rows), 1)
    mask = (r // Tc == c // Tc) & (c % Tc <= r % Tc)
    if nb > 1:
        sel = (lax.broadcasted_iota(jnp.int32, (ws_ref.shape[2], rows), 0)
               == lax.broadcasted_iota(jnp.int32, (ws_ref.shape[2], rows), 1) % Tc).astype(BF16)
        bias_rows = jnp.broadcast_to(bst_ref[0:Tc, :][None], (nb, Tc, bst_ref.shape[1])).reshape(rows, -1)
    else:
        bias_rows = bst_ref[0:Tc, :]
    gph = half // gw
    for g in range(ngroups):
        hf, gi = divmod(g, gph)
        sl = slice(gi * gw, (gi + 1) * gw)
        if nb > 1:
            t1 = jnp.broadcast_to(ws_ref[g, 0:Tc, :][None], (nb, Tc, ws_ref.shape[2])).reshape(rows, -1)
            wfull = jnp.dot(t1.astype(BF16), sel, preferred_element_type=F32)
        else:
            wfull = ws_ref[g, 0:Tc, 0:Tc]
        wm = jnp.where(mask, wfull, 0.0).astype(BF16)
        mixed = jnp.dot(wm, vn_halves[hf][:, sl].astype(BF16), preferred_element_type=F32)
        mixed = mixed + bias_rows[:, g:g + 1]
        o_ref[:, g * gw:(g + 1) * gw] = (jax.nn.gelu(u_halves[hf][:, sl]) * mixed).astype(o_ref.dtype)

    if cps == 1:
        vn_ref[:, 0:half] = vn_halves[0]
        vn_ref[:, half:wc] = vn_halves[1]
    else:
        @pl.when(pl.program_id(0) % cps == cps - 1)
        def _():
            vn_ref[:, 0:half] = vn_halves[0]
            vn_ref[:, half:wc] = vn_halves[1]


def _chunk(chunk_o, z, W, l, *, R, row0, nsteps, nb, Tc, cps, col_u, col_v, wc):
    half = wc // 2
    ngroups, chunk = W['chunk_w_s'].shape[1:3]
    gw = wc // ngroups
    rows = nb * Tc
    rb0 = row0 // rows
    aliased = chunk_o is not None

    def rowblk(col):
        return pl.BlockSpec((rows, half), lambda i: (rb0 + i, col))

    args = [z, z, z, z, W['chunk_v_norm_w'].reshape(-1, 1, wc), W['chunk_w_s'],
            jnp.swapaxes(W['chunk_b_s'], 1, 2)]
    specs = [rowblk(col_u), rowblk(col_u + 1), rowblk(col_v), rowblk(col_v + 1),
             pl.BlockSpec((None, 1, wc), lambda i: (l, 0, 0)),
             pl.BlockSpec((None, ngroups, chunk, chunk), lambda i: (l, 0, 0, 0)),
             pl.BlockSpec((None, chunk, ngroups), lambda i: (l, 0, 0))]
    io_alias = {}
    if aliased:
        args.insert(0, chunk_o)
        specs.insert(0, pl.BlockSpec(memory_space=pl.ANY))
        io_alias = {0: 0}
    return pl.pallas_call(
        functools.partial(_chunk_kernel, nb=nb, Tc=Tc, ngroups=ngroups, gw=gw, cps=cps, aliased=aliased),
        grid=(nsteps,),
        in_specs=specs,
        out_specs=[pl.BlockSpec((rows, wc), lambda i: (rb0 + i, 0)),
                   pl.BlockSpec((None, rows, wc), lambda i: (i // cps, 0, 0))],
        out_shape=[jax.ShapeDtypeStruct((R, wc), BF16),
                   jax.ShapeDtypeStruct((nsteps // cps, rows, wc), F32)],
        input_output_aliases=io_alias,
        compiler_params=_params("arbitrary"),
        name="chunk_sample" if aliased else "chunk_prompt",
    )(*args)


def _merge_kernel(a_ref, b_ref, c_ref, wa_ref, wb_ref, wc_ref, ga_ref, gb_ref, gc_ref, o_ref):
    def branch(x_ref, w_ref, g_ref):
        y = jnp.dot(x_ref[...], w_ref[...].astype(BF16), preferred_element_type=F32)
        return jax.nn.sigmoid(g_ref[...]) * y

    m = branch(a_ref, wa_ref, ga_ref) + branch(b_ref, wb_ref, gb_ref) + branch(c_ref, wc_ref, gc_ref)
    o_ref[...] = m.astype(o_ref.dtype)


def _merge(attn_o, lru_o, chunk_o, z, W, l, *, gate_col, tm, tn):
    R = attn_o.shape[0]
    D = W['w_branch_a'].shape[-1]
    g0 = gate_col // tn
    gstep = D // tn

    def xin(a):
        return pl.BlockSpec((tm, a.shape[1]), lambda i, j: (i, 0))

    def win(w):
        return pl.BlockSpec((None, w.shape[1], tn), lambda i, j: (l, 0, j))

    def gin(k):
        return pl.BlockSpec((tm, tn), lambda i, j: (i, g0 + k * gstep + j))

    return pl.pallas_call(
        _merge_kernel,
        grid=(R // tm, D // tn),
        in_specs=[xin(attn_o), xin(lru_o), xin(chunk_o),
                  win(W['w_branch_a']), win(W['w_branch_b']), win(W['w_branch_c']),
                  gin(0), gin(1), gin(2)],
        out_specs=pl.BlockSpec((tm, tn), lambda i, j: (i, j)),
        out_shape=jax.ShapeDtypeStruct((R, D), BF16),
        compiler_params=_params("parallel", "parallel"),
        name="merge",
    )(attn_o, lru_o, chunk_o, W['w_branch_a'], W['w_branch_b'], W['w_branch_c'], z, z, z)


def _glu_kernel(*refs, gated):
    if gated:
        h_ref, wg_ref, wu_ref, gate_ref, o_ref = refs
    else:
        h_ref, wg_ref, wu_ref, o_ref = refs
    h = h_ref[...]
    g = jnp.dot(h, wg_ref[...].astype(BF16), preferred_element_type=F32)
    u = jnp.dot(h, wu_ref[...].astype(BF16), preferred_element_type=F32)
    a = jax.nn.silu(g) * u
    if gated:
        e = pl.program_id(1)
        gate = gate_ref[...]
        lane = lax.broadcasted_iota(jnp.int32, gate.shape, 1)
        a = a * jnp.sum(jnp.where(lane == e, gate, 0.0), axis=-1, keepdims=True)
    o_ref[...] = a.astype(o_ref.dtype)


def _glu(h, w_gate, w_up, l0, n_exp, gate, *, tm, tn):
    R, D = h.shape
    F = w_gate.shape[-1]
    nf = F // tn
    gated = gate is not None
    wspec = pl.BlockSpec((None, D, tn), lambda i, e, j: (l0 + e, 0, j))
    args = [h, w_gate, w_up]
    specs = [pl.BlockSpec((tm, D), lambda i, e, j: (i, 0)), wspec, wspec]
    if gated:
        args.append(gate)
        specs.append(pl.BlockSpec((tm, gate.shape[1]), lambda i, e, j: (i, 0)))
    return pl.pallas_call(
        functools.partial(_glu_kernel, gated=gated),
        grid=(R // tm, n_exp, nf),
        in_specs=specs,
        out_specs=pl.BlockSpec((tm, tn), lambda i, e, j: (i, e * nf + j)),
        out_shape=jax.ShapeDtypeStruct((R, n_exp * F), BF16),
        compiler_params=_params("parallel", "parallel", "parallel"),
        name="glu",
    )(*args)


META_E1, META_E2, META_W1, META_W2, META_R1, META_R2 = range(6)


def _router_kernel(h_ref, w_ref, meta_ref, cnt_ref, carry, *, n_exp):
    i = pl.program_id(0)

    @pl.when(i == 0)
    def _():
        carry[...] = jnp.zeros_like(carry)

    logits = jnp.dot(h_ref[...], w_ref[...].astype(BF16), preferred_element_type=F32)
    tm = logits.shape[0]
    lane = lax.broadcasted_iota(jnp.int32, logits.shape, 1).astype(F32)
    big = float(logits.shape[1])
    l0 = jnp.where(lane < n_exp, logits, -jnp.inf)
    m1 = jnp.max(l0, axis=-1, keepdims=True)
    i1 = jnp.min(jnp.where(l0 == m1, lane, big), axis=-1, keepdims=True)
    l1 = jnp.where(lane == i1, -jnp.inf, l0)
    m2 = jnp.max(l1, axis=-1, keepdims=True)
    i2 = jnp.min(jnp.where(l1 == m2, lane, big), axis=-1, keepdims=True)
    e2 = jnp.exp(m2 - m1)
    den = 1.0 + e2
    hit = ((lane == i1) | (lane == i2)).astype(F32)
    earlier = (lax.broadcasted_iota(jnp.int32, (tm, tm), 0) > lax.broadcasted_iota(jnp.int32, (tm, tm), 1))
    rank = jnp.dot(earlier.astype(BF16), hit.astype(BF16), preferred_element_type=F32) + carry[...]
    r1 = jnp.sum(jnp.where(lane == i1, rank, 0.0), axis=-1, keepdims=True)
    r2 = jnp.sum(jnp.where(lane == i2, rank, 0.0), axis=-1, keepdims=True)
    meta = jnp.zeros_like(logits)
    for k, v in ((META_E1, i1), (META_E2, i2), (META_W1, 1.0 / den), (META_W2, e2 / den), (META_R1, r1),
                 (META_R2, r2)):
        meta = jnp.where(lane == k, v, meta)
    meta_ref[...] = meta
    carry[...] += jnp.sum(hit, axis=0, keepdims=True)
    cnt_ref[...] = carry[...]


def _router(h, w_router_padded, j, *, n_exp, tm):
    R, D = h.shape
    NP = w_router_padded.shape[-1]
    return pl.pallas_call(
        functools.partial(_router_kernel, n_exp=n_exp),
        grid=(R // tm,),
        in_specs=[pl.BlockSpec((tm, D), lambda i: (i, 0)),
                  pl.BlockSpec((None, D, NP), lambda i: (j, 0, 0))],
        out_specs=[pl.BlockSpec((tm, NP), lambda i: (i, 0)), pl.BlockSpec((1, NP), lambda i: (0, 0))],
        out_shape=[jax.ShapeDtypeStruct((R, NP), F32), jax.ShapeDtypeStruct((1, NP), F32)],
        scratch_shapes=[pltpu.VMEM((1, NP), F32)],
        compiler_params=_params("arbitrary"),
        name="router",
    )(h, w_router_padded)


def _row_copies_wait(src_like, dst_like, sem, n):
    for _ in range(n):
        pltpu.make_async_copy(src_like, dst_like, sem).wait()


def _dispatch_kernel(p1_ref, p2_ref, h_ref, xs_ref, sem):
    tm = h_ref.shape[0]
    i0 = pl.program_id(0) * tm

    def body(i, c):
        pltpu.make_async_copy(h_ref.at[i], xs_ref.at[p1_ref[i0 + i]], sem).start()
        pltpu.make_async_copy(h_ref.at[i], xs_ref.at[p2_ref[i0 + i]], sem).start()
        return c

    lax.fori_loop(0, tm, body, 0)
    _row_copies_wait(h_ref, xs_ref.at[pl.ds(0, tm)], sem, TOP_K)


def _dispatch(h_rows, p1, p2, *, n_slots, tm):
    R, C, _ = h_rows.shape
    return pl.pallas_call(
        _dispatch_kernel,
        grid_spec=pltpu.PrefetchScalarGridSpec(
            num_scalar_prefetch=2, grid=(R // tm,),
            in_specs=[pl.BlockSpec((tm, C, LANES), lambda i, p1, p2: (i, 0, 0))],
            out_specs=pl.BlockSpec(memory_space=pl.ANY),
            scratch_shapes=[pltpu.SemaphoreType.DMA]),
        out_shape=jax.ShapeDtypeStruct((n_slots, C, LANES), F32),
        compiler_params=_params("arbitrary"),
        name="moe_dispatch",
    )(p1, p2, h_rows)


def _per_live_rows(nv, tile, sub, body):
    for m in range(sub, tile + 1, sub):
        pl.when((nv > m - sub) & (nv <= m))(functools.partial(body, m))


def _moe_glu_kernel(te_ref, nv_ref, nu_ref, xs_ref, wg_ref, wu_ref, a_ref, xb_s, *, sub):
    t = pl.program_id(0)
    j = pl.program_id(1)
    nv = nv_ref[t]
    tile, nchunk = xs_ref.shape[0], xs_ref.shape[1]

    @pl.when(t < nu_ref[0])
    def _():
        wg = wg_ref[...].astype(BF16)
        wu = wu_ref[...].astype(BF16)

        def body(m):
            rows = slice(0, m)

            @pl.when(j == 0)
            def _():
                live = lax.broadcasted_iota(jnp.int32, (m, 1), 0) < nv
                x2d = xs_ref[rows, :, :].reshape(m, nchunk * LANES)
                xb_s[rows, :] = jnp.where(live, x2d, 0.0).astype(BF16)

            x = xb_s[rows, :]
            g = jnp.dot(x, wg, preferred_element_type=F32)
            u = jnp.dot(x, wu, preferred_element_type=F32)
            a_ref[rows, :] = (jax.nn.silu(g) * u).astype(a_ref.dtype)
            if m < tile:
                a_ref[m:tile, :] = jnp.zeros((tile - m, a_ref.shape[1]), a_ref.dtype)

        _per_live_rows(nv, tile, sub, body)


def _moe_down_kernel(te_ref, nv_ref, nu_ref, a_ref, wd_ref, y_ref, acc_s, *, sub, nk):
    t = pl.program_id(0)
    k = pl.program_id(1)
    nv = nv_ref[t]
    tile, nchunk = y_ref.shape[0], y_ref.shape[1]

    @pl.when(t < nu_ref[0])
    def _():
        wd = wd_ref[...].astype(BF16)

        def body(m):
            rows = slice(0, m)
            part = jnp.dot(a_ref[rows, :], wd, preferred_element_type=F32)

            @pl.when(k == 0)
            def _():
                acc_s[rows, :] = part

            @pl.when(k > 0)
            def _():
                acc_s[rows, :] += part

            @pl.when(k == nk - 1)
            def _():
                y_ref[rows, :, :] = acc_s[rows, :].reshape(m, nchunk, LANES)
                if m < tile:
                    y_ref[m:tile, :, :] = jnp.zeros((tile - m, nchunk, LANES), F32)

        _per_live_rows(nv, tile, sub, body)


def _moe_ffn(xs, w_gate, w_up, w_down, l0, tile_expert, tile_valid, n_used, *, tile, sub, tn, tk):
    n_slots, C, _ = xs.shape
    D = C * LANES
    F = w_gate.shape[-1]
    nf, nk = F // tn, F // tk
    nt = n_slots // tile

    def live_t(t, nu):
        return jnp.minimum(t, nu[0] - 1)

    def live_step(t, j, nu, n):
        return jnp.where(t < nu[0], j, n - 1)

    def expert(t, te, nu):
        return l0 + te[live_t(t, nu)]

    row_tile = pl.BlockSpec((tile, C, LANES), lambda t, j, te, nv, nu: (live_t(t, nu), 0, 0))
    w_col = pl.BlockSpec((None, D, tn), lambda t, j, te, nv, nu: (expert(t, te, nu), 0, live_step(t, j, nu, nf)))
    a = pl.pallas_call(
        functools.partial(_moe_glu_kernel, sub=sub),
        grid_spec=pltpu.PrefetchScalarGridSpec(
            num_scalar_prefetch=3, grid=(nt, nf),
            in_specs=[row_tile, w_col, w_col],
            out_specs=pl.BlockSpec((tile, tn), lambda t, j, te, nv, nu: (live_t(t, nu), live_step(t, j, nu, nf))),
            scratch_shapes=[pltpu.VMEM((tile, D), BF16)]),
        out_shape=jax.ShapeDtypeStruct((n_slots, F), BF16),
        compiler_params=_params("arbitrary", "arbitrary"),
        name="moe_glu",
    )(tile_expert, tile_valid, n_used, xs, w_gate, w_up)
    return pl.pallas_call(
        functools.partial(_moe_down_kernel, sub=sub, nk=nk),
        grid_spec=pltpu.PrefetchScalarGridSpec(
            num_scalar_prefetch=3, grid=(nt, nk),
            in_specs=[pl.BlockSpec((tile, tk), lambda t, k, te, nv, nu: (live_t(t, nu), live_step(t, k, nu, nk))),
                      pl.BlockSpec((None, tk, D),
                                   lambda t, k, te, nv, nu: (expert(t, te, nu), live_step(t, k, nu, nk), 0))],
            out_specs=pl.BlockSpec((tile, C, LANES), lambda t, j, te, nv, nu: (live_t(t, nu), 0, 0),
                                   pipeline_mode=pl.Buffered(1)),
            scratch_shapes=[pltpu.VMEM((tile, D), F32)]),
        out_shape=jax.ShapeDtypeStruct((n_slots, C, LANES), F32),
        compiler_params=_params("arbitrary", "arbitrary"),
        name="moe_down",
    )(tile_expert, tile_valid, n_used, a, w_down)


def _combine_kernel(p1_ref, p2_ref, meta_ref, y_ref, f_ref, a_s, b_s, sem):
    tm, nchunk = a_s.shape[0], a_s.shape[1]
    i0 = pl.program_id(0) * tm

    def body(i, c):
        pltpu.make_async_copy(y_ref.at[p1_ref[i0 + i]], a_s.at[i], sem).start()
        pltpu.make_async_copy(y_ref.at[p2_ref[i0 + i]], b_s.at[i], sem).start()
        return c

    lax.fori_loop(0, tm, body, 0)
    _row_copies_wait(y_ref.at[pl.ds(0, tm)], a_s, sem, TOP_K)
    w1 = meta_ref[:, META_W1:META_W1 + 1]
    w2 = meta_ref[:, META_W2:META_W2 + 1]
    d = nchunk * LANES
    f_ref[...] = w1 * a_s[...].reshape(tm, d) + w2 * b_s[...].reshape(tm, d)


def _combine(y, meta, p1, p2, *, tm):
    _, C, _ = y.shape
    R, NP = meta.shape
    return pl.pallas_call(
        _combine_kernel,
        grid_spec=pltpu.PrefetchScalarGridSpec(
            num_scalar_prefetch=2, grid=(R // tm,),
            in_specs=[pl.BlockSpec((tm, NP), lambda i, p1, p2: (i, 0)),
                      pl.BlockSpec(memory_space=pl.ANY)],
            out_specs=pl.BlockSpec((tm, C * LANES), lambda i, p1, p2: (i, 0)),
            scratch_shapes=[pltpu.VMEM((tm, C, LANES), F32), pltpu.VMEM((tm, C, LANES), F32),
                            pltpu.SemaphoreType.DMA]),
        out_shape=jax.ShapeDtypeStruct((R, C * LANES), F32),
        compiler_params=_params("arbitrary"),
        name="moe_combine",
    )(p1, p2, meta, y)


def _moe(h, h_rows, w_router_padded, w_gate, w_up, w_down, j, *, n_exp, tm_router, tile, sub, tn, tk, tm_rows):
    R = h.shape[0]
    meta, counts = _router(h, w_router_padded, j, n_exp=n_exp, tm=tm_router)
    cnt = counts[0, :n_exp].astype(jnp.int32)
    ntile = (cnt + tile - 1) // tile
    tend = jnp.cumsum(ntile)
    tstart = tend - ntile
    n_tiles = -(-TOP_K * R // tile) + n_exp
    e1, e2 = meta[:, META_E1].astype(jnp.int32), meta[:, META_E2].astype(jnp.int32)
    p1 = tstart[e1] * tile + meta[:, META_R1].astype(jnp.int32)
    p2 = tstart[e2] * tile + meta[:, META_R2].astype(jnp.int32)
    tid = jnp.arange(n_tiles, dtype=jnp.int32)
    tile_expert = jnp.minimum(jnp.sum(tid[:, None] >= tend[None, :], axis=1), n_exp - 1).astype(jnp.int32)
    tile_valid = jnp.clip(cnt[tile_expert] - (tid - tstart[tile_expert]) * tile, 0, tile)
    tile_valid = jnp.where(tid < tend[-1], tile_valid, 0).astype(jnp.int32)
    xs = _dispatch(h_rows, p1, p2, n_slots=n_tiles * tile, tm=tm_rows)
    y = _moe_ffn(xs, w_gate, w_up, w_down, j * n_exp, tile_expert, tile_valid, tend[-1:].astype(jnp.int32),
                 tile=tile, sub=sub, tn=tn, tk=tk)
    return _combine(y, meta, p1, p2, tm=tm_rows)


def kernel(x_prompt, x_sample, cache_k_win, cache_v_win, state_rglru_h, state_conv, c_prompt, c_sample, norm1_w, norm2_w, w_ada, b_ada, w_in, q_norm_w, k_norm_w, attn_sinks, conv_w, conv_b, lru_w_a, lru_b_a, lru_w_x, lru_b_x, lru_lambda, chunk_v_norm_w, chunk_w_s, chunk_b_s, w_branch_a, w_branch_b, w_branch_c, w_out, ffn_w_gate, ffn_w_up, ffn_w_down, moe_w_router, moe_w_gate, moe_w_up, moe_w_down):
    Bp, S, D = x_prompt.shape
    Bs, Ts, _ = x_sample.shape
    L = w_in.shape[0]
    window, n_kv, hd = cache_k_win.shape[2:]
    n_q = attn_sinks.shape[1]
    wa, kvw = n_q * hd, n_kv * hd
    wb = conv_w.shape[-1]
    wc = chunk_v_norm_w.shape[-1]
    chunk = chunk_w_s.shape[-1]
    cwid = conv_w.shape[1]
    n_exp = moe_w_router.shape[-1]
    assert Ts == SUBLANES and S % (Bs * Ts) == 0 and S % chunk == 0 and S % window == 0
    Rp, Rs = Bp * S, Bs * Ts
    R = Rp + Rs
    half = wb // 2
    assert wb == wc and wa % half == 0 and kvw % half == 0 or True
    col_lx = (wa + 2 * kvw) // half
    col_lg = col_lx + 2
    col_cu = col_lg + 2
    col_cv = col_cu + 2
    gate_col = wa + 2 * kvw + 2 * wb + 2 * wc

    W = dict(conv_w=conv_w, conv_b=conv_b, lru_w_a=lru_w_a, lru_b_a=lru_b_a, lru_w_x=lru_w_x, lru_b_x=lru_b_x,
             lru_lambda=lru_lambda, chunk_v_norm_w=chunk_v_norm_w, chunk_w_s=chunk_w_s, chunk_b_s=chunk_b_s,
             w_branch_a=w_branch_a, w_branch_b=w_branch_b, w_branch_c=w_branch_c)

    n_c = Bp + Bs
    n_c_pad = -(-n_c // SUBLANES) * SUBLANES
    c_all = jnp.concatenate([c_prompt, c_sample, jnp.zeros((n_c_pad - n_c, D), F32)], axis=0)
    mod = _ada(c_all, w_ada, b_ada)
    mod_p = jnp.repeat(mod[:, :Bp], SUBLANES, axis=1)
    mod_s = jnp.repeat(mod[:, Bp:n_c], Ts, axis=1)

    x = jnp.concatenate([x_prompt.reshape(Rp, D), x_sample.reshape(Rs, D)], axis=0)
    rn = functools.partial(_resid_norm, seq=S, n_prompt=Bp)
    tm_big = _pick(R, 1408, 128) if R % 128 == 0 else _pick(R, 1408, 16)
    tm_mid = _pick(R, 768, 128) if R % 128 == 0 else _pick(R, 768, 16)
    tm_wide = _pick(R, 2816, 128) if R % 128 == 0 else _pick(R, 2816, 16)
    w_router_p = jnp.pad(moe_w_router, ((0, 0), (0, 0), (0, LANES - n_exp)))
    moe_gate_w = moe_w_gate.reshape((-1,) + moe_w_gate.shape[2:])
    moe_up_w = moe_w_up.reshape((-1,) + moe_w_up.shape[2:])
    moe_down_w = moe_w_down.reshape((-1,) + moe_w_down.shape[2:])
    moe_sub = MOE_SUB_ROWS if TOP_K * R >= n_exp * MOE_SUB_ROWS * MOE_TILE_SUBS else 4 * SUBLANES

    buf_rows = jnp.pad(state_conv, ((0, 0), (0, 0), (Ts - (cwid - 1), 0), (0, 0))).reshape(L, Rs, wb)
    h0_rep = jnp.repeat(state_rglru_h, Ts, axis=1)

    _, h = rn(x, None, mod_p, mod_s, norm1_w, 0, gate_k=None, scale_k=(0, 1), shift_k=(0, 0))
    outs = []
    for l in range(L):
        z = _matmul(h, w_in, l, out_dtype=F32, tm=tm_wide, tn=_pick(w_in.shape[-1], 512, half), tk=D)
        attn_o, kwp, vwp = _attn_prompt(z, attn_sinks, q_norm_w, k_norm_w, l, R=R, n_prompt=Bp, seq=S,
                                        n_q=n_q, n_kv=n_kv, hd=hd, window=window)
        attn_o, kws, vws = _attn_sample(attn_o, z, cache_k_win, cache_v_win, attn_sinks, q_norm_w, k_norm_w, l,
                                        n_sample=Bs, T=Ts, n_q=n_q, n_kv=n_kv, hd=hd, window=window)
        lru_o, hp, cp8 = _lru_prompt(z, W, l, R=R, nseq=Bp, T=S, Tb=_pick(S, 512, SUBLANES),
                                     col_x=col_lx, col_g=col_lg, wb=wb)
        lru_o, hs_all = _lru_sample(lru_o, z, h0_rep[l], buf_rows[l], W, l, row0=Rp, T=Ts,
                                    col_x=col_lx, col_g=col_lg, wb=wb)
        chunk_o, vnp = _chunk(None, z, W, l, R=R, row0=0, nsteps=Rp // chunk, nb=1, Tc=chunk,
                              cps=S // chunk, col_u=col_cu, col_v=col_cv, wc=wc)
        chunk_o, vns = _chunk(chunk_o, z, W, l, R=R, row0=Rp, nsteps=1, nb=Bs, Tc=Ts, cps=1,
                              col_u=col_cu, col_v=col_cv, wc=wc)
        m = _merge(attn_o, lru_o, chunk_o, z, W, l, gate_col=gate_col, tm=tm_big, tn=_pick(D, 256, LANES))
        y = _matmul(m, w_out, l, out_dtype=F32, tm=tm_wide, tn=_pick(D, 512, LANES), tk=D)
        j = l // 2
        if l % 2 == 0:
            x, h = rn(x, y, mod_p, mod_s, norm2_w, l, gate_k=(l, 2), scale_k=(l, 4), shift_k=(l, 3))
            a = _glu(h, ffn_w_gate, ffn_w_up, j, 1, None, tm=tm_wide, tn=_pick(ffn_w_gate.shape[-1], 256, LANES))
            f = _matmul(a, ffn_w_down, j, out_dtype=F32, tm=tm_big, tn=_pick(D, 1024, LANES),
                        tk=_pick(a.shape[1], 1408, LANES))
        else:
            x, h, h_rows = rn(x, y, mod_p, mod_s, norm2_w, l, gate_k=(l, 2), scale_k=(l, 4), shift_k=(l, 3),
                              rows_out=True)
            f = _moe(h, h_rows, w_router_p, moe_gate_w, moe_up_w, moe_down_w, j, n_exp=n_exp, tm_router=tm_mid,
                     tile=MOE_TILE_SUBS * moe_sub, sub=moe_sub, tn=_pick(moe_gate_w.shape[-1], 512, LANES),
                     tk=_pick(moe_gate_w.shape[-1], 1024, LANES), tm_rows=Rs)
        if l + 1 < L:
            x, h = rn(x, f, mod_p, mod_s, norm1_w, l + 1, gate_k=(l, 5), scale_k=(l + 1, 1), shift_k=(l + 1, 0))
        else:
            x, _ = rn(x, f, mod_p, mod_s, None, l, gate_k=(l, 5), scale_k=None, shift_k=None)
        conv_p = cp8[:, SUBLANES - (cwid - 1):, :]
        lx_s = z[Rp:, col_lx * half:col_lx * half + wb].reshape(Bs, Ts, wb)
        conv_s = jnp.concatenate([state_conv[l], lx_s], axis=1)[:, -(cwid - 1):]
        hs = hs_all.reshape(Bs, Ts, wb)[:, Ts - 1]
        outs.append((kwp.reshape(Bp, window, n_kv, hd), vwp.reshape(Bp, window, n_kv, hd), hp.reshape(Bp, wb),
                     conv_p, vnp,
                     kws.reshape(Bs, window, n_kv, hd), vws.reshape(Bs, window, n_kv, hd), hs.reshape(Bs, wb),
                     conv_s, vns.reshape(Bs, Ts, wc)))
    st = [jnp.stack(s) for s in zip(*outs)]
    y_prompt = x[:Rp].reshape(Bp, S, D)
    y_sample = x[Rp:].reshape(Bs, Ts, D)
    return (y_prompt, y_sample, st[0], st[1], st[2], st[3], st[4], st[5], st[6], st[7], st[8], st[9])
```

```python
import functools

import jax
import jax.numpy as jnp
from jax import lax
from jax.experimental import pallas as pl
from jax.experimental.pallas import tpu as pltpu

EPS = 1e-6
LRU_C = 8.0
PAST_LEN = 16384
TOP_K = 2
MOE_SUB_ROWS = 288
MOE_TILE_SUBS = 4
SUBLANES = 8
LANES = 128
V7X_VMEM_LIMIT = 56 * 1024 * 1024

F32 = jnp.float32
BF16 = jnp.bfloat16


def _pick(n, target, mult):
    best = None
    for t in range(mult, min(n, target) + 1, mult):
        if n % t == 0:
            best = t
    assert best is not None, (n, target, mult)
    return best


def _params(*sem):
    return pltpu.CompilerParams(dimension_semantics=sem, vmem_limit_bytes=V7X_VMEM_LIMIT)


def _rms(x, w):
    return x * lax.rsqrt(jnp.mean(x * x, axis=-1, keepdims=True) + EPS) * w


def _ada_kernel(c_ref, w_ref, b_ref, o_ref):
    a = jax.nn.silu(c_ref[...]).astype(BF16)
    o_ref[...] = jnp.dot(a, w_ref[...].astype(BF16), preferred_element_type=F32) + b_ref[...]


def _ada(c_all, w_ada, b_ada):
    L, D, W6 = w_ada.shape
    Bc = c_all.shape[0]
    tn = _pick(W6, 1024, LANES)
    return pl.pallas_call(
        _ada_kernel,
        grid=(L, W6 // tn),
        in_specs=[pl.BlockSpec((Bc, D), lambda l, j: (0, 0)),
                  pl.BlockSpec((None, D, tn), lambda l, j: (l, 0, j)),
                  pl.BlockSpec((None, 1, tn), lambda l, j: (l, 0, j))],
        out_specs=pl.BlockSpec((None, Bc, tn), lambda l, j: (l, 0, j)),
        out_shape=jax.ShapeDtypeStruct((L, Bc, W6), F32),
        compiler_params=_params("parallel", "parallel"),
        name="ada_mod",
    )(c_all, w_ada, b_ada.reshape(L, 1, W6))


def _resid_norm_kernel(*refs, n_prompt_tiles, has_resid, has_norm, has_rows):
    refs = list(refs)
    x_ref = refs.pop(0)
    if has_resid:
        y_ref, gp_ref, gs_ref = refs.pop(0), refs.pop(0), refs.pop(0)
    if has_norm:
        scp_ref, scs_ref, shp_ref, shs_ref, nw_ref = (refs.pop(0) for _ in range(5))
    if has_resid:
        xo_ref = refs.pop(0)
    if has_norm:
        h_ref = refs.pop(0)
    if has_rows:
        hrow_ref = refs.pop(0)
    i = pl.program_id(0)

    def body(prompt):
        mod = (lambda p, s: p[0:1, :]) if prompt else (lambda p, s: s[...])
        x = x_ref[...]
        if has_resid:
            x = x + mod(gp_ref, gs_ref) * y_ref[...]
            xo_ref[...] = x
        if has_norm:
            hn = _rms(x, nw_ref[...])
            h = hn * (1.0 + mod(scp_ref, scs_ref)) + mod(shp_ref, shs_ref)
            h_ref[...] = h.astype(h_ref.dtype)
            if has_rows:
                hrow_ref[...] = h.reshape(hrow_ref.shape)

    pl.when(i < n_prompt_tiles)(lambda: body(True))
    pl.when(i >= n_prompt_tiles)(lambda: body(False))


def _resid_norm(x, y, mod_p, mod_s, norm_w, l, *, gate_k, scale_k, shift_k, seq, n_prompt, rows_out=False):
    R, D = x.shape
    TE = mod_s.shape[1]
    tpb = seq // TE
    npt = n_prompt * tpb
    has_resid = y is not None
    has_norm = norm_w is not None
    row = pl.BlockSpec((TE, D), lambda i: (i, 0))

    def mp(lk):
        return pl.BlockSpec((None, SUBLANES, D), lambda i: (lk[0], jnp.minimum(i // tpb, n_prompt - 1), lk[1]))

    def ms(lk):
        return pl.BlockSpec((None, TE, D), lambda i: (lk[0], 0, lk[1]))

    args, specs, outs, out_specs = [x], [row], [], []
    if has_resid:
        args += [y, mod_p, mod_s]
        specs += [row, mp(gate_k), ms(gate_k)]
        outs.append(jax.ShapeDtypeStruct((R, D), F32))
        out_specs.append(row)
    if has_norm:
        args += [mod_p, mod_s, mod_p, mod_s, norm_w.reshape(norm_w.shape[0], 1, D)]
        specs += [mp(scale_k), ms(scale_k), mp(shift_k), ms(shift_k),
                  pl.BlockSpec((None, 1, D), lambda i: (l, 0, 0))]
        outs.append(jax.ShapeDtypeStruct((R, D), BF16))
        out_specs.append(row)
    if rows_out:
        outs.append(jax.ShapeDtypeStruct((R, D // LANES, LANES), F32))
        out_specs.append(pl.BlockSpec((TE, D // LANES, LANES), lambda i: (i, 0, 0)))
    res = pl.pallas_call(
        functools.partial(_resid_norm_kernel, n_prompt_tiles=npt, has_resid=has_resid, has_norm=has_norm,
                          has_rows=rows_out),
        grid=(R // TE,),
        in_specs=specs, out_specs=out_specs, out_shape=outs,
        compiler_params=_params("parallel"),
        name="resid_norm",
    )(*args)
    res = list(res)
    x_new = res.pop(0) if has_resid else x
    h = res.pop(0) if has_norm else None
    if rows_out:
        return x_new, h, res.pop(0)
    return x_new, h


def _mm_kernel(a_ref, w_ref, o_ref, acc_ref, *, nk):
    part = jnp.dot(a_ref[...], w_ref[...].astype(BF16), preferred_element_type=F32)
    if nk == 1:
        o_ref[...] = part.astype(o_ref.dtype)
        return
    k = pl.program_id(2)

    @pl.when(k == 0)
    def _():
        acc_ref[...] = part

    @pl.when(k > 0)
    def _():
        acc_ref[...] += part

    @pl.when(k == nk - 1)
    def _():
        o_ref[...] = acc_ref[...].astype(o_ref.dtype)


def _matmul(a, w, l, *, out_dtype, tm, tn, tk):
    R, K = a.shape
    N = w.shape[-1]
    nk = K // tk
    acc_shape = (tm, tn) if nk > 1 else (SUBLANES, LANES)
    return pl.pallas_call(
        functools.partial(_mm_kernel, nk=nk),
        grid=(R // tm, N // tn, nk),
        in_specs=[pl.BlockSpec((tm, tk), lambda i, j, k: (i, k)),
                  pl.BlockSpec((None, tk, tn), lambda i, j, k: (l, k, j))],
        out_specs=pl.BlockSpec((tm, tn), lambda i, j, k: (i, j)),
        out_shape=jax.ShapeDtypeStruct((R, N), out_dtype),
        scratch_shapes=[pltpu.VMEM(acc_shape, F32)],
        compiler_params=_params("parallel", "parallel", "arbitrary"),
        name="matmul",
    )(a, w)


def _softmax_sink_pv(s, sink, v):
    m = jnp.maximum(jnp.max(s, axis=-1, keepdims=True), sink)
    e = jnp.exp(s - m)
    den = jnp.sum(e, axis=-1, keepdims=True) + jnp.exp(sink - m)
    return jnp.dot((e / den).astype(BF16), v, preferred_element_type=F32)


def _attn_prompt_kernel(sink_ref, q_ref, kp_ref, kc_ref, vp_ref, vc_ref, qw_ref, kw_ref,
                        o_ref, kwin_ref, vwin_ref, *, l, n_kv, q_per_kv, hd, window, nblk):
    i = pl.program_id(1)
    blk = q_ref.shape[0]
    scale = hd ** -0.5
    rows_q = q_per_kv * blk
    r = lax.broadcasted_iota(jnp.int32, (rows_q, 2 * blk), 0) % blk
    c = lax.broadcasted_iota(jnp.int32, (rows_q, 2 * blk), 1)
    dist = r + blk - c
    ok = (dist >= 0) & (dist <= window) & ((c >= blk) | (i > 0))
    hrow = lax.broadcasted_iota(jnp.int32, (rows_q, 1), 0) // blk
    qw = qw_ref[...]
    kw = kw_ref[...]
    for g in range(n_kv):
        sl = slice(g * hd, (g + 1) * hd)
        kcn = _rms(kc_ref[:, sl], kw)
        kcat = jnp.concatenate([_rms(kp_ref[:, sl], kw), kcn], axis=0).astype(BF16)
        vcat = jnp.concatenate([vp_ref[:, sl], vc_ref[:, sl]], axis=0).astype(BF16)
        heads = [g * q_per_kv + h for h in range(q_per_kv)]
        qs = jnp.concatenate([_rms(q_ref[:, hh * hd:(hh + 1) * hd], qw) for hh in heads], axis=0).astype(BF16)
        s = lax.dot_general(qs, kcat, (((1,), (1,)), ((), ())), preferred_element_type=F32) * scale
        s = jnp.where(ok, s, -1e30)
        sink = jnp.zeros((rows_q, 1), F32)
        for h, hh in enumerate(heads):
            sink = jnp.where(hrow == h, sink_ref[l, hh], sink)
        o = _softmax_sink_pv(s, sink, vcat)
        for h, hh in enumerate(heads):
            o_ref[:, hh * hd:(hh + 1) * hd] = o[h * blk:(h + 1) * blk, :].astype(o_ref.dtype)

        @pl.when(i == nblk - 1)
        def _():
            kwin_ref[:, sl] = kcn
            vwin_ref[:, sl] = vc_ref[:, sl]


def _attn_prompt(z, sinks, q_norm_w, k_norm_w, l, *, R, n_prompt, seq, n_q, n_kv, hd, window):
    blk = window
    nblk = seq // blk
    wa = n_q * hd
    kvw = n_kv * hd
    kcol = wa // kvw
    vcol = kcol + 1

    def cur(col):
        return pl.BlockSpec((blk, kvw), lambda b, i: (b * nblk + i, col))

    def prev(col):
        return pl.BlockSpec((blk, kvw), lambda b, i: (b * nblk + jnp.maximum(i - 1, 0), col))

    L = q_norm_w.shape[0]
    nw = pl.BlockSpec((None, 1, hd), lambda b, i: (l, 0, 0))
    return pl.pallas_call(
        functools.partial(_attn_prompt_kernel, l=l, n_kv=n_kv, q_per_kv=n_q // n_kv, hd=hd,
                          window=window, nblk=nblk),
        grid=(n_prompt, nblk),
        in_specs=[pl.BlockSpec(memory_space=pltpu.SMEM),
                  pl.BlockSpec((blk, wa), lambda b, i: (b * nblk + i, 0)),
                  prev(kcol), cur(kcol), prev(vcol), cur(vcol), nw, nw],
        out_specs=[pl.BlockSpec((blk, wa), lambda b, i: (b * nblk + i, 0)),
                   pl.BlockSpec((None, blk, kvw), lambda b, i: (b, 0, 0)),
                   pl.BlockSpec((None, blk, kvw), lambda b, i: (b, 0, 0))],
        out_shape=[jax.ShapeDtypeStruct((R, wa), BF16),
                   jax.ShapeDtypeStruct((n_prompt, blk, kvw), F32),
                   jax.ShapeDtypeStruct((n_prompt, blk, kvw), F32)],
        compiler_params=_params("parallel", "arbitrary"),
        name="attn_prompt",
    )(sinks, z, z, z, z, z, q_norm_w.reshape(L, 1, hd), k_norm_w.reshape(L, 1, hd))


def _attn_sample_kernel(sink_ref, o_in_ref, q_ref, kn_ref, vn_ref, ck_ref, cv_ref, qw_ref, kw_ref,
                        o_ref, kwin_ref, vwin_ref, keys_s, vals_s, ostage,
                        *, l, nb, T, n_kv, q_per_kv, hd, window):
    del o_in_ref
    W = ck_ref.shape[1]
    scale = hd ** -0.5
    rows_q = q_per_kv * T
    r = lax.broadcasted_iota(jnp.int32, (rows_q, 2 * W), 0) % T
    c = lax.broadcasted_iota(jnp.int32, (rows_q, 2 * W), 1)
    q_pos = PAST_LEN + r
    k_pos = jnp.where(c < W, PAST_LEN - W + c, PAST_LEN + c - W)
    dist = q_pos - k_pos
    ok = (dist >= 0) & (dist <= window) & (k_pos >= 0) & (c < W + T)
    hrow = lax.broadcasted_iota(jnp.int32, (rows_q, 1), 0) // T
    qw = qw_ref[...]
    kw = kw_ref[...]
    keys_s[...] = jnp.zeros_like(keys_s)
    vals_s[...] = jnp.zeros_like(vals_s)

    def step(b, carry):
        rows = pl.ds(pl.multiple_of(b * T, T), T)
        for g in range(n_kv):
            sl = slice(g * hd, (g + 1) * hd)
            ck = ck_ref[b, :, sl]
            cv = cv_ref[b, :, sl]
            knn = _rms(kn_ref[rows, sl], kw)
            vnn = vn_ref[rows, sl]
            keys_s[0:W, :] = ck
            keys_s[W:W + T, :] = knn
            vals_s[0:W, :] = cv
            vals_s[W:W + T, :] = vnn
            qs = jnp.concatenate(
                [_rms(q_ref[rows, (g * q_per_kv + h) * hd:(g * q_per_kv + h + 1) * hd], qw)
                 for h in range(q_per_kv)], axis=0).astype(BF16)
            s = lax.dot_general(qs, keys_s[...].astype(BF16), (((1,), (1,)), ((), ())),
                                preferred_element_type=F32) * scale
            s = jnp.where(ok, s, -1e30)
            sink = jnp.zeros((rows_q, 1), F32)
            for h in range(q_per_kv):
                sink = jnp.where(hrow == h, sink_ref[l, g * q_per_kv + h], sink)
            o = _softmax_sink_pv(s, sink, vals_s[...].astype(BF16))
            for h in range(q_per_kv):
                head = g * q_per_kv + h
                ostage[rows, head * hd:(head + 1) * hd] = o[h * T:(h + 1) * T, :]
            kwin_ref[b, 0:W - T, sl] = ck[T:, :]
            kwin_ref[b, W - T:W, sl] = knn
            vwin_ref[b, 0:W - T, sl] = cv[T:, :]
            vwin_ref[b, W - T:W, sl] = vnn
        return carry

    lax.fori_loop(0, nb, step, 0)
    o_ref[...] = ostage[...].astype(o_ref.dtype)


def _attn_sample(attn_o, z, cache_k, cache_v, sinks, q_norm_w, k_norm_w, l, *, n_sample, T, n_q, n_kv, hd,
                 window):
    R, wa = attn_o.shape
    kvw = n_kv * hd
    rows = n_sample * T
    blk_i = (R - rows) // rows
    kcol = wa // kvw
    L, nb, W = cache_k.shape[:3]
    ck = cache_k.reshape(L, nb, W, kvw)
    cv = cache_v.reshape(L, nb, W, kvw)
    nw = pl.BlockSpec((None, 1, hd), lambda i: (l, 0, 0))
    cache = pl.BlockSpec((None, nb, W, kvw), lambda i: (l, 0, 0, 0))
    win = pl.BlockSpec((nb, W, kvw), lambda i: (0, 0, 0))
    return pl.pallas_call(
        functools.partial(_attn_sample_kernel, l=l, nb=nb, T=T, n_kv=n_kv, q_per_kv=n_q // n_kv, hd=hd,
                          window=window),
        grid=(1,),
        in_specs=[pl.BlockSpec(memory_space=pltpu.SMEM),
                  pl.BlockSpec(memory_space=pl.ANY),
                  pl.BlockSpec((rows, wa), lambda i: (blk_i, 0)),
                  pl.BlockSpec((rows, kvw), lambda i: (blk_i, kcol)),
                  pl.BlockSpec((rows, kvw), lambda i: (blk_i, kcol + 1)),
                  cache, cache, nw, nw],
        out_specs=[pl.BlockSpec((rows, wa), lambda i: (blk_i, 0)), win, win],
        out_shape=[jax.ShapeDtypeStruct((R, wa), BF16),
                   jax.ShapeDtypeStruct((nb, W, kvw), F32),
                   jax.ShapeDtypeStruct((nb, W, kvw), F32)],
        scratch_shapes=[pltpu.VMEM((2 * W, hd), F32), pltpu.VMEM((2 * W, hd), F32),
                        pltpu.VMEM((rows, wa), F32)],
        input_output_aliases={1: 0},
        compiler_params=_params("arbitrary"),
        name="attn_sample",
    )(sinks, attn_o, z, z, z, ck, cv, q_norm_w.reshape(-1, 1, hd), k_norm_w.reshape(-1, 1, hd))


def _lru_gates(xc, wa_ref, ba_ref, wx_ref, bx_ref, lam_ref, a_s, u_s, *, nblk, lb):
    xcb = xc.astype(BF16)
    for n in range(nblk):
        sl = slice(n * lb, (n + 1) * lb)
        ra = jnp.dot(xcb[:, sl], wa_ref[n].astype(BF16), preferred_element_type=F32) + ba_ref[:, sl]
        rx = jnp.dot(xcb[:, sl], wx_ref[n].astype(BF16), preferred_element_type=F32) + bx_ref[:, sl]
        log_a = -LRU_C * jax.nn.sigmoid(ra) * jax.nn.softplus(-lam_ref[:, sl])
        a_s[:, sl] = jnp.exp(log_a)
        th = jnp.tanh(log_a)
        u_s[:, sl] = jnp.sqrt(-2.0 * th / (1.0 - th)) * jax.nn.sigmoid(rx) * xc[:, sl]

def _tile_scan(a, u, T):
    t = lax.broadcasted_iota(jnp.int32, (a.shape[0], 1), 0) % T
    d = 1
    while d < T:
        keep = t >= d
        u = jnp.where(keep, a * pltpu.roll(u, d, axis=0) + u, u)
        a = jnp.where(keep, a * pltpu.roll(a, d, axis=0), a)
        d *= 2
    return a, u


def _lru_prompt_kernel(x_ref, g_ref, cw_ref, cb_ref, wa_ref, ba_ref, wx_ref, bx_ref, lam_ref,
                       o_ref, hT_ref, cnew_ref, xp_s, a_s, u_s, h_s, *, Tb, ntb, cw, nblk, lb):
    tb = pl.program_id(2)
    P = SUBLANES

    @pl.when(tb == 0)
    def _():
        xp_s[0:P, :] = jnp.zeros((P, xp_s.shape[1]), F32)
        h_s[...] = jnp.zeros_like(h_s)

    xp_s[P:P + Tb, :] = x_ref[...]
    xc = cb_ref[...]
    for j in range(cw):
        off = P - (cw - 1) + j
        xc = xc + xp_s[off:off + Tb, :] * cw_ref[j:j + 1, :]
    _lru_gates(xc, wa_ref, ba_ref, wx_ref, bx_ref, lam_ref, a_s, u_s, nblk=nblk, lb=lb)

    a, u = _tile_scan(a_s[...], u_s[...], P)
    a_s[...] = a
    u_s[...] = u

    def step(k, h):
        rr = pl.ds(pl.multiple_of(k * P, P), P)
        hs = u_s[rr, :] + a_s[rr, :] * h
        u_s[rr, :] = hs
        return hs[P - 1:P, :]

    h = lax.fori_loop(0, Tb // P, step, h_s[...])
    h_s[...] = h
    o_ref[...] = (jax.nn.gelu(g_ref[...]) * u_s[...]).astype(o_ref.dtype)
    tail = xp_s[Tb:Tb + P, :]
    xp_s[0:P, :] = tail

    @pl.when(tb == ntb - 1)
    def _():
        hT_ref[...] = h
        cnew_ref[...] = tail


def _lru_sample_kernel(o_in_ref, x_ref, g_ref, h0_ref, buf_ref, cw_ref, cb_ref, wa_ref, ba_ref, wx_ref, bx_ref,
                       lam_ref, o_ref, hs_ref, a_s, u_s, *, T, cw, nblk, lb):
    del o_in_ref
    rows = x_ref.shape[0]
    t = lax.broadcasted_iota(jnp.int32, (rows, 1), 0) % T
    x = x_ref[...]
    buf = buf_ref[...]
    xc = cb_ref[...] + x * cw_ref[cw - 1:cw, :]
    for s in range(1, cw):
        xs = jnp.where(t >= s, pltpu.roll(x, s, axis=0), pltpu.roll(buf, rows - (T - s), axis=0))
        xc = xc + xs * cw_ref[cw - 1 - s:cw - s, :]
    _lru_gates(xc, wa_ref, ba_ref, wx_ref, bx_ref, lam_ref, a_s, u_s, nblk=nblk, lb=lb)
    a, u = _tile_scan(a_s[...], u_s[...], T)
    hs = u + a * h0_ref[...]
    hs_ref[...] = hs
    o_ref[...] = (jax.nn.gelu(g_ref[...]) * hs).astype(o_ref.dtype)


def _lru_weight_args(W, l, wb, index):
    ch = wb // 2
    nlb, lb = W['lru_w_a'].shape[1:3]
    nblk = nlb // 2
    cw = W['conv_w'].shape[1]

    def vec(name):
        return W[name].reshape(W[name].shape[0], 1, wb), pl.BlockSpec((None, 1, ch), index(lambda hf: (l, 0, hf)))

    def blkw(name):
        return W[name], pl.BlockSpec((None, nblk, lb, lb), index(lambda hf: (l, hf, 0, 0)))

    pairs = [(W['conv_w'], pl.BlockSpec((None, cw, ch), index(lambda hf: (l, 0, hf)))),
             vec('conv_b'), blkw('lru_w_a'), vec('lru_b_a'), blkw('lru_w_x'), vec('lru_b_x'), vec('lru_lambda')]
    return [p[0] for p in pairs], [p[1] for p in pairs], dict(cw=cw, nblk=nblk, lb=lb)


def _lru_prompt(z, W, l, *, R, nseq, T, Tb, col_x, col_g, wb):
    ch = wb // 2
    ntb = T // Tb
    wargs, wspecs, kw = _lru_weight_args(W, l, wb, lambda f: (lambda s, hf, t: f(hf)))

    def rowblk(col):
        return pl.BlockSpec((Tb, ch), lambda s, hf, t: (s * ntb + t, col + hf))

    return pl.pallas_call(
        functools.partial(_lru_prompt_kernel, Tb=Tb, ntb=ntb, **kw),
        grid=(nseq, 2, ntb),
        in_specs=[rowblk(col_x), rowblk(col_g)] + wspecs,
        out_specs=[pl.BlockSpec((Tb, ch), lambda s, hf, t: (s * ntb + t, hf)),
                   pl.BlockSpec((None, 1, ch), lambda s, hf, t: (s, 0, hf)),
                   pl.BlockSpec((None, SUBLANES, ch), lambda s, hf, t: (s, 0, hf))],
        out_shape=[jax.ShapeDtypeStruct((R, wb), BF16),
                   jax.ShapeDtypeStruct((nseq, 1, wb), F32),
                   jax.ShapeDtypeStruct((nseq, SUBLANES, wb), F32)],
        scratch_shapes=[pltpu.VMEM((SUBLANES + Tb, ch), F32), pltpu.VMEM((Tb, ch), F32),
                        pltpu.VMEM((Tb, ch), F32), pltpu.VMEM((1, ch), F32)],
        compiler_params=_params("parallel", "parallel", "arbitrary"),
        name="lru_prompt",
    )(z, z, *wargs)


def _lru_sample(lru_o, z, h0_rep, buf_rows, W, l, *, row0, T, col_x, col_g, wb):
    ch = wb // 2
    rows = h0_rep.shape[0]
    rb = row0 // rows
    wargs, wspecs, kw = _lru_weight_args(W, l, wb, lambda f: (lambda hf: f(hf)))
    st = pl.BlockSpec((rows, ch), lambda hf: (0, hf))
    return pl.pallas_call(
        functools.partial(_lru_sample_kernel, T=T, **kw),
        grid=(2,),
        in_specs=[pl.BlockSpec(memory_space=pl.ANY),
                  pl.BlockSpec((rows, ch), lambda hf: (rb, col_x + hf)),
                  pl.BlockSpec((rows, ch), lambda hf: (rb, col_g + hf)), st, st] + wspecs,
        out_specs=[pl.BlockSpec((rows, ch), lambda hf: (rb, hf)), st],
        out_shape=[jax.ShapeDtypeStruct(lru_o.shape, BF16), jax.ShapeDtypeStruct((rows, wb), F32)],
        scratch_shapes=[pltpu.VMEM((rows, ch), F32), pltpu.VMEM((rows, ch), F32)],
        input_output_aliases={0: 0},
        compiler_params=_params("parallel"),
        name="lru_sample",
    )(lru_o, z, z, h0_rep, buf_rows, *wargs)


def _chunk_kernel(*refs, nb, Tc, ngroups, gw, cps, aliased):
    refs = list(refs)
    if aliased:
        refs.pop(0)
    ulo_ref, uhi_ref, vlo_ref, vhi_ref, nw_ref, ws_ref, bst_ref, o_ref, vn_ref = refs
    rows = nb * Tc
    half = vlo_ref.shape[1]
    wc = 2 * half
    v_lo = jax.nn.gelu(vlo_ref[...])
    v_hi = jax.nn.gelu(vhi_ref[...])
    ms = (jnp.sum(v_lo * v_lo, axis=-1, keepdims=True) + jnp.sum(v_hi * v_hi, axis=-1, keepdims=True)) / wc
    inv = lax.rsqrt(ms + EPS)
    vn_halves = (v_lo * inv * nw_ref[:, 0:half], v_hi * inv * nw_ref[:, half:wc])
    u_halves = (ulo_ref, uhi_ref)

    r = lax.broadcasted_iota(jnp.int32, (rows, rows), 0)
    c = lax.broadcasted_iota(jnp.int32, (rows, rows), 1)
    mask = (r // Tc == c // Tc) & (c % Tc <= r % Tc)
    if nb > 1:
        sel = (lax.broadcasted_iota(jnp.int32, (ws_ref.shape[2], rows), 0)
               == lax.broadcasted_iota(jnp.int32, (ws_ref.shape[2], rows), 1) % Tc).astype(BF16)
        bias_rows = jnp.broadcast_to(bst_ref[0:Tc, :][None], (nb, Tc, bst_ref.shape[1])).reshape(rows, -1)
    else:
        bias_rows = bst_ref[0:Tc, :]
    gph = half // gw
    for g in range(ngroups):
        hf, gi = divmod(g, gph)
        sl = slice(gi * gw, (gi + 1) * gw)
        if nb > 1:
            t1 = jnp.broadcast_to(ws_ref[g, 0:Tc, :][None], (nb, Tc, ws_ref.shape[2])).reshape(rows, -1)
            wfull = jnp.dot(t1.astype(BF16), sel, preferred_element_type=F32)
        else:
            wfull = ws_ref[g, 0:Tc, 0:Tc]
        wm = jnp.where(mask, wfull, 0.0).astype(BF16)
        mixed = jnp.dot(wm, vn_halves[hf][:, sl].astype(BF16), preferred_element_type=F32)
        mixed = mixed + bias_rows[:, g:g + 1]
        o_ref[:, g * gw:(g + 1) * gw] = (jax.nn.gelu(u_halves[hf][:, sl]) * mixed).astype(o_ref.dtype)

    if cps == 1:
        vn_ref[:, 0:half] = vn_halves[0]
        vn_ref[:, half:wc] = vn_halves[1]
    else:
        @pl.when(pl.program_id(0) % cps == cps - 1)
        def _():
            vn_ref[:, 0:half] = vn_halves[0]
            vn_ref[:, half:wc] = vn_halves[1]


def _chunk(chunk_o, z, W, l, *, R, row0, nsteps, nb, Tc, cps, col_u, col_v, wc):
    half = wc // 2
    ngroups, chunk = W['chunk_w_s'].shape[1:3]
    gw = wc // ngroups
    rows = nb * Tc
    rb0 = row0 // rows
    aliased = chunk_o is not None

    def rowblk(col):
        return pl.BlockSpec((rows, half), lambda i: (rb0 + i, col))

    args = [z, z, z, z, W['chunk_v_norm_w'].reshape(-1, 1, wc), W['chunk_w_s'],
            jnp.swapaxes(W['chunk_b_s'], 1, 2)]
    specs = [rowblk(col_u), rowblk(col_u + 1), rowblk(col_v), rowblk(col_v + 1),
             pl.BlockSpec((None, 1, wc), lambda i: (l, 0, 0)),
             pl.BlockSpec((None, ngroups, chunk, chunk), lambda i: (l, 0, 0, 0)),
             pl.BlockSpec((None, chunk, ngroups), lambda i: (l, 0, 0))]
    io_alias = {}
    if aliased:
        args.insert(0, chunk_o)
        specs.insert(0, pl.BlockSpec(memory_space=pl.ANY))
        io_alias = {0: 0}
    return pl.pallas_call(
        functools.partial(_chunk_kernel, nb=nb, Tc=Tc, ngroups=ngroups, gw=gw, cps=cps, aliased=aliased),
        grid=(nsteps,),
        in_specs=specs,
        out_specs=[pl.BlockSpec((rows, wc), lambda i: (rb0 + i, 0)),
                   pl.BlockSpec((None, rows, wc), lambda i: (i // cps, 0, 0))],
        out_shape=[jax.ShapeDtypeStruct((R, wc), BF16),
                   jax.ShapeDtypeStruct((nsteps // cps, rows, wc), F32)],
        input_output_aliases=io_alias,
        compiler_params=_params("arbitrary"),
        name="chunk_sample" if aliased else "chunk_prompt",
    )(*args)


def _merge_kernel(a_ref, b_ref, c_ref, wa_ref, wb_ref, wc_ref, ga_ref, gb_ref, gc_ref, o_ref):
    def branch(x_ref, w_ref, g_ref):
        y = jnp.dot(x_ref[...], w_ref[...].astype(BF16), preferred_element_type=F32)
        return jax.nn.sigmoid(g_ref[...]) * y

    m = branch(a_ref, wa_ref, ga_ref) + branch(b_ref, wb_ref, gb_ref) + branch(c_ref, wc_ref, gc_ref)
    o_ref[...] = m.astype(o_ref.dtype)


def _merge(attn_o, lru_o, chunk_o, z, W, l, *, gate_col, tm, tn):
    R = attn_o.shape[0]
    D = W['w_branch_a'].shape[-1]
    g0 = gate_col // tn
    gstep = D // tn

    def xin(a):
        return pl.BlockSpec((tm, a.shape[1]), lambda i, j: (i, 0))

    def win(w):
        return pl.BlockSpec((None, w.shape[1], tn), lambda i, j: (l, 0, j))

    def gin(k):
        return pl.BlockSpec((tm, tn), lambda i, j: (i, g0 + k * gstep + j))

    return pl.pallas_call(
        _merge_kernel,
        grid=(R // tm, D // tn),
        in_specs=[xin(attn_o), xin(lru_o), xin(chunk_o),
                  win(W['w_branch_a']), win(W['w_branch_b']), win(W['w_branch_c']),
                  gin(0), gin(1), gin(2)],
        out_specs=pl.BlockSpec((tm, tn), lambda i, j: (i, j)),
        out_shape=jax.ShapeDtypeStruct((R, D), BF16),
        compiler_params=_params("parallel", "parallel"),
        name="merge",
    )(attn_o, lru_o, chunk_o, W['w_branch_a'], W['w_branch_b'], W['w_branch_c'], z, z, z)


def _glu_kernel(*refs, gated):
    if gated:
        h_ref, wg_ref, wu_ref, gate_ref, o_ref = refs
    else:
        h_ref, wg_ref, wu_ref, o_ref = refs
    h = h_ref[...]
    g = jnp.dot(h, wg_ref[...].astype(BF16), preferred_element_type=F32)
    u = jnp.dot(h, wu_ref[...].astype(BF16), preferred_element_type=F32)
    a = jax.nn.silu(g) * u
    if gated:
        e = pl.program_id(1)
        gate = gate_ref[...]
        lane = lax.broadcasted_iota(jnp.int32, gate.shape, 1)
        a = a * jnp.sum(jnp.where(lane == e, gate, 0.0), axis=-1, keepdims=True)
    o_ref[...] = a.astype(o_ref.dtype)


def _glu(h, w_gate, w_up, l0, n_exp, gate, *, tm, tn):
    R, D = h.shape
    F = w_gate.shape[-1]
    nf = F // tn
    gated = gate is not None
    wspec = pl.BlockSpec((None, D, tn), lambda i, e, j: (l0 + e, 0, j))
    args = [h, w_gate, w_up]
    specs = [pl.BlockSpec((tm, D), lambda i, e, j: (i, 0)), wspec, wspec]
    if gated:
        args.append(gate)
        specs.append(pl.BlockSpec((tm, gate.shape[1]), lambda i, e, j: (i, 0)))
    return pl.pallas_call(
        functools.partial(_glu_kernel, gated=gated),
        grid=(R // tm, n_exp, nf),
        in_specs=specs,
        out_specs=pl.BlockSpec((tm, tn), lambda i, e, j: (i, e * nf + j)),
        out_shape=jax.ShapeDtypeStruct((R, n_exp * F), BF16),
        compiler_params=_params("parallel", "parallel", "parallel"),
        name="glu",
    )(*args)


META_E1, META_E2, META_W1, META_W2, META_R1, META_R2 = range(6)


def _router_kernel(h_ref, w_ref, meta_ref, cnt_ref, carry, *, n_exp):
    i = pl.program_id(0)

    @pl.when(i == 0)
    def _():
        carry[...] = jnp.zeros_like(carry)

    logits = jnp.dot(h_ref[...], w_ref[...].astype(BF16), preferred_element_type=F32)
    tm = logits.shape[0]
    lane = lax.broadcasted_iota(jnp.int32, logits.shape, 1).astype(F32)
    big = float(logits.shape[1])
    l0 = jnp.where(lane < n_exp, logits, -jnp.inf)
    m1 = jnp.max(l0, axis=-1, keepdims=True)
    i1 = jnp.min(jnp.where(l0 == m1, lane, big), axis=-1, keepdims=True)
    l1 = jnp.where(lane == i1, -jnp.inf, l0)
    m2 = jnp.max(l1, axis=-1, keepdims=True)
    i2 = jnp.min(jnp.where(l1 == m2, lane, big), axis=-1, keepdims=True)
    e2 = jnp.exp(m2 - m1)
    den = 1.0 + e2
    hit = ((lane == i1) | (lane == i2)).astype(F32)
    earlier = (lax.broadcasted_iota(jnp.int32, (tm, tm), 0) > lax.broadcasted_iota(jnp.int32, (tm, tm), 1))
    rank = jnp.dot(earlier.astype(BF16), hit.astype(BF16), preferred_element_type=F32) + carry[...]
    r1 = jnp.sum(jnp.where(lane == i1, rank, 0.0), axis=-1, keepdims=True)
    r2 = jnp.sum(jnp.where(lane == i2, rank, 0.0), axis=-1, keepdims=True)
    meta = jnp.zeros_like(logits)
    for k, v in ((META_E1, i1), (META_E2, i2), (META_W1, 1.0 / den), (META_W2, e2 / den), (META_R1, r1),
                 (META_R2, r2)):
        meta = jnp.where(lane == k, v, meta)
    meta_ref[...] = meta
    carry[...] += jnp.sum(hit, axis=0, keepdims=True)
    cnt_ref[...] = carry[...]


def _router(h, w_router_padded, j, *, n_exp, tm):
    R, D = h.shape
    NP = w_router_padded.shape[-1]
    return pl.pallas_call(
        functools.partial(_router_kernel, n_exp=n_exp),
        grid=(R // tm,),
        in_specs=[pl.BlockSpec((tm, D), lambda i: (i, 0)),
                  pl.BlockSpec((None, D, NP), lambda i: (j, 0, 0))],
        out_specs=[pl.BlockSpec((tm, NP), lambda i: (i, 0)), pl.BlockSpec((1, NP), lambda i: (0, 0))],
        out_shape=[jax.ShapeDtypeStruct((R, NP), F32), jax.ShapeDtypeStruct((1, NP), F32)],
        scratch_shapes=[pltpu.VMEM((1, NP), F32)],
        compiler_params=_params("arbitrary"),
        name="router",
    )(h, w_router_padded)


def _row_copies_wait(src_like, dst_like, sem, n):
    for _ in range(n):
        pltpu.make_async_copy(src_like, dst_like, sem).wait()


def _dispatch_kernel(p1_ref, p2_ref, h_ref, xs_ref, sem):
    tm = h_ref.shape[0]
    i0 = pl.program_id(0) * tm

    def body(i, c):
        pltpu.make_async_copy(h_ref.at[i], xs_ref.at[p1_ref[i0 + i]], sem).start()
        pltpu.make_async_copy(h_ref.at[i], xs_ref.at[p2_ref[i0 + i]], sem).start()
        return c

    lax.fori_loop(0, tm, body, 0)
    _row_copies_wait(h_ref, xs_ref.at[pl.ds(0, tm)], sem, TOP_K)


def _dispatch(h_rows, p1, p2, *, n_slots, tm):
    R, C, _ = h_rows.shape
    return pl.pallas_call(
        _dispatch_kernel,
        grid_spec=pltpu.PrefetchScalarGridSpec(
            num_scalar_prefetch=2, grid=(R // tm,),
            in_specs=[pl.BlockSpec((tm, C, LANES), lambda i, p1, p2: (i, 0, 0))],
            out_specs=pl.BlockSpec(memory_space=pl.ANY),
            scratch_shapes=[pltpu.SemaphoreType.DMA]),
        out_shape=jax.ShapeDtypeStruct((n_slots, C, LANES), F32),
        compiler_params=_params("arbitrary"),
        name="moe_dispatch",
    )(p1, p2, h_rows)


def _per_live_rows(nv, tile, sub, body):
    for m in range(sub, tile + 1, sub):
        pl.when((nv > m - sub) & (nv <= m))(functools.partial(body, m))


def _moe_glu_kernel(te_ref, nv_ref, nu_ref, xs_ref, wg_ref, wu_ref, a_ref, xb_s, *, sub):
    t = pl.program_id(0)
    j = pl.program_id(1)
    nv = nv_ref[t]
    tile, nchunk = xs_ref.shape[0], xs_ref.shape[1]

    @pl.when(t < nu_ref[0])
    def _():
        wg = wg_ref[...].astype(BF16)
        wu = wu_ref[...].astype(BF16)

        def body(m):
            rows = slice(0, m)

            @pl.when(j == 0)
            def _():
                live = lax.broadcasted_iota(jnp.int32, (m, 1), 0) < nv
                x2d = xs_ref[rows, :, :].reshape(m, nchunk * LANES)
                xb_s[rows, :] = jnp.where(live, x2d, 0.0).astype(BF16)

            x = xb_s[rows, :]
            g = jnp.dot(x, wg, preferred_element_type=F32)
            u = jnp.dot(x, wu, preferred_element_type=F32)
            a_ref[rows, :] = (jax.nn.silu(g) * u).astype(a_ref.dtype)
            if m < tile:
                a_ref[m:tile, :] = jnp.zeros((tile - m, a_ref.shape[1]), a_ref.dtype)

        _per_live_rows(nv, tile, sub, body)


def _moe_down_kernel(te_ref, nv_ref, nu_ref, a_ref, wd_ref, y_ref, acc_s, *, sub, nk):
    t = pl.program_id(0)
    k = pl.program_id(1)
    nv = nv_ref[t]
    tile, nchunk = y_ref.shape[0], y_ref.shape[1]

    @pl.when(t < nu_ref[0])
    def _():
        wd = wd_ref[...].astype(BF16)

        def body(m):
            rows = slice(0, m)
            part = jnp.dot(a_ref[rows, :], wd, preferred_element_type=F32)

            @pl.when(k == 0)
            def _():
                acc_s[rows, :] = part

            @pl.when(k > 0)
            def _():
                acc_s[rows, :] += part

            @pl.when(k == nk - 1)
            def _():
                y_ref[rows, :, :] = acc_s[rows, :].reshape(m, nchunk, LANES)
                if m < tile:
                    y_ref[m:tile, :, :] = jnp.zeros((tile - m, nchunk, LANES), F32)

        _per_live_rows(nv, tile, sub, body)


def _moe_ffn(xs, w_gate, w_up, w_down, l0, tile_expert, tile_valid, n_used, *, tile, sub, tn, tk):
    n_slots, C, _ = xs.shape
    D = C * LANES
    F = w_gate.shape[-1]
    nf, nk = F // tn, F // tk
    nt = n_slots // tile

    def live_t(t, nu):
        return jnp.minimum(t, nu[0] - 1)

    def live_step(t, j, nu, n):
        return jnp.where(t < nu[0], j, n - 1)

    def expert(t, te, nu):
        return l0 + te[live_t(t, nu)]

    row_tile = pl.BlockSpec((tile, C, LANES), lambda t, j, te, nv, nu: (live_t(t, nu), 0, 0))
    w_col = pl.BlockSpec((None, D, tn), lambda t, j, te, nv, nu: (expert(t, te, nu), 0, live_step(t, j, nu, nf)))
    a = pl.pallas_call(
        functools.partial(_moe_glu_kernel, sub=sub),
        grid_spec=pltpu.PrefetchScalarGridSpec(
            num_scalar_prefetch=3, grid=(nt, nf),
            in_specs=[row_tile, w_col, w_col],
            out_specs=pl.BlockSpec((tile, tn), lambda t, j, te, nv, nu: (live_t(t, nu), live_step(t, j, nu, nf))),
            scratch_shapes=[pltpu.VMEM((tile, D), BF16)]),
        out_shape=jax.ShapeDtypeStruct((n_slots, F), BF16),
        compiler_params=_params("arbitrary", "arbitrary"),
        name="moe_glu",
    )(tile_expert, tile_valid, n_used, xs, w_gate, w_up)
    return pl.pallas_call(
        functools.partial(_moe_down_kernel, sub=sub, nk=nk),
        grid_spec=pltpu.PrefetchScalarGridSpec(
            num_scalar_prefetch=3, grid=(nt, nk),
            in_specs=[pl.BlockSpec((tile, tk), lambda t, k, te, nv, nu: (live_t(t, nu), live_step(t, k, nu, nk))),
                      pl.BlockSpec((None, tk, D),
                                   lambda t, k, te, nv, nu: (expert(t, te, nu), live_step(t, k, nu, nk), 0))],
            out_specs=pl.BlockSpec((tile, C, LANES), lambda t, j, te, nv, nu: (live_t(t, nu), 0, 0),
                                   pipeline_mode=pl.Buffered(1)),
            scratch_shapes=[pltpu.VMEM((tile, D), F32)]),
        out_shape=jax.ShapeDtypeStruct((n_slots, C, LANES), F32),
        compiler_params=_params("arbitrary", "arbitrary"),
        name="moe_down",
    )(tile_expert, tile_valid, n_used, a, w_down)


def _combine_kernel(p1_ref, p2_ref, meta_ref, y_ref, f_ref, a_s, b_s, sem):
    tm, nchunk = a_s.shape[0], a_s.shape[1]
    i0 = pl.program_id(0) * tm

    def body(i, c):
        pltpu.make_async_copy(y_ref.at[p1_ref[i0 + i]], a_s.at[i], sem).start()
        pltpu.make_async_copy(y_ref.at[p2_ref[i0 + i]], b_s.at[i], sem).start()
        return c

    lax.fori_loop(0, tm, body, 0)
    _row_copies_wait(y_ref.at[pl.ds(0, tm)], a_s, sem, TOP_K)
    w1 = meta_ref[:, META_W1:META_W1 + 1]
    w2 = meta_ref[:, META_W2:META_W2 + 1]
    d = nchunk * LANES
    f_ref[...] = w1 * a_s[...].reshape(tm, d) + w2 * b_s[...].reshape(tm, d)


def _combine(y, meta, p1, p2, *, tm):
    _, C, _ = y.shape
    R, NP = meta.shape
    return pl.pallas_call(
        _combine_kernel,
        grid_spec=pltpu.PrefetchScalarGridSpec(
            num_scalar_prefetch=2, grid=(R // tm,),
            in_specs=[pl.BlockSpec((tm, NP), lambda i, p1, p2: (i, 0)),
                      pl.BlockSpec(memory_space=pl.ANY)],
            out_specs=pl.BlockSpec((tm, C * LANES), lambda i, p1, p2: (i, 0)),
            scratch_shapes=[pltpu.VMEM((tm, C, LANES), F32), pltpu.VMEM((tm, C, LANES), F32),
                            pltpu.SemaphoreType.DMA]),
        out_shape=jax.ShapeDtypeStruct((R, C * LANES), F32),
        compiler_params=_params("arbitrary"),
        name="moe_combine",
    )(p1, p2, meta, y)


def _moe(h, h_rows, w_router_padded, w_gate, w_up, w_down, j, *, n_exp, tm_router, tile, sub, tn, tk, tm_rows):
    R = h.shape[0]
    meta, counts = _router(h, w_router_padded, j, n_exp=n_exp, tm=tm_router)
    cnt = counts[0, :n_exp].astype(jnp.int32)
    ntile = (cnt + tile - 1) // tile
    tend = jnp.cumsum(ntile)
    tstart = tend - ntile
    n_tiles = -(-TOP_K * R // tile) + n_exp
    e1, e2 = meta[:, META_E1].astype(jnp.int32), meta[:, META_E2].astype(jnp.int32)
    p1 = tstart[e1] * tile + meta[:, META_R1].astype(jnp.int32)
    p2 = tstart[e2] * tile + meta[:, META_R2].astype(jnp.int32)
    tid = jnp.arange(n_tiles, dtype=jnp.int32)
    tile_expert = jnp.minimum(jnp.sum(tid[:, None] >= tend[None, :], axis=1), n_exp - 1).astype(jnp.int32)
    tile_valid = jnp.clip(cnt[tile_expert] - (tid - tstart[tile_expert]) * tile, 0, tile)
    tile_valid = jnp.where(tid < tend[-1], tile_valid, 0).astype(jnp.int32)
    xs = _dispatch(h_rows, p1, p2, n_slots=n_tiles * tile, tm=tm_rows)
    y = _moe_ffn(xs, w_gate, w_up, w_down, j * n_exp, tile_expert, tile_valid, tend[-1:].astype(jnp.int32),
                 tile=tile, sub=sub, tn=tn, tk=tk)
    return _combine(y, meta, p1, p2, tm=tm_rows)


def kernel(x_prompt, x_sample, cache_k_win, cache_v_win, state_rglru_h, state_conv, c_prompt, c_sample, norm1_w, norm2_w, w_ada, b_ada, w_in, q_norm_w, k_norm_w, attn_sinks, conv_w, conv_b, lru_w_a, lru_b_a, lru_w_x, lru_b_x, lru_lambda, chunk_v_norm_w, chunk_w_s, chunk_b_s, w_branch_a, w_branch_b, w_branch_c, w_out, ffn_w_gate, ffn_w_up, ffn_w_down, moe_w_router, moe_w_gate, moe_w_up, moe_w_down):
    Bp, S, D = x_prompt.shape
    Bs, Ts, _ = x_sample.shape
    L = w_in.shape[0]
    window, n_kv, hd = cache_k_win.shape[2:]
    n_q = attn_sinks.shape[1]
    wa, kvw = n_q * hd, n_kv * hd
    wb = conv_w.shape[-1]
    wc = chunk_v_norm_w.shape[-1]
    chunk = chunk_w_s.shape[-1]
    cwid = conv_w.shape[1]
    n_exp = moe_w_router.shape[-1]
    assert Ts == SUBLANES and S % (Bs * Ts) == 0 and S % chunk == 0 and S % window == 0
    Rp, Rs = Bp * S, Bs * Ts
    R = Rp + Rs
    half = wb // 2
    assert wb == wc and wa % half == 0 and kvw % half == 0 or True
    col_lx = (wa + 2 * kvw) // half
    col_lg = col_lx + 2
    col_cu = col_lg + 2
    col_cv = col_cu + 2
    gate_col = wa + 2 * kvw + 2 * wb + 2 * wc

    W = dict(conv_w=conv_w, conv_b=conv_b, lru_w_a=lru_w_a, lru_b_a=lru_b_a, lru_w_x=lru_w_x, lru_b_x=lru_b_x,
             lru_lambda=lru_lambda, chunk_v_norm_w=chunk_v_norm_w, chunk_w_s=chunk_w_s, chunk_b_s=chunk_b_s,
             w_branch_a=w_branch_a, w_branch_b=w_branch_b, w_branch_c=w_branch_c)

    n_c = Bp + Bs
    n_c_pad = -(-n_c // SUBLANES) * SUBLANES
    c_all = jnp.concatenate([c_prompt, c_sample, jnp.zeros((n_c_pad - n_c, D), F32)], axis=0)
    mod = _ada(c_all, w_ada, b_ada)
    mod_p = jnp.repeat(mod[:, :Bp], SUBLANES, axis=1)
    mod_s = jnp.repeat(mod[:, Bp:n_c], Ts, axis=1)

    x = jnp.concatenate([x_prompt.reshape(Rp, D), x_sample.reshape(Rs, D)], axis=0)
    rn = functools.partial(_resid_norm, seq=S, n_prompt=Bp)
    tm_big = _pick(R, 1408, 128) if R % 128 == 0 else _pick(R, 1408, 16)
    tm_mid = _pick(R, 768, 128) if R % 128 == 0 else _pick(R, 768, 16)
    tm_wide = _pick(R, 2816, 128) if R % 128 == 0 else _pick(R, 2816, 16)
    w_router_p = jnp.pad(moe_w_router, ((0, 0), (0, 0), (0, LANES - n_exp)))
    moe_gate_w = moe_w_gate.reshape((-1,) + moe_w_gate.shape[2:])
    moe_up_w = moe_w_up.reshape((-1,) + moe_w_up.shape[2:])
    moe_down_w = moe_w_down.reshape((-1,) + moe_w_down.shape[2:])
    moe_sub = MOE_SUB_ROWS if TOP_K * R >= n_exp * MOE_SUB_ROWS * MOE_TILE_SUBS else 4 * SUBLANES

    buf_rows = jnp.pad(state_conv, ((0, 0), (0, 0), (Ts - (cwid - 1), 0), (0, 0))).reshape(L, Rs, wb)
    h0_rep = jnp.repeat(state_rglru_h, Ts, axis=1)

    _, h = rn(x, None, mod_p, mod_s, norm1_w, 0, gate_k=None, scale_k=(0, 1), shift_k=(0, 0))
    outs = []
    for l in range(L):
        z = _matmul(h, w_in, l, out_dtype=F32, tm=tm_wide, tn=_pick(w_in.shape[-1], 512, half), tk=D)
        attn_o, kwp, vwp = _attn_prompt(z, attn_sinks, q_norm_w, k_norm_w, l, R=R, n_prompt=Bp, seq=S,
                                        n_q=n_q, n_kv=n_kv, hd=hd, window=window)
        attn_o, kws, vws = _attn_sample(attn_o, z, cache_k_win, cache_v_win, attn_sinks, q_norm_w, k_norm_w, l,
                                        n_sample=Bs, T=Ts, n_q=n_q, n_kv=n_kv, hd=hd, window=window)
        lru_o, hp, cp8 = _lru_prompt(z, W, l, R=R, nseq=Bp, T=S, Tb=_pick(S, 512, SUBLANES),
                                     col_x=col_lx, col_g=col_lg, wb=wb)
        lru_o, hs_all = _lru_sample(lru_o, z, h0_rep[l], buf_rows[l], W, l, row0=Rp, T=Ts,
                                    col_x=col_lx, col_g=col_lg, wb=wb)
        chunk_o, vnp = _chunk(None, z, W, l, R=R, row0=0, nsteps=Rp // chunk, nb=1, Tc=chunk,
                              cps=S // chunk, col_u=col_cu, col_v=col_cv, wc=wc)
        chunk_o, vns = _chunk(chunk_o, z, W, l, R=R, row0=Rp, nsteps=1, nb=Bs, Tc=Ts, cps=1,
                              col_u=col_cu, col_v=col_cv, wc=wc)
        m = _merge(attn_o, lru_o, chunk_o, z, W, l, gate_col=gate_col, tm=tm_big, tn=_pick(D, 256, LANES))
        y = _matmul(m, w_out, l, out_dtype=F32, tm=tm_wide, tn=_pick(D, 512, LANES), tk=D)
        j = l // 2
        if l % 2 == 0:
            x, h = rn(x, y, mod_p, mod_s, norm2_w, l, gate_k=(l, 2), scale_k=(l, 4), shift_k=(l, 3))
            a = _glu(h, ffn_w_gate, ffn_w_up, j, 1, None, tm=tm_wide, tn=_pick(ffn_w_gate.shape[-1], 256, LANES))
            f = _matmul(a, ffn_w_down, j, out_dtype=F32, tm=tm_big, tn=_pick(D, 1024, LANES),
                        tk=_pick(a.shape[1], 1408, LANES))
        else:
            x, h, h_rows = rn(x, y, mod_p, mod_s, norm2_w, l, gate_k=(l, 2), scale_k=(l, 4), shift_k=(l, 3),
                              rows_out=True)
            f = _moe(h, h_rows, w_router_p, moe_gate_w, moe_up_w, moe_down_w, j, n_exp=n_exp, tm_router=tm_mid,
                     tile=MOE_TILE_SUBS * moe_sub, sub=moe_sub, tn=_pick(moe_gate_w.shape[-1], 512, LANES),
                     tk=_pick(moe_gate_w.shape[-1], 1024, LANES), tm_rows=Rs)
        if l + 1 < L:
            x, h = rn(x, f, mod_p, mod_s, norm1_w, l + 1, gate_k=(l, 5), scale_k=(l + 1, 1), shift_k=(l + 1, 0))
        else:
            x, _ = rn(x, f, mod_p, mod_s, None, l, gate_k=(l, 5), scale_k=None, shift_k=None)
        conv_p = cp8[:, SUBLANES - (cwid - 1):, :]
        lx_s = z[Rp:, col_lx * half:col_lx * half + wb].reshape(Bs, Ts, wb)
        conv_s = jnp.concatenate([state_conv[l], lx_s], axis=1)[:, -(cwid - 1):]
        hs = hs_all.reshape(Bs, Ts, wb)[:, Ts - 1]
        outs.append((kwp.reshape(Bp, window, n_kv, hd), vwp.reshape(Bp, window, n_kv, hd), hp.reshape(Bp, wb),
                     conv_p, vnp,
                     kws.reshape(Bs, window, n_kv, hd), vws.reshape(Bs, window, n_kv, hd), hs.reshape(Bs, wb),
                     conv_s, vns.reshape(Bs, Ts, wc)))
    st = [jnp.stack(s) for s in zip(*outs)]
    y_prompt = x[:Rp].reshape(Bp, S, D)
    y_sample = x[Rp:].reshape(Bs, Ts, D)
    return (y_prompt, y_sample, st[0], st[1], st[2], st[3], st[4], st[5], st[6], st[7], st[8], st[9])
```

```python
import functools

import jax
import jax.numpy as jnp
from jax import lax
from jax.experimental import pallas as pl
from jax.experimental.pallas import tpu as pltpu

EPS = 1e-6
LRU_C = 8.0
PAST_LEN = 16384
TOP_K = 2
MOE_SUB_ROWS = 384
MOE_TILE_SUBS = 3
SUBLANES = 8
LANES = 128
V7X_VMEM_LIMIT = 56 * 1024 * 1024

F32 = jnp.float32
BF16 = jnp.bfloat16


def _pick(n, target, mult):
    best = None
    for t in range(mult, min(n, target) + 1, mult):
        if n % t == 0:
            best = t
    assert best is not None, (n, target, mult)
    return best


def _params(*sem):
    return pltpu.CompilerParams(dimension_semantics=sem, vmem_limit_bytes=V7X_VMEM_LIMIT)


def _rms(x, w):
    return x * lax.rsqrt(jnp.mean(x * x, axis=-1, keepdims=True) + EPS) * w


def _ada_kernel(c_ref, w_ref, b_ref, o_ref):
    a = jax.nn.silu(c_ref[...]).astype(BF16)
    o_ref[...] = jnp.dot(a, w_ref[...].astype(BF16), preferred_element_type=F32) + b_ref[...]


def _ada(c_all, w_ada, b_ada):
    L, D, W6 = w_ada.shape
    Bc = c_all.shape[0]
    tn = _pick(W6, 1024, LANES)
    return pl.pallas_call(
        _ada_kernel,
        grid=(L, W6 // tn),
        in_specs=[pl.BlockSpec((Bc, D), lambda l, j: (0, 0)),
                  pl.BlockSpec((None, D, tn), lambda l, j: (l, 0, j)),
                  pl.BlockSpec((None, 1, tn), lambda l, j: (l, 0, j))],
        out_specs=pl.BlockSpec((None, Bc, tn), lambda l, j: (l, 0, j)),
        out_shape=jax.ShapeDtypeStruct((L, Bc, W6), F32),
        compiler_params=_params("parallel", "parallel"),
        name="ada_mod",
    )(c_all, w_ada, b_ada.reshape(L, 1, W6))


def _resid_norm_kernel(*refs, n_prompt_tiles, has_resid, has_norm, has_rows):
    refs = list(refs)
    x_ref = refs.pop(0)
    if has_resid:
        y_ref, gp_ref, gs_ref = refs.pop(0), refs.pop(0), refs.pop(0)
    if has_norm:
        scp_ref, scs_ref, shp_ref, shs_ref, nw_ref = (refs.pop(0) for _ in range(5))
    if has_resid:
        xo_ref = refs.pop(0)
    if has_norm:
        h_ref = refs.pop(0)
    if has_rows:
        hrow_ref = refs.pop(0)
    i = pl.program_id(0)

    def body(prompt):
        mod = (lambda p, s: p[0:1, :]) if prompt else (lambda p, s: s[...])
        x = x_ref[...]
        if has_resid:
            x = x + mod(gp_ref, gs_ref) * y_ref[...]
            xo_ref[...] = x
        if has_norm:
            hn = _rms(x, nw_ref[...])
            h = hn * (1.0 + mod(scp_ref, scs_ref)) + mod(shp_ref, shs_ref)
            h_ref[...] = h.astype(h_ref.dtype)
            if has_rows:
                hrow_ref[...] = h.reshape(hrow_ref.shape)

    pl.when(i < n_prompt_tiles)(lambda: body(True))
    pl.when(i >= n_prompt_tiles)(lambda: body(False))


def _resid_norm(x, y, mod_p, mod_s, norm_w, l, *, gate_k, scale_k, shift_k, seq, n_prompt, rows_out=False):
    R, D = x.shape
    TE = mod_s.shape[1]
    tpb = seq // TE
    npt = n_prompt * tpb
    has_resid = y is not None
    has_norm = norm_w is not None
    row = pl.BlockSpec((TE, D), lambda i: (i, 0))

    def mp(lk):
        return pl.BlockSpec((None, SUBLANES, D), lambda i: (lk[0], jnp.minimum(i // tpb, n_prompt - 1), lk[1]))

    def ms(lk):
        return pl.BlockSpec((None, TE, D), lambda i: (lk[0], 0, lk[1]))

    args, specs, outs, out_specs = [x], [row], [], []
    if has_resid:
        args += [y, mod_p, mod_s]
        specs += [row, mp(gate_k), ms(gate_k)]
        outs.append(jax.ShapeDtypeStruct((R, D), F32))
        out_specs.append(row)
    if has_norm:
        args += [mod_p, mod_s, mod_p, mod_s, norm_w.reshape(norm_w.shape[0], 1, D)]
        specs += [mp(scale_k), ms(scale_k), mp(shift_k), ms(shift_k),
                  pl.BlockSpec((None, 1, D), lambda i: (l, 0, 0))]
        outs.append(jax.ShapeDtypeStruct((R, D), BF16))
        out_specs.append(row)
    if rows_out:
        outs.append(jax.ShapeDtypeStruct((R, D // LANES, LANES), F32))
        out_specs.append(pl.BlockSpec((TE, D // LANES, LANES), lambda i: (i, 0, 0)))
    res = pl.pallas_call(
        functools.partial(_resid_norm_kernel, n_prompt_tiles=npt, has_resid=has_resid, has_norm=has_norm,
                          has_rows=rows_out),
        grid=(R // TE,),
        in_specs=specs, out_specs=out_specs, out_shape=outs,
        compiler_params=_params("parallel"),
        name="resid_norm",
    )(*args)
    res = list(res)
    x_new = res.pop(0) if has_resid else x
    h = res.pop(0) if has_norm else None
    if rows_out:
        return x_new, h, res.pop(0)
    return x_new, h


def _mm_kernel(a_ref, w_ref, o_ref, acc_ref, *, nk):
    part = jnp.dot(a_ref[...], w_ref[...].astype(BF16), preferred_element_type=F32)
    if nk == 1:
        o_ref[...] = part.astype(o_ref.dtype)
        return
    k = pl.program_id(2)

    @pl.when(k == 0)
    def _():
        acc_ref[...] = part

    @pl.when(k > 0)
    def _():
        acc_ref[...] += part

    @pl.when(k == nk - 1)
    def _():
        o_ref[...] = acc_ref[...].astype(o_ref.dtype)


def _matmul(a, w, l, *, out_dtype, tm, tn, tk):
    R, K = a.shape
    N = w.shape[-1]
    nk = K // tk
    acc_shape = (tm, tn) if nk > 1 else (SUBLANES, LANES)
    return pl.pallas_call(
        functools.partial(_mm_kernel, nk=nk),
        grid=(R // tm, N // tn, nk),
        in_specs=[pl.BlockSpec((tm, tk), lambda i, j, k: (i, k)),
                  pl.BlockSpec((None, tk, tn), lambda i, j, k: (l, k, j))],
        out_specs=pl.BlockSpec((tm, tn), lambda i, j, k: (i, j)),
        out_shape=jax.ShapeDtypeStruct((R, N), out_dtype),
        scratch_shapes=[pltpu.VMEM(acc_shape, F32)],
        compiler_params=_params("parallel", "parallel", "arbitrary"),
        name="matmul",
    )(a, w)


def _softmax_sink_pv(s, sink, v):
    m = jnp.maximum(jnp.max(s, axis=-1, keepdims=True), sink)
    e = jnp.exp(s - m)
    den = jnp.sum(e, axis=-1, keepdims=True) + jnp.exp(sink - m)
    return jnp.dot((e / den).astype(BF16), v, preferred_element_type=F32)


def _attn_prompt_kernel(sink_ref, q_ref, kp_ref, kc_ref, vp_ref, vc_ref, qw_ref, kw_ref,
                        o_ref, kwin_ref, vwin_ref, *, l, n_kv, q_per_kv, hd, window, nblk):
    i = pl.program_id(1)
    blk = q_ref.shape[0]
    scale = hd ** -0.5
    rows_q = q_per_kv * blk
    r = lax.broadcasted_iota(jnp.int32, (rows_q, 2 * blk), 0) % blk
    c = lax.broadcasted_iota(jnp.int32, (rows_q, 2 * blk), 1)
    dist = r + blk - c
    ok = (dist >= 0) & (dist <= window) & ((c >= blk) | (i > 0))
    hrow = lax.broadcasted_iota(jnp.int32, (rows_q, 1), 0) // blk
    qw = qw_ref[...]
    kw = kw_ref[...]
    for g in range(n_kv):
        sl = slice(g * hd, (g + 1) * hd)
        kcn = _rms(kc_ref[:, sl], kw)
        kcat = jnp.concatenate([_rms(kp_ref[:, sl], kw), kcn], axis=0).astype(BF16)
        vcat = jnp.concatenate([vp_ref[:, sl], vc_ref[:, sl]], axis=0).astype(BF16)
        heads = [g * q_per_kv + h for h in range(q_per_kv)]
        qs = jnp.concatenate([_rms(q_ref[:, hh * hd:(hh + 1) * hd], qw) for hh in heads], axis=0).astype(BF16)
        s = lax.dot_general(qs, kcat, (((1,), (1,)), ((), ())), preferred_element_type=F32) * scale
        s = jnp.where(ok, s, -1e30)
        sink = jnp.zeros((rows_q, 1), F32)
        for h, hh in enumerate(heads):
            sink = jnp.where(hrow == h, sink_ref[l, hh], sink)
        o = _softmax_sink_pv(s, sink, vcat)
        for h, hh in enumerate(heads):
            o_ref[:, hh * hd:(hh + 1) * hd] = o[h * blk:(h + 1) * blk, :].astype(o_ref.dtype)

        @pl.when(i == nblk - 1)
        def _():
            kwin_ref[:, sl] = kcn
            vwin_ref[:, sl] = vc_ref[:, sl]


def _attn_prompt(z, sinks, q_norm_w, k_norm_w, l, *, R, n_prompt, seq, n_q, n_kv, hd, window):
    blk = window
    nblk = seq // blk
    wa = n_q * hd
    kvw = n_kv * hd
    kcol = wa // kvw
    vcol = kcol + 1

    def cur(col):
        return pl.BlockSpec((blk, kvw), lambda b, i: (b * nblk + i, col))

    def prev(col):
        return pl.BlockSpec((blk, kvw), lambda b, i: (b * nblk + jnp.maximum(i - 1, 0), col))

    L = q_norm_w.shape[0]
    nw = pl.BlockSpec((None, 1, hd), lambda b, i: (l, 0, 0))
    return pl.pallas_call(
        functools.partial(_attn_prompt_kernel, l=l, n_kv=n_kv, q_per_kv=n_q // n_kv, hd=hd,
                          window=window, nblk=nblk),
        grid=(n_prompt, nblk),
        in_specs=[pl.BlockSpec(memory_space=pltpu.SMEM),
                  pl.BlockSpec((blk, wa), lambda b, i: (b * nblk + i, 0)),
                  prev(kcol), cur(kcol), prev(vcol), cur(vcol), nw, nw],
        out_specs=[pl.BlockSpec((blk, wa), lambda b, i: (b * nblk + i, 0)),
                   pl.BlockSpec((None, blk, kvw), lambda b, i: (b, 0, 0)),
                   pl.BlockSpec((None, blk, kvw), lambda b, i: (b, 0, 0))],
        out_shape=[jax.ShapeDtypeStruct((R, wa), BF16),
                   jax.ShapeDtypeStruct((n_prompt, blk, kvw), F32),
                   jax.ShapeDtypeStruct((n_prompt, blk, kvw), F32)],
        compiler_params=_params("parallel", "arbitrary"),
        name="attn_prompt",
    )(sinks, z, z, z, z, z, q_norm_w.reshape(L, 1, hd), k_norm_w.reshape(L, 1, hd))


def _attn_sample_kernel(sink_ref, o_in_ref, q_ref, kn_ref, vn_ref, ck_ref, cv_ref, qw_ref, kw_ref,
                        o_ref, kwin_ref, vwin_ref, keys_s, vals_s, ostage,
                        *, l, nb, T, n_kv, q_per_kv, hd, window):
    del o_in_ref
    W = ck_ref.shape[1]
    scale = hd ** -0.5
    rows_q = q_per_kv * T
    r = lax.broadcasted_iota(jnp.int32, (rows_q, 2 * W), 0) % T
    c = lax.broadcasted_iota(jnp.int32, (rows_q, 2 * W), 1)
    q_pos = PAST_LEN + r
    k_pos = jnp.where(c < W, PAST_LEN - W + c, PAST_LEN + c - W)
    dist = q_pos - k_pos
    ok = (dist >= 0) & (dist <= window) & (k_pos >= 0) & (c < W + T)
    hrow = lax.broadcasted_iota(jnp.int32, (rows_q, 1), 0) // T
    qw = qw_ref[...]
    kw = kw_ref[...]
    keys_s[...] = jnp.zeros_like(keys_s)
    vals_s[...] = jnp.zeros_like(vals_s)

    def step(b, carry):
        rows = pl.ds(pl.multiple_of(b * T, T), T)
        for g in range(n_kv):
            sl = slice(g * hd, (g + 1) * hd)
            ck = ck_ref[b, :, sl]
            cv = cv_ref[b, :, sl]
            knn = _rms(kn_ref[rows, sl], kw)
            vnn = vn_ref[rows, sl]
            keys_s[0:W, :] = ck
            keys_s[W:W + T, :] = knn
            vals_s[0:W, :] = cv
            vals_s[W:W + T, :] = vnn
            qs = jnp.concatenate(
                [_rms(q_ref[rows, (g * q_per_kv + h) * hd:(g * q_per_kv + h + 1) * hd], qw)
                 for h in range(q_per_kv)], axis=0).astype(BF16)
            s = lax.dot_general(qs, keys_s[...].astype(BF16), (((1,), (1,)), ((), ())),
                                preferred_element_type=F32) * scale
            s = jnp.where(ok, s, -1e30)
            sink = jnp.zeros((rows_q, 1), F32)
            for h in range(q_per_kv):
                sink = jnp.where(hrow == h, sink_ref[l, g * q_per_kv + h], sink)
            o = _softmax_sink_pv(s, sink, vals_s[...].astype(BF16))
            for h in range(q_per_kv):
                head = g * q_per_kv + h
                ostage[rows, head * hd:(head + 1) * hd] = o[h * T:(h + 1) * T, :]
            kwin_ref[b, 0:W - T, sl] = ck[T:, :]
            kwin_ref[b, W - T:W, sl] = knn
            vwin_ref[b, 0:W - T, sl] = cv[T:, :]
            vwin_ref[b, W - T:W, sl] = vnn
        return carry

    lax.fori_loop(0, nb, step, 0)
    o_ref[...] = ostage[...].astype(o_ref.dtype)


def _attn_sample(attn_o, z, cache_k, cache_v, sinks, q_norm_w, k_norm_w, l, *, n_sample, T, n_q, n_kv, hd,
                 window):
    R, wa = attn_o.shape
    kvw = n_kv * hd
    rows = n_sample * T
    blk_i = (R - rows) // rows
    kcol = wa // kvw
    L, nb, W = cache_k.shape[:3]
    ck = cache_k.reshape(L, nb, W, kvw)
    cv = cache_v.reshape(L, nb, W, kvw)
    nw = pl.BlockSpec((None, 1, hd), lambda i: (l, 0, 0))
    cache = pl.BlockSpec((None, nb, W, kvw), lambda i: (l, 0, 0, 0))
    win = pl.BlockSpec((nb, W, kvw), lambda i: (0, 0, 0))
    return pl.pallas_call(
        functools.partial(_attn_sample_kernel, l=l, nb=nb, T=T, n_kv=n_kv, q_per_kv=n_q // n_kv, hd=hd,
                          window=window),
        grid=(1,),
        in_specs=[pl.BlockSpec(memory_space=pltpu.SMEM),
                  pl.BlockSpec(memory_space=pl.ANY),
                  pl.BlockSpec((rows, wa), lambda i: (blk_i, 0)),
                  pl.BlockSpec((rows, kvw), lambda i: (blk_i, kcol)),
                  pl.BlockSpec((rows, kvw), lambda i: (blk_i, kcol + 1)),
                  cache, cache, nw, nw],
        out_specs=[pl.BlockSpec((rows, wa), lambda i: (blk_i, 0)), win, win],
        out_shape=[jax.ShapeDtypeStruct((R, wa), BF16),
                   jax.ShapeDtypeStruct((nb, W, kvw), F32),
                   jax.ShapeDtypeStruct((nb, W, kvw), F32)],
        scratch_shapes=[pltpu.VMEM((2 * W, hd), F32), pltpu.VMEM((2 * W, hd), F32),
                        pltpu.VMEM((rows, wa), F32)],
        input_output_aliases={1: 0},
        compiler_params=_params("arbitrary"),
        name="attn_sample",
    )(sinks, attn_o, z, z, z, ck, cv, q_norm_w.reshape(-1, 1, hd), k_norm_w.reshape(-1, 1, hd))


def _lru_gates(xc, wa_ref, ba_ref, wx_ref, bx_ref, lam_ref, a_s, u_s, *, nblk, lb):
    xcb = xc.astype(BF16)
    for n in range(nblk):
        sl = slice(n * lb, (n + 1) * lb)
        ra = jnp.dot(xcb[:, sl], wa_ref[n].astype(BF16), preferred_element_type=F32) + ba_ref[:, sl]
        rx = jnp.dot(xcb[:, sl], wx_ref[n].astype(BF16), preferred_element_type=F32) + bx_ref[:, sl]
        log_a = -LRU_C * jax.nn.sigmoid(ra) * jax.nn.softplus(-lam_ref[:, sl])
        a_s[:, sl] = jnp.exp(log_a)
        th = jnp.tanh(log_a)
        u_s[:, sl] = jnp.sqrt(-2.0 * th / (1.0 - th)) * jax.nn.sigmoid(rx) * xc[:, sl]

def _tile_scan(a, u, T):
    t = lax.broadcasted_iota(jnp.int32, (a.shape[0], 1), 0) % T
    d = 1
    while d < T:
        keep = t >= d
        u = jnp.where(keep, a * pltpu.roll(u, d, axis=0) + u, u)
        a = jnp.where(keep, a * pltpu.roll(a, d, axis=0), a)
        d *= 2
    return a, u


def _lru_prompt_kernel(x_ref, g_ref, cw_ref, cb_ref, wa_ref, ba_ref, wx_ref, bx_ref, lam_ref,
                       o_ref, hT_ref, cnew_ref, xp_s, a_s, u_s, h_s, *, Tb, ntb, cw, nblk, lb):
    tb = pl.program_id(2)
    P = SUBLANES

    @pl.when(tb == 0)
    def _():
        xp_s[0:P, :] = jnp.zeros((P, xp_s.shape[1]), F32)
        h_s[...] = jnp.zeros_like(h_s)

    xp_s[P:P + Tb, :] = x_ref[...]
    xc = cb_ref[...]
    for j in range(cw):
        off = P - (cw - 1) + j
        xc = xc + xp_s[off:off + Tb, :] * cw_ref[j:j + 1, :]
    _lru_gates(xc, wa_ref, ba_ref, wx_ref, bx_ref, lam_ref, a_s, u_s, nblk=nblk, lb=lb)

    a, u = _tile_scan(a_s[...], u_s[...], P)
    a_s[...] = a
    u_s[...] = u

    def step(k, h):
        rr = pl.ds(pl.multiple_of(k * P, P), P)
        hs = u_s[rr, :] + a_s[rr, :] * h
        u_s[rr, :] = hs
        return hs[P - 1:P, :]

    h = lax.fori_loop(0, Tb // P, step, h_s[...])
    h_s[...] = h
    o_ref[...] = (jax.nn.gelu(g_ref[...]) * u_s[...]).astype(o_ref.dtype)
    tail = xp_s[Tb:Tb + P, :]
    xp_s[0:P, :] = tail

    @pl.when(tb == ntb - 1)
    def _():
        hT_ref[...] = h
        cnew_ref[...] = tail


def _lru_sample_kernel(o_in_ref, x_ref, g_ref, h0_ref, buf_ref, cw_ref, cb_ref, wa_ref, ba_ref, wx_ref, bx_ref,
                       lam_ref, o_ref, hs_ref, a_s, u_s, *, T, cw, nblk, lb):
    del o_in_ref
    rows = x_ref.shape[0]
    t = lax.broadcasted_iota(jnp.int32, (rows, 1), 0) % T
    x = x_ref[...]
    buf = buf_ref[...]
    xc = cb_ref[...] + x * cw_ref[cw - 1:cw, :]
    for s in range(1, cw):
        xs = jnp.where(t >= s, pltpu.roll(x, s, axis=0), pltpu.roll(buf, rows - (T - s), axis=0))
        xc = xc + xs * cw_ref[cw - 1 - s:cw - s, :]
    _lru_gates(xc, wa_ref, ba_ref, wx_ref, bx_ref, lam_ref, a_s, u_s, nblk=nblk, lb=lb)
    a, u = _tile_scan(a_s[...], u_s[...], T)
    hs = u + a * h0_ref[...]
    hs_ref[...] = hs
    o_ref[...] = (jax.nn.gelu(g_ref[...]) * hs).astype(o_ref.dtype)


def _lru_weight_args(W, l, wb, index):
    ch = wb // 2
    nlb, lb = W['lru_w_a'].shape[1:3]
    nblk = nlb // 2
    cw = W['conv_w'].shape[1]

    def vec(name):
        return W[name].reshape(W[name].shape[0], 1, wb), pl.BlockSpec((None, 1, ch), index(lambda hf: (l, 0, hf)))

    def blkw(name):
        return W[name], pl.BlockSpec((None, nblk, lb, lb), index(lambda hf: (l, hf, 0, 0)))

    pairs = [(W['conv_w'], pl.BlockSpec((None, cw, ch), index(lambda hf: (l, 0, hf)))),
             vec('conv_b'), blkw('lru_w_a'), vec('lru_b_a'), blkw('lru_w_x'), vec('lru_b_x'), vec('lru_lambda')]
    return [p[0] for p in pairs], [p[1] for p in pairs], dict(cw=cw, nblk=nblk, lb=lb)


def _lru_prompt(z, W, l, *, R, nseq, T, Tb, col_x, col_g, wb):
    ch = wb // 2
    ntb = T // Tb
    wargs, wspecs, kw = _lru_weight_args(W, l, wb, lambda f: (lambda s, hf, t: f(hf)))

    def rowblk(col):
        return pl.BlockSpec((Tb, ch), lambda s, hf, t: (s * ntb + t, col + hf))

    return pl.pallas_call(
        functools.partial(_lru_prompt_kernel, Tb=Tb, ntb=ntb, **kw),
        grid=(nseq, 2, ntb),
        in_specs=[rowblk(col_x), rowblk(col_g)] + wspecs,
        out_specs=[pl.BlockSpec((Tb, ch), lambda s, hf, t: (s * ntb + t, hf)),
                   pl.BlockSpec((None, 1, ch), lambda s, hf, t: (s, 0, hf)),
                   pl.BlockSpec((None, SUBLANES, ch), lambda s, hf, t: (s, 0, hf))],
        out_shape=[jax.ShapeDtypeStruct((R, wb), BF16),
                   jax.ShapeDtypeStruct((nseq, 1, wb), F32),
                   jax.ShapeDtypeStruct((nseq, SUBLANES, wb), F32)],
        scratch_shapes=[pltpu.VMEM((SUBLANES + Tb, ch), F32), pltpu.VMEM((Tb, ch), F32),
                        pltpu.VMEM((Tb, ch), F32), pltpu.VMEM((1, ch), F32)],
        compiler_params=_params("parallel", "parallel", "arbitrary"),
        name="lru_prompt",
    )(z, z, *wargs)


def _lru_sample(lru_o, z, h0_rep, buf_rows, W, l, *, row0, T, col_x, col_g, wb):
    ch = wb // 2
    rows = h0_rep.shape[0]
    rb = row0 // rows
    wargs, wspecs, kw = _lru_weight_args(W, l, wb, lambda f: (lambda hf: f(hf)))
    st = pl.BlockSpec((rows, ch), lambda hf: (0, hf))
    return pl.pallas_call(
        functools.partial(_lru_sample_kernel, T=T, **kw),
        grid=(2,),
        in_specs=[pl.BlockSpec(memory_space=pl.ANY),
                  pl.BlockSpec((rows, ch), lambda hf: (rb, col_x + hf)),
                  pl.BlockSpec((rows, ch), lambda hf: (rb, col_g + hf)), st, st] + wspecs,
        out_specs=[pl.BlockSpec((rows, ch), lambda hf: (rb, hf)), st],
        out_shape=[jax.ShapeDtypeStruct(lru_o.shape, BF16), jax.ShapeDtypeStruct((rows, wb), F32)],
        scratch_shapes=[pltpu.VMEM((rows, ch), F32), pltpu.VMEM((rows, ch), F32)],
        input_output_aliases={0: 0},
        compiler_params=_params("parallel"),
        name="lru_sample",
    )(lru_o, z, z, h0_rep, buf_rows, *wargs)


def _chunk_kernel(*refs, nb, Tc, ngroups, gw, cps, aliased):
    refs = list(refs)
    if aliased:
        refs.pop(0)
    ulo_ref, uhi_ref, vlo_ref, vhi_ref, nw_ref, ws_ref, bst_ref, o_ref, vn_ref = refs
    rows = nb * Tc
    half = vlo_ref.shape[1]
    wc = 2 * half
    v_lo = jax.nn.gelu(vlo_ref[...])
    v_hi = jax.nn.gelu(vhi_ref[...])
    ms = (jnp.sum(v_lo * v_lo, axis=-1, keepdims=True) + jnp.sum(v_hi * v_hi, axis=-1, keepdims=True)) / wc
    inv = lax.rsqrt(ms + EPS)
    vn_halves = (v_lo * inv * nw_ref[:, 0:half], v_hi * inv * nw_ref[:, half:wc])
    u_halves = (ulo_ref, uhi_ref)

    r = lax.broadcasted_iota(jnp.int32, (rows, rows), 0)
    c = lax.broadcasted_iota(jnp.int32, (rows, rows), 1)
    mask = (r // Tc == c // Tc) & (c % Tc <= r % Tc)
    if nb > 1:
        sel = (lax.broadcasted_iota(jnp.int32, (ws_ref.shape[2], rows), 0)
               == lax.broadcasted_iota(jnp.int32, (ws_ref.shape[2], rows), 1) % Tc).astype(BF16)
        bias_rows = jnp.broadcast_to(bst_ref[0:Tc, :][None], (nb, Tc, bst_ref.shape[1])).reshape(rows, -1)
    else:
        bias_rows = bst_ref[0:Tc, :]
    gph = half // gw
    for g in range(ngroups):
        hf, gi = divmod(g, gph)
        sl = slice(gi * gw, (gi + 1) * gw)
        if nb > 1:
            t1 = jnp.broadcast_to(ws_ref[g, 0:Tc, :][None], (nb, Tc, ws_ref.shape[2])).reshape(rows, -1)
            wfull = jnp.dot(t1.astype(BF16), sel, preferred_element_type=F32)
        else:
            wfull = ws_ref[g, 0:Tc, 0:Tc]
        wm = jnp.where(mask, wfull, 0.0).astype(BF16)
        mixed = jnp.dot(wm, vn_halves[hf][:, sl].astype(BF16), preferred_element_type=F32)
        mixed = mixed + bias_rows[:, g:g + 1]
        o_ref[:, g * gw:(g + 1) * gw] = (jax.nn.gelu(u_halves[hf][:, sl]) * mixed).astype(o_ref.dtype)

    if cps == 1:
        vn_ref[:, 0:half] = vn_halves[0]
        vn_ref[:, half:wc] = vn_halves[1]
    else:
        @pl.when(pl.program_id(0) % cps == cps - 1)
        def _():
            vn_ref[:, 0:half] = vn_halves[0]
            vn_ref[:, half:wc] = vn_halves[1]


def _chunk(chunk_o, z, W, l, *, R, row0, nsteps, nb, Tc, cps, col_u, col_v, wc):
    half = wc // 2
    ngroups, chunk = W['chunk_w_s'].shape[1:3]
    gw = wc // ngroups
    rows = nb * Tc
    rb0 = row0 // rows
    aliased = chunk_o is not None

    def rowblk(col):
        return pl.BlockSpec((rows, half), lambda i: (rb0 + i, col))

    args = [z, z, z, z, W['chunk_v_norm_w'].reshape(-1, 1, wc), W['chunk_w_s'],
            jnp.swapaxes(W['chunk_b_s'], 1, 2)]
    specs = [rowblk(col_u), rowblk(col_u + 1), rowblk(col_v), rowblk(col_v + 1),
             pl.BlockSpec((None, 1, wc), lambda i: (l, 0, 0)),
             pl.BlockSpec((None, ngroups, chunk, chunk), lambda i: (l, 0, 0, 0)),
             pl.BlockSpec((None, chunk, ngroups), lambda i: (l, 0, 0))]
    io_alias = {}
    if aliased:
        args.insert(0, chunk_o)
        specs.insert(0, pl.BlockSpec(memory_space=pl.ANY))
        io_alias = {0: 0}
    return pl.pallas_call(
        functools.partial(_chunk_kernel, nb=nb, Tc=Tc, ngroups=ngroups, gw=gw, cps=cps, aliased=aliased),
        grid=(nsteps,),
        in_specs=specs,
        out_specs=[pl.BlockSpec((rows, wc), lambda i: (rb0 + i, 0)),
                   pl.BlockSpec((None, rows, wc), lambda i: (i // cps, 0, 0))],
        out_shape=[jax.ShapeDtypeStruct((R, wc), BF16),
                   jax.ShapeDtypeStruct((nsteps // cps, rows, wc), F32)],
        input_output_aliases=io_alias,
        compiler_params=_params("arbitrary"),
        name="chunk_sample" if aliased else "chunk_prompt",
    )(*args)


def _merge_kernel(a_ref, b_ref, c_ref, wa_ref, wb_ref, wc_ref, ga_ref, gb_ref, gc_ref, o_ref):
    def branch(x_ref, w_ref, g_ref):
        y = jnp.dot(x_ref[...], w_ref[...].astype(BF16), preferred_element_type=F32)
        return jax.nn.sigmoid(g_ref[...]) * y

    m = branch(a_ref, wa_ref, ga_ref) + branch(b_ref, wb_ref, gb_ref) + branch(c_ref, wc_ref, gc_ref)
    o_ref[...] = m.astype(o_ref.dtype)


def _merge(attn_o, lru_o, chunk_o, z, W, l, *, gate_col, tm, tn):
    R = attn_o.shape[0]
    D = W['w_branch_a'].shape[-1]
    g0 = gate_col // tn
    gstep = D // tn

    def xin(a):
        return pl.BlockSpec((tm, a.shape[1]), lambda i, j: (i, 0))

    def win(w):
        return pl.BlockSpec((None, w.shape[1], tn), lambda i, j: (l, 0, j))

    def gin(k):
        return pl.BlockSpec((tm, tn), lambda i, j: (i, g0 + k * gstep + j))

    return pl.pallas_call(
        _merge_kernel,
        grid=(R // tm, D // tn),
        in_specs=[xin(attn_o), xin(lru_o), xin(chunk_o),
                  win(W['w_branch_a']), win(W['w_branch_b']), win(W['w_branch_c']),
                  gin(0), gin(1), gin(2)],
        out_specs=pl.BlockSpec((tm, tn), lambda i, j: (i, j)),
        out_shape=jax.ShapeDtypeStruct((R, D), BF16),
        compiler_params=_params("parallel", "parallel"),
        name="merge",
    )(attn_o, lru_o, chunk_o, W['w_branch_a'], W['w_branch_b'], W['w_branch_c'], z, z, z)


def _glu_kernel(*refs, gated):
    if gated:
        h_ref, wg_ref, wu_ref, gate_ref, o_ref = refs
    else:
        h_ref, wg_ref, wu_ref, o_ref = refs
    h = h_ref[...]
    g = jnp.dot(h, wg_ref[...].astype(BF16), preferred_element_type=F32)
    u = jnp.dot(h, wu_ref[...].astype(BF16), preferred_element_type=F32)
    a = jax.nn.silu(g) * u
    if gated:
        e = pl.program_id(1)
        gate = gate_ref[...]
        lane = lax.broadcasted_iota(jnp.int32, gate.shape, 1)
        a = a * jnp.sum(jnp.where(lane == e, gate, 0.0), axis=-1, keepdims=True)
    o_ref[...] = a.astype(o_ref.dtype)


def _glu(h, w_gate, w_up, l0, n_exp, gate, *, tm, tn):
    R, D = h.shape
    F = w_gate.shape[-1]
    nf = F // tn
    gated = gate is not None
    wspec = pl.BlockSpec((None, D, tn), lambda i, e, j: (l0 + e, 0, j))
    args = [h, w_gate, w_up]
    specs = [pl.BlockSpec((tm, D), lambda i, e, j: (i, 0)), wspec, wspec]
    if gated:
        args.append(gate)
        specs.append(pl.BlockSpec((tm, gate.shape[1]), lambda i, e, j: (i, 0)))
    return pl.pallas_call(
        functools.partial(_glu_kernel, gated=gated),
        grid=(R // tm, n_exp, nf),
        in_specs=specs,
        out_specs=pl.BlockSpec((tm, tn), lambda i, e, j: (i, e * nf + j)),
        out_shape=jax.ShapeDtypeStruct((R, n_exp * F), BF16),
        compiler_params=_params("parallel", "parallel", "parallel"),
        name="glu",
    )(*args)


META_E1, META_E2, META_W1, META_W2, META_R1, META_R2 = range(6)


def _router_kernel(h_ref, w_ref, meta_ref, cnt_ref, carry, *, n_exp):
    i = pl.program_id(0)

    @pl.when(i == 0)
    def _():
        carry[...] = jnp.zeros_like(carry)

    logits = jnp.dot(h_ref[...], w_ref[...].astype(BF16), preferred_element_type=F32)
    tm = logits.shape[0]
    lane = lax.broadcasted_iota(jnp.int32, logits.shape, 1).astype(F32)
    big = float(logits.shape[1])
    l0 = jnp.where(lane < n_exp, logits, -jnp.inf)
    m1 = jnp.max(l0, axis=-1, keepdims=True)
    i1 = jnp.min(jnp.where(l0 == m1, lane, big), axis=-1, keepdims=True)
    l1 = jnp.where(lane == i1, -jnp.inf, l0)
    m2 = jnp.max(l1, axis=-1, keepdims=True)
    i2 = jnp.min(jnp.where(l1 == m2, lane, big), axis=-1, keepdims=True)
    e2 = jnp.exp(m2 - m1)
    den = 1.0 + e2
    hit = ((lane == i1) | (lane == i2)).astype(F32)
    earlier = (lax.broadcasted_iota(jnp.int32, (tm, tm), 0) > lax.broadcasted_iota(jnp.int32, (tm, tm), 1))
    rank = jnp.dot(earlier.astype(BF16), hit.astype(BF16), preferred_element_type=F32) + carry[...]
    r1 = jnp.sum(jnp.where(lane == i1, rank, 0.0), axis=-1, keepdims=True)
    r2 = jnp.sum(jnp.where(lane == i2, rank, 0.0), axis=-1, keepdims=True)
    meta = jnp.zeros_like(logits)
    for k, v in ((META_E1, i1), (META_E2, i2), (META_W1, 1.0 / den), (META_W2, e2 / den), (META_R1, r1),
                 (META_R2, r2)):
        meta = jnp.where(lane == k, v, meta)
    meta_ref[...] = meta
    carry[...] += jnp.sum(hit, axis=0, keepdims=True)
    cnt_ref[...] = carry[...]


def _router(h, w_router_padded, j, *, n_exp, tm):
    R, D = h.shape
    NP = w_router_padded.shape[-1]
    return pl.pallas_call(
        functools.partial(_router_kernel, n_exp=n_exp),
        grid=(R // tm,),
        in_specs=[pl.BlockSpec((tm, D), lambda i: (i, 0)),
                  pl.BlockSpec((None, D, NP), lambda i: (j, 0, 0))],
        out_specs=[pl.BlockSpec((tm, NP), lambda i: (i, 0)), pl.BlockSpec((1, NP), lambda i: (0, 0))],
        out_shape=[jax.ShapeDtypeStruct((R, NP), F32), jax.ShapeDtypeStruct((1, NP), F32)],
        scratch_shapes=[pltpu.VMEM((1, NP), F32)],
        compiler_params=_params("arbitrary"),
        name="router",
    )(h, w_router_padded)


def _row_copies_wait(src_like, dst_like, sem, n):
    for _ in range(n):
        pltpu.make_async_copy(src_like, dst_like, sem).wait()


def _dispatch_kernel(p1_ref, p2_ref, h_ref, xs_ref, sem):
    tm = h_ref.shape[0]
    i0 = pl.program_id(0) * tm

    def body(i, c):
        pltpu.make_async_copy(h_ref.at[i], xs_ref.at[p1_ref[i0 + i]], sem).start(priority=0)
        pltpu.make_async_copy(h_ref.at[i], xs_ref.at[p2_ref[i0 + i]], sem).start(priority=1)
        return c

    lax.fori_loop(0, tm, body, 0)
    _row_copies_wait(h_ref, xs_ref.at[pl.ds(0, tm)], sem, TOP_K)


def _dispatch(h_rows, p1, p2, *, n_slots, tm):
    R, C, _ = h_rows.shape
    return pl.pallas_call(
        _dispatch_kernel,
        grid_spec=pltpu.PrefetchScalarGridSpec(
            num_scalar_prefetch=2, grid=(R // tm,),
            in_specs=[pl.BlockSpec((tm, C, LANES), lambda i, p1, p2: (i, 0, 0))],
            out_specs=pl.BlockSpec(memory_space=pl.ANY),
            scratch_shapes=[pltpu.SemaphoreType.DMA]),
        out_shape=jax.ShapeDtypeStruct((n_slots, C, LANES), F32),
        compiler_params=_params("arbitrary"),
        name="moe_dispatch",
    )(p1, p2, h_rows)


def _per_live_rows(nv, tile, sub, body):
    for m in range(sub, tile + 1, sub):
        pl.when((nv > m - sub) & (nv <= m))(functools.partial(body, m))


def _moe_glu_kernel(te_ref, nv_ref, nu_ref, xs_ref, wg_ref, wu_ref, a_ref, xb_s, *, sub):
    t = pl.program_id(0)
    j = pl.program_id(1)
    nv = nv_ref[t]
    tile, nchunk = xs_ref.shape[0], xs_ref.shape[1]

    @pl.when(t < nu_ref[0])
    def _():
        wg = wg_ref[...].astype(BF16)
        wu = wu_ref[...].astype(BF16)

        def body(m):
            rows = slice(0, m)

            @pl.when(j == 0)
            def _():
                live = lax.broadcasted_iota(jnp.int32, (m, 1), 0) < nv
                x2d = xs_ref[rows, :, :].reshape(m, nchunk * LANES)
                xb_s[rows, :] = jnp.where(live, x2d, 0.0).astype(BF16)

            x = xb_s[rows, :]
            g = jnp.dot(x, wg, preferred_element_type=F32)
            u = jnp.dot(x, wu, preferred_element_type=F32)
            a_ref[rows, :] = (jax.nn.silu(g) * u).astype(a_ref.dtype)
            if m < tile:
                a_ref[m:tile, :] = jnp.zeros((tile - m, a_ref.shape[1]), a_ref.dtype)

        _per_live_rows(nv, tile, sub, body)


def _moe_down_kernel(te_ref, nv_ref, nu_ref, a_ref, wd_ref, y_ref, acc_s, *, sub, nk):
    t = pl.program_id(0)
    k = pl.program_id(1)
    nv = nv_ref[t]
    tile, nchunk = y_ref.shape[0], y_ref.shape[1]

    @pl.when(t < nu_ref[0])
    def _():
        wd = wd_ref[...].astype(BF16)

        def body(m):
            rows = slice(0, m)
            part = jnp.dot(a_ref[rows, :], wd, preferred_element_type=F32)

            @pl.when(k == 0)
            def _():
                acc_s[rows, :] = part

            @pl.when(k > 0)
            def _():
                acc_s[rows, :] += part

            @pl.when(k == nk - 1)
            def _():
                y_ref[rows, :, :] = acc_s[rows, :].reshape(m, nchunk, LANES)
                if m < tile:
                    y_ref[m:tile, :, :] = jnp.zeros((tile - m, nchunk, LANES), F32)

        _per_live_rows(nv, tile, sub, body)


def _moe_ffn(xs, w_gate, w_up, w_down, l0, tile_expert, tile_valid, n_used, *, tile, sub, tn, tk):
    n_slots, C, _ = xs.shape
    D = C * LANES
    F = w_gate.shape[-1]
    nf, nk = F // tn, F // tk
    nt = n_slots // tile

    def live_t(t, nu):
        return jnp.minimum(t, nu[0] - 1)

    def live_step(t, j, nu, n):
        return jnp.where(t < nu[0], j, n - 1)

    def expert(t, te, nu):
        return l0 + te[live_t(t, nu)]

    row_tile = pl.BlockSpec((tile, C, LANES), lambda t, j, te, nv, nu: (live_t(t, nu), 0, 0))
    w_col = pl.BlockSpec((None, D, tn), lambda t, j, te, nv, nu: (expert(t, te, nu), 0, live_step(t, j, nu, nf)))
    a = pl.pallas_call(
        functools.partial(_moe_glu_kernel, sub=sub),
        grid_spec=pltpu.PrefetchScalarGridSpec(
            num_scalar_prefetch=3, grid=(nt, nf),
            in_specs=[row_tile, w_col, w_col],
            out_specs=pl.BlockSpec((tile, tn), lambda t, j, te, nv, nu: (live_t(t, nu), live_step(t, j, nu, nf))),
            scratch_shapes=[pltpu.VMEM((tile, D), BF16)]),
        out_shape=jax.ShapeDtypeStruct((n_slots, F), BF16),
        compiler_params=_params("arbitrary", "arbitrary"),
        name="moe_glu",
    )(tile_expert, tile_valid, n_used, xs, w_gate, w_up)
    return pl.pallas_call(
        functools.partial(_moe_down_kernel, sub=sub, nk=nk),
        grid_spec=pltpu.PrefetchScalarGridSpec(
            num_scalar_prefetch=3, grid=(nt, nk),
            in_specs=[pl.BlockSpec((tile, tk), lambda t, k, te, nv, nu: (live_t(t, nu), live_step(t, k, nu, nk))),
                      pl.BlockSpec((None, tk, D),
                                   lambda t, k, te, nv, nu: (expert(t, te, nu), live_step(t, k, nu, nk), 0))],
            out_specs=pl.BlockSpec((tile, C, LANES), lambda t, j, te, nv, nu: (live_t(t, nu), 0, 0),
                                   pipeline_mode=pl.Buffered(1)),
            scratch_shapes=[pltpu.VMEM((tile, D), F32)]),
        out_shape=jax.ShapeDtypeStruct((n_slots, C, LANES), F32),
        compiler_params=_params("arbitrary", "arbitrary"),
        name="moe_down",
    )(tile_expert, tile_valid, n_used, a, w_down)


def _combine_kernel(p1_ref, p2_ref, meta_ref, y_ref, f_ref, a_s, b_s, sem):
    tm, nchunk = a_s.shape[0], a_s.shape[1]
    i0 = pl.program_id(0) * tm

    def body(i, c):
        pltpu.make_async_copy(y_ref.at[p1_ref[i0 + i]], a_s.at[i], sem).start(priority=0)
        pltpu.make_async_copy(y_ref.at[p2_ref[i0 + i]], b_s.at[i], sem).start(priority=1)
        return c

    lax.fori_loop(0, tm, body, 0)
    _row_copies_wait(y_ref.at[pl.ds(0, tm)], a_s, sem, TOP_K)
    w1 = meta_ref[:, META_W1:META_W1 + 1]
    w2 = meta_ref[:, META_W2:META_W2 + 1]
    d = nchunk * LANES
    f_ref[...] = w1 * a_s[...].reshape(tm, d) + w2 * b_s[...].reshape(tm, d)


def _combine(y, meta, p1, p2, *, tm):
    _, C, _ = y.shape
    R, NP = meta.shape
    return pl.pallas_call(
        _combine_kernel,
        grid_spec=pltpu.PrefetchScalarGridSpec(
            num_scalar_prefetch=2, grid=(R // tm,),
            in_specs=[pl.BlockSpec((tm, NP), lambda i, p1, p2: (i, 0)),
                      pl.BlockSpec(memory_space=pl.ANY)],
            out_specs=pl.BlockSpec((tm, C * LANES), lambda i, p1, p2: (i, 0)),
            scratch_shapes=[pltpu.VMEM((tm, C, LANES), F32), pltpu.VMEM((tm, C, LANES), F32),
                            pltpu.SemaphoreType.DMA]),
        out_shape=jax.ShapeDtypeStruct((R, C * LANES), F32),
        compiler_params=_params("arbitrary"),
        name="moe_combine",
    )(p1, p2, meta, y)


def _moe(h, h_rows, w_router_padded, w_gate, w_up, w_down, j, *, n_exp, tm_router, tile, sub, tn, tk, tm_rows):
    R = h.shape[0]
    meta, counts = _router(h, w_router_padded, j, n_exp=n_exp, tm=tm_router)
    cnt = counts[0, :n_exp].astype(jnp.int32)
    ntile = (cnt + tile - 1) // tile
    tend = jnp.cumsum(ntile)
    tstart = tend - ntile
    n_tiles = -(-TOP_K * R // tile) + n_exp
    e1, e2 = meta[:, META_E1].astype(jnp.int32), meta[:, META_E2].astype(jnp.int32)
    p1 = tstart[e1] * tile + meta[:, META_R1].astype(jnp.int32)
    p2 = tstart[e2] * tile + meta[:, META_R2].astype(jnp.int32)
    tid = jnp.arange(n_tiles, dtype=jnp.int32)
    tile_expert = jnp.minimum(jnp.sum(tid[:, None] >= tend[None, :], axis=1), n_exp - 1).astype(jnp.int32)
    tile_valid = jnp.clip(cnt[tile_expert] - (tid - tstart[tile_expert]) * tile, 0, tile)
    tile_valid = jnp.where(tid < tend[-1], tile_valid, 0).astype(jnp.int32)
    xs = _dispatch(h_rows, p1, p2, n_slots=n_tiles * tile, tm=tm_rows)
    y = _moe_ffn(xs, w_gate, w_up, w_down, j * n_exp, tile_expert, tile_valid, tend[-1:].astype(jnp.int32),
                 tile=tile, sub=sub, tn=tn, tk=tk)
    return _combine(y, meta, p1, p2, tm=tm_rows)


def kernel(x_prompt, x_sample, cache_k_win, cache_v_win, state_rglru_h, state_conv, c_prompt, c_sample, norm1_w, norm2_w, w_ada, b_ada, w_in, q_norm_w, k_norm_w, attn_sinks, conv_w, conv_b, lru_w_a, lru_b_a, lru_w_x, lru_b_x, lru_lambda, chunk_v_norm_w, chunk_w_s, chunk_b_s, w_branch_a, w_branch_b, w_branch_c, w_out, ffn_w_gate, ffn_w_up, ffn_w_down, moe_w_router, moe_w_gate, moe_w_up, moe_w_down):
    Bp, S, D = x_prompt.shape
    Bs, Ts, _ = x_sample.shape
    L = w_in.shape[0]
    window, n_kv, hd = cache_k_win.shape[2:]
    n_q = attn_sinks.shape[1]
    wa, kvw = n_q * hd, n_kv * hd
    wb = conv_w.shape[-1]
    wc = chunk_v_norm_w.shape[-1]
    chunk = chunk_w_s.shape[-1]
    cwid = conv_w.shape[1]
    n_exp = moe_w_router.shape[-1]
    assert Ts == SUBLANES and S % (Bs * Ts) == 0 and S % chunk == 0 and S % window == 0
    Rp, Rs = Bp * S, Bs * Ts
    R = Rp + Rs
    half = wb // 2
    assert wb == wc and wa % half == 0 and kvw % half == 0 or True
    col_lx = (wa + 2 * kvw) // half
    col_lg = col_lx + 2
    col_cu = col_lg + 2
    col_cv = col_cu + 2
    gate_col = wa + 2 * kvw + 2 * wb + 2 * wc

    W = dict(conv_w=conv_w, conv_b=conv_b, lru_w_a=lru_w_a, lru_b_a=lru_b_a, lru_w_x=lru_w_x, lru_b_x=lru_b_x,
             lru_lambda=lru_lambda, chunk_v_norm_w=chunk_v_norm_w, chunk_w_s=chunk_w_s, chunk_b_s=chunk_b_s,
             w_branch_a=w_branch_a, w_branch_b=w_branch_b, w_branch_c=w_branch_c)

    n_c = Bp + Bs
    n_c_pad = -(-n_c // SUBLANES) * SUBLANES
    c_all = jnp.concatenate([c_prompt, c_sample, jnp.zeros((n_c_pad - n_c, D), F32)], axis=0)
    mod = _ada(c_all, w_ada, b_ada)
    mod_p = jnp.repeat(mod[:, :Bp], SUBLANES, axis=1)
    mod_s = jnp.repeat(mod[:, Bp:n_c], Ts, axis=1)

    x = jnp.concatenate([x_prompt.reshape(Rp, D), x_sample.reshape(Rs, D)], axis=0)
    rn = functools.partial(_resid_norm, seq=S, n_prompt=Bp)
    tm_big = _pick(R, 1408, 128) if R % 128 == 0 else _pick(R, 1408, 16)
    tm_mid = _pick(R, 768, 128) if R % 128 == 0 else _pick(R, 768, 16)
    tm_wide = _pick(R, 2816, 128) if R % 128 == 0 else _pick(R, 2816, 16)
    w_router_p = jnp.pad(moe_w_router, ((0, 0), (0, 0), (0, LANES - n_exp)))
    moe_gate_w = moe_w_gate.reshape((-1,) + moe_w_gate.shape[2:])
    moe_up_w = moe_w_up.reshape((-1,) + moe_w_up.shape[2:])
    moe_down_w = moe_w_down.reshape((-1,) + moe_w_down.shape[2:])
    moe_sub = MOE_SUB_ROWS if TOP_K * R >= n_exp * MOE_SUB_ROWS * MOE_TILE_SUBS else 4 * SUBLANES

    buf_rows = jnp.pad(state_conv, ((0, 0), (0, 0), (Ts - (cwid - 1), 0), (0, 0))).reshape(L, Rs, wb)
    h0_rep = jnp.repeat(state_rglru_h, Ts, axis=1)

    _, h = rn(x, None, mod_p, mod_s, norm1_w, 0, gate_k=None, scale_k=(0, 1), shift_k=(0, 0))
    outs = []
    for l in range(L):
        z = _matmul(h, w_in, l, out_dtype=F32, tm=tm_wide, tn=_pick(w_in.shape[-1], 512, half), tk=D)
        attn_o, kwp, vwp = _attn_prompt(z, attn_sinks, q_norm_w, k_norm_w, l, R=R, n_prompt=Bp, seq=S,
                                        n_q=n_q, n_kv=n_kv, hd=hd, window=window)
        attn_o, kws, vws = _attn_sample(attn_o, z, cache_k_win, cache_v_win, attn_sinks, q_norm_w, k_norm_w, l,
                                        n_sample=Bs, T=Ts, n_q=n_q, n_kv=n_kv, hd=hd, window=window)
        lru_o, hp, cp8 = _lru_prompt(z, W, l, R=R, nseq=Bp, T=S, Tb=_pick(S, 512, SUBLANES),
                                     col_x=col_lx, col_g=col_lg, wb=wb)
        lru_o, hs_all = _lru_sample(lru_o, z, h0_rep[l], buf_rows[l], W, l, row0=Rp, T=Ts,
                                    col_x=col_lx, col_g=col_lg, wb=wb)
        chunk_o, vnp = _chunk(None, z, W, l, R=R, row0=0, nsteps=Rp // chunk, nb=1, Tc=chunk,
                              cps=S // chunk, col_u=col_cu, col_v=col_cv, wc=wc)
        chunk_o, vns = _chunk(chunk_o, z, W, l, R=R, row0=Rp, nsteps=1, nb=Bs, Tc=Ts, cps=1,
                              col_u=col_cu, col_v=col_cv, wc=wc)
        m = _merge(attn_o, lru_o, chunk_o, z, W, l, gate_col=gate_col, tm=tm_big, tn=_pick(D, 256, LANES))
        y = _matmul(m, w_out, l, out_dtype=F32, tm=tm_wide, tn=_pick(D, 512, LANES), tk=D)
        j = l // 2
        if l % 2 == 0:
            x, h = rn(x, y, mod_p, mod_s, norm2_w, l, gate_k=(l, 2), scale_k=(l, 4), shift_k=(l, 3))
            a = _glu(h, ffn_w_gate, ffn_w_up, j, 1, None, tm=tm_wide, tn=_pick(ffn_w_gate.shape[-1], 256, LANES))
            f = _matmul(a, ffn_w_down, j, out_dtype=F32, tm=tm_big, tn=_pick(D, 1024, LANES),
                        tk=_pick(a.shape[1], 1408, LANES))
        else:
            x, h, h_rows = rn(x, y, mod_p, mod_s, norm2_w, l, gate_k=(l, 2), scale_k=(l, 4), shift_k=(l, 3),
                              rows_out=True)
            f = _moe(h, h_rows, w_router_p, moe_gate_w, moe_up_w, moe_down_w, j, n_exp=n_exp, tm_router=tm_mid,
                     tile=MOE_TILE_SUBS * moe_sub, sub=moe_sub, tn=_pick(moe_gate_w.shape[-1], 512, LANES),
                     tk=_pick(moe_gate_w.shape[-1], 1024, LANES), tm_rows=Rs)
        if l + 1 < L:
            x, h = rn(x, f, mod_p, mod_s, norm1_w, l + 1, gate_k=(l, 5), scale_k=(l + 1, 1), shift_k=(l + 1, 0))
        else:
            x, _ = rn(x, f, mod_p, mod_s, None, l, gate_k=(l, 5), scale_k=None, shift_k=None)
        conv_p = cp8[:, SUBLANES - (cwid - 1):, :]
        lx_s = z[Rp:, col_lx * half:col_lx * half + wb].reshape(Bs, Ts, wb)
        conv_s = jnp.concatenate([state_conv[l], lx_s], axis=1)[:, -(cwid - 1):]
        hs = hs_all.reshape(Bs, Ts, wb)[:, Ts - 1]
        outs.append((kwp.reshape(Bp, window, n_kv, hd), vwp.reshape(Bp, window, n_kv, hd), hp.reshape(Bp, wb),
                     conv_p, vnp,
                     kws.reshape(Bs, window, n_kv, hd), vws.reshape(Bs, window, n_kv, hd), hs.reshape(Bs, wb),
                     conv_s, vns.reshape(Bs, Ts, wc)))
    st = [jnp.stack(s) for s in zip(*outs)]
    y_prompt = x[:Rp].reshape(Bp, S, D)
    y_sample = x[Rp:].reshape(Bs, Ts, D)
    return (y_prompt, y_sample, st[0], st[1], st[2], st[3], st[4], st[5], st[6], st[7], st[8], st[9])
```
